```python
import math
import jax, jax.numpy as jnp
from jax import lax
import numpy as np

D_MODEL = 1024
BATCH = 16
SEQ = 256
DEPTH = 4
DEC_BATCH = 4
DEC_SEQ = 1024
PAST_LEN = 512

GRID_W = 64
N_MIXERS = 2
N_S5_LAYERS = (DEPTH + 1) // 2
N_ATTN_LAYERS = DEPTH // 2
GROUP_CH = 16
N_GROUPS = D_MODEL // GROUP_CH
STATE_DIM = 64
DT_MIN = 0.001
DT_MAX = 0.1
HEAD_DIM = 128
N_HEADS = D_MODEL // HEAD_DIM
N_KV_HEADS = 2
KV_REP = N_HEADS // N_KV_HEADS
D_Q = N_HEADS * HEAD_DIM
D_KV = N_KV_HEADS * HEAD_DIM
QKV_DIM = D_Q + 2 * D_KV
Q_BLOCK = 128
ROPE_THETA = 10000.0
AXIS_PAIRS = HEAD_DIM // 4
ATTN_SCALE = HEAD_DIM ** -0.5
D_FF = -(-8 * D_MODEL // (3 * 256)) * 256
DEEPNORM_ALPHA = (2.0 * DEPTH) ** 0.25
DEEPNORM_BETA = (8.0 * DEPTH) ** -0.25
LN_EPS = 1e-6
RMS_EPS = 1e-6

kernel_name = "hybrid_s5_gqa_prefix_diffusion_step"


def layer_norm(x, g, b):
    xf = x.astype(jnp.float32)
    mu = xf.mean(-1, keepdims=True)
    var = jnp.square(xf - mu).mean(-1, keepdims=True)
    return ((xf - mu) * lax.rsqrt(var + LN_EPS) * g + b).astype(x.dtype)


def rms_norm(x, g):
    xf = x.astype(jnp.float32)
    return (xf * lax.rsqrt(jnp.mean(xf * xf, -1, keepdims=True) + RMS_EPS) * g).astype(x.dtype)


def adaln(cvec, w_mod, b_mod):
    m = jax.nn.silu(cvec) @ w_mod + b_mod
    return jnp.split(m[:, None, :], 6, axis=-1)


def modulate(x, shift, scale):
    return x * (1.0 + scale) + shift


def swiglu(h, w_in, w_out):
    gate, up = jnp.split(h @ w_in, 2, axis=-1)
    return (jax.nn.silu(gate) * up) @ w_out


def s5_discretize(a_re, a_im, log_dt, b_re, b_im):
    dt = jnp.exp(log_dt)[:, None]
    mag = jnp.exp(dt * a_re)
    ab_re = mag * jnp.cos(dt * a_im)
    ab_im = mag * jnp.sin(dt * a_im)
    den = a_re * a_re + a_im * a_im
    nr = ab_re - 1.0
    k_re = (nr * a_re + ab_im * a_im) / den
    k_im = (ab_im * a_re - nr * a_im) / den
    bb_re = k_re[..., None] * b_re - k_im[..., None] * b_im
    bb_im = k_re[..., None] * b_im + k_im[..., None] * b_re
    return ab_re, ab_im, bb_re, bb_im


def _complex_affine_combine(e1, e2):
    a1r, a1i, b1r, b1i = e1
    a2r, a2i, b2r, b2i = e2
    return (a2r * a1r - a2i * a1i,
            a2r * a1i + a2i * a1r,
            a2r * b1r - a2i * b1i + b2r,
            a2r * b1i + a2i * b1r + b2i)


def s5_direction(u, h0_re, h0_im, a_re, a_im, log_dt, b_re, b_im, c_re, c_im, reverse):
    ab_re, ab_im, bb_re, bb_im = s5_discretize(a_re, a_im, log_dt, b_re, b_im)
    bu_re = jnp.einsum('blgh,gph->blgp', u, bb_re)
    bu_im = jnp.einsum('blgh,gph->blgp', u, bb_im)
    if reverse:
        bu_re, bu_im = jnp.flip(bu_re, 1), jnp.flip(bu_im, 1)
    bu_re = bu_re.at[:, 0].add(ab_re * h0_re - ab_im * h0_im)
    bu_im = bu_im.at[:, 0].add(ab_re * h0_im + ab_im * h0_re)
    ar = jnp.broadcast_to(ab_re, bu_re.shape)
    ai = jnp.broadcast_to(ab_im, bu_im.shape)
    _, _, s_re, s_im = lax.associative_scan(_complex_affine_combine, (ar, ai, bu_re, bu_im), axis=1)
    fin_re, fin_im = s_re[:, -1], s_im[:, -1]
    if reverse:
        s_re, s_im = jnp.flip(s_re, 1), jnp.flip(s_im, 1)
    y = jnp.einsum('blgp,ghp->blgh', s_re, c_re) - jnp.einsum('blgp,ghp->blgh', s_im, c_im)
    return y, fin_re, fin_im


def s5_mixer(h, h0, w_in, a_re, a_im, log_dt, b_re, b_im, c_re, c_im, d_skip, w_glu, w_out):
    b, l, _ = h.shape
    u = (h @ w_in).reshape(b, l, N_GROUPS, GROUP_CH)
    y = d_skip.reshape(N_GROUPS, GROUP_CH) * u
    finals = []
    for d in range(2):
        yd, fr, fi = s5_direction(u, h0[:, d, 0], h0[:, d, 1], a_re[d], a_im[d], log_dt[d],
                                  b_re[d], b_im[d], c_re[d], c_im[d], reverse=(d == 1))
        y = y + yd
        finals.append(jnp.stack([fr, fi], axis=1))
    z = jax.nn.gelu(y.reshape(b, l, D_MODEL))
    val, gate = jnp.split(z @ w_glu, 2, axis=-1)
    return (val * jax.nn.sigmoid(gate)) @ w_out, jnp.stack(finals, axis=1)


def attn_project(h, w_qkv, q_gain, k_gain):
    b, l, _ = h.shape
    qkv = h @ w_qkv
    q = qkv[..., :D_Q].reshape(b, l, N_HEADS, HEAD_DIM)
    k = qkv[..., D_Q:D_Q + D_KV].reshape(b, l, N_KV_HEADS, HEAD_DIM)
    v = qkv[..., D_Q + D_KV:].reshape(b, l, N_KV_HEADS, HEAD_DIM)
    return rms_norm(q, q_gain), rms_norm(k, k_gain), v


def _rotate(x, ang):
    cos = jnp.cos(ang)[:, None, :].astype(x.dtype)
    sin = jnp.sin(ang)[:, None, :].astype(x.dtype)
    x1, x2 = x[..., :AXIS_PAIRS], x[..., AXIS_PAIRS:]
    return jnp.concatenate([x1 * cos - x2 * sin, x2 * cos + x1 * sin], axis=-1)


def axial_rope(x):
    l = x.shape[1]
    rows = l // GRID_W
    row = jnp.repeat(jnp.arange(rows, dtype=jnp.float32), GRID_W)
    col = jnp.tile(jnp.arange(GRID_W, dtype=jnp.float32), rows)
    inv = ROPE_THETA ** (-jnp.arange(AXIS_PAIRS, dtype=jnp.float32) / AXIS_PAIRS)
    half = HEAD_DIM // 2
    return jnp.concatenate([_rotate(x[..., :half], row[:, None] * inv),
                            _rotate(x[..., half:], col[:, None] * inv)], axis=-1)


def blocked_attention(q, k, v):
    b, lq = q.shape[0], q.shape[1]
    nb = lq // Q_BLOCK
    qb = q.reshape(b, nb, Q_BLOCK, N_KV_HEADS, KV_REP, HEAD_DIM).transpose(1, 0, 2, 3, 4, 5)

    def one_block(qblk):
        s = jnp.einsum('bqgrd,bkgd->bgrqk', qblk, k).astype(jnp.float32) * ATTN_SCALE
        p = jax.nn.softmax(s, axis=-1).astype(v.dtype)
        return jnp.einsum('bgrqk,bkgd->bqgrd', p, v)

    o = lax.map(one_block, qb)
    return o.transpose(1, 0, 2, 3, 4, 5).reshape(b, lq, D_Q)


def setup_inputs(seed: int = 0) -> dict:
    key = jax.random.key(seed)
    ks = jax.random.split(key, 32)
    f32 = jnp.float32

    def nrm(k, shape, s=1.0):
        return s * jax.random.normal(k, shape, f32)

    s5_shape = (N_S5_LAYERS, 2, N_GROUPS, STATE_DIM)
    n_idx = jnp.arange(STATE_DIM, dtype=f32)
    return {
        "x_prompt": nrm(ks[0], (BATCH, SEQ, D_MODEL)),
        "x_sample": nrm(ks[1], (DEC_BATCH, DEC_SEQ, D_MODEL)),
        "c": nrm(ks[2], (DEC_BATCH, D_MODEL)),
        "cache_k": nrm(ks[3], (DEC_BATCH, N_ATTN_LAYERS, PAST_LEN, N_KV_HEADS, HEAD_DIM)),
        "cache_v": nrm(ks[4], (DEC_BATCH, N_ATTN_LAYERS, PAST_LEN, N_KV_HEADS, HEAD_DIM)),
        "state_s5": nrm(ks[5], (DEC_BATCH, N_S5_LAYERS, 2, 2, N_GROUPS, STATE_DIM), 0.1),
        "c_ctx": nrm(ks[6], (D_MODEL,)),
        "w_mod": nrm(ks[7], (DEPTH, D_MODEL, 6 * D_MODEL), 0.5 * D_MODEL ** -0.5),
        "b_mod": nrm(ks[8], (DEPTH, 6 * D_MODEL), 0.01),
        "ln_g": 1.0 + nrm(ks[9], (DEPTH, 2, D_MODEL), 0.02),
        "ln_b": nrm(ks[10], (DEPTH, 2, D_MODEL), 0.02),
        "w_s5_in": nrm(ks[11], (N_S5_LAYERS, D_MODEL, D_MODEL), D_MODEL ** -0.5),
        "s5_a_re": -0.5 * jnp.exp(nrm(ks[12], s5_shape, 0.05)),
        "s5_a_im": jnp.pi * n_idx + nrm(ks[13], s5_shape, 0.01),
        "s5_log_dt": jax.random.uniform(ks[14], (N_S5_LAYERS, 2, N_GROUPS), f32,
                                        math.log(DT_MIN), math.log(DT_MAX)),
        "s5_b_re": nrm(ks[15], (N_S5_LAYERS, 2, N_GROUPS, STATE_DIM, GROUP_CH), (2 * GROUP_CH) ** -0.5),
        "s5_b_im": nrm(ks[16], (N_S5_LAYERS, 2, N_GROUPS, STATE_DIM, GROUP_CH), (2 * GROUP_CH) ** -0.5),
        "s5_c_re": nrm(ks[17], (N_S5_LAYERS, 2, N_GROUPS, GROUP_CH, STATE_DIM), STATE_DIM ** -0.5),
        "s5_c_im": nrm(ks[18], (N_S5_LAYERS, 2, N_GROUPS, GROUP_CH, STATE_DIM), STATE_DIM ** -0.5),
        "s5_d": nrm(ks[19], (N_S5_LAYERS, D_MODEL)),
        "w_s5_glu": nrm(ks[20], (N_S5_LAYERS, D_MODEL, 2 * D_MODEL), D_MODEL ** -0.5),
        "w_s5_out": nrm(ks[21], (N_S5_LAYERS, D_MODEL, D_MODEL), DEEPNORM_BETA * D_MODEL ** -0.5),
        "w_qkv": nrm(ks[22], (N_ATTN_LAYERS, D_MODEL, QKV_DIM), D_MODEL ** -0.5),
        "q_norm_g": 1.0 + nrm(ks[23], (N_ATTN_LAYERS, HEAD_DIM), 0.02),
        "k_norm_g": 1.0 + nrm(ks[24], (N_ATTN_LAYERS, HEAD_DIM), 0.02),
        "w_o": nrm(ks[25], (N_ATTN_LAYERS, D_Q, D_MODEL), DEEPNORM_BETA * D_Q ** -0.5),
        "w_ffn_in": nrm(ks[26], (DEPTH, D_MODEL, 2 * D_FF), D_MODEL ** -0.5),
        "w_ffn_out": nrm(ks[27], (DEPTH, D_FF, D_MODEL), DEEPNORM_BETA * D_FF ** -0.5),
    }


def reference(x_prompt, x_sample, c, cache_k, cache_v, state_s5, c_ctx, w_mod, b_mod, ln_g, ln_b,
              w_s5_in, s5_a_re, s5_a_im, s5_log_dt, s5_b_re, s5_b_im, s5_c_re, s5_c_im, s5_d,
              w_s5_glu, w_s5_out, w_qkv, q_norm_g, k_norm_g, w_o, w_ffn_in, w_ffn_out):
    xp, xs = x_prompt, x_sample
    ctx_state0 = jnp.zeros((xp.shape[0], 2, 2, N_GROUPS, STATE_DIM), xp.dtype)
    new_k, new_v, new_s = [], [], []
    for layer in range(DEPTH):
        j = layer // N_MIXERS
        sh1p, sc1p, g1p, sh2p, sc2p, g2p = adaln(c_ctx[None, :], w_mod[layer], b_mod[layer])
        sh1s, sc1s, g1s, sh2s, sc2s, g2s = adaln(c, w_mod[layer], b_mod[layer])
        hp = modulate(xp, sh1p, sc1p)
        hs = modulate(xs, sh1s, sc1s)
        if layer % N_MIXERS == 0:
            prm = (w_s5_in[j], s5_a_re[j], s5_a_im[j], s5_log_dt[j], s5_b_re[j], s5_b_im[j],
                   s5_c_re[j], s5_c_im[j], s5_d[j], w_s5_glu[j], w_s5_out[j])
            mp, fin = s5_mixer(hp, ctx_state0, *prm)
            ms, _ = s5_mixer(hs, state_s5[:, j], *prm)
            new_s.append(fin)
        else:
            qp, kp, vp = attn_project(hp, w_qkv[j], q_norm_g[j], k_norm_g[j])
            mp = blocked_attention(qp, kp, vp) @ w_o[j]
            qs, ks_, vs = attn_project(hs, w_qkv[j], q_norm_g[j], k_norm_g[j])
            qs, ks_ = axial_rope(qs), axial_rope(ks_)
            k_all = jnp.concatenate([ks_, cache_k[:, j]], axis=1)
            v_all = jnp.concatenate([vs, cache_v[:, j]], axis=1)
            ms = blocked_attention(qs, k_all, v_all) @ w_o[j]
            new_k.append(kp)
            new_v.append(vp)
        xp = layer_norm(DEEPNORM_ALPHA * xp + g1p * mp, ln_g[layer, 0], ln_b[layer, 0])
        xs = layer_norm(DEEPNORM_ALPHA * xs + g1s * ms, ln_g[layer, 0], ln_b[layer, 0])
        fp = swiglu(modulate(xp, sh2p, sc2p), w_ffn_in[layer], w_ffn_out[layer])
        fs = swiglu(modulate(xs, sh2s, sc2s), w_ffn_in[layer], w_ffn_out[layer])
        xp = layer_norm(DEEPNORM_ALPHA * xp + g2p * fp, ln_g[layer, 1], ln_b[layer, 1])
        xs = layer_norm(DEEPNORM_ALPHA * xs + g2s * fs, ln_g[layer, 1], ln_b[layer, 1])
    y_prompt, y_sample = xp, xs
    new_cache_k = jnp.stack(new_k, axis=1)
    new_cache_v = jnp.stack(new_v, axis=1)
    new_state_s5 = jnp.stack(new_s, axis=1)
    return (y_prompt, y_sample, new_cache_k, new_cache_v, new_state_s5)
```

```python
import functools
import math

import jax
import jax.numpy as jnp
from jax import lax
from jax.experimental import pallas as pl
from jax.experimental.pallas import tpu as pltpu

F32 = jnp.float32
BF16 = jnp.bfloat16

D_MODEL = 1024
DEPTH = 4
N_GROUPS = 64
GROUP_CH = 16
STATE_DIM = 64
HEAD_DIM = 128
N_HEADS = 8
N_KV_HEADS = 2
KV_REP = N_HEADS // N_KV_HEADS
D_Q = N_HEADS * HEAD_DIM
D_KV = N_KV_HEADS * HEAD_DIM
QKV_DIM = D_Q + 2 * D_KV
GRID_W = 64
ROPE_THETA = 10000.0
AXIS_PAIRS = HEAD_DIM // 4
ATTN_SCALE = HEAD_DIM ** -0.5
D_FF = 2816
DEEPNORM_ALPHA = (2.0 * DEPTH) ** 0.25
LN_EPS = 1e-6
RMS_EPS = 1e-6

V7X_VMEM_LIMIT_BYTES = 56 * 1024 * 1024
LANES = 128
SUBLANES = 8
GROUPS_PER_BLOCK = LANES // GROUP_CH
N_GROUP_BLOCKS = N_GROUPS // GROUPS_PER_BLOCK
BLOCK_STATES = GROUPS_PER_BLOCK * STATE_DIM


def _params(n_axes):
    return pltpu.CompilerParams(dimension_semantics=("arbitrary",) * n_axes,
                                vmem_limit_bytes=V7X_VMEM_LIMIT_BYTES)


def _sigmoid(x):
    return 1.0 / (1.0 + jnp.exp(-x))


def _gelu_tanh(x):
    cdf = 0.5 * (1.0 + jnp.tanh(math.sqrt(2.0 / math.pi) * (x + 0.044715 * (x * x * x))))
    return x * cdf


def _layer_norm(r, g, b):
    mu = jnp.mean(r, axis=-1, keepdims=True)
    d = r - mu
    var = jnp.mean(d * d, axis=-1, keepdims=True)
    return d * lax.rsqrt(var + LN_EPS) * g + b


def _dot(a, b):
    return jnp.dot(a, b, preferred_element_type=F32)


def _adaln_kernel(c_ref, w_ref, b_ref, o_ref):
    c = c_ref[...]
    s = c * _sigmoid(c)
    o_ref[0] = _dot(s.astype(BF16), w_ref[0].astype(BF16)) + b_ref[0]


def _adaln(cond, w_mod, b_mod):
    tn = 1536
    n = 6 * D_MODEL
    return pl.pallas_call(
        _adaln_kernel,
        grid=(DEPTH, n // tn),
        in_specs=[pl.BlockSpec((8, D_MODEL), lambda l, j: (0, 0)),
                  pl.BlockSpec((1, D_MODEL, tn), lambda l, j: (l, 0, j)),
                  pl.BlockSpec((1, 1, tn), lambda l, j: (l, 0, j))],
        out_specs=pl.BlockSpec((1, 8, tn), lambda l, j: (l, 0, j)),
        out_shape=jax.ShapeDtypeStruct((DEPTH, 8, n), F32),
        compiler_params=_params(2),
        name="adaln",
    )(cond, w_mod, b_mod.reshape(DEPTH, 1, n))


def _s5_prep_kernel(are_ref, aim_ref, ldt_ref, bre_ref, bim_ref,
                    abre_ref, abim_ref, bbre_ref, bbim_ref):
    a_re = are_ref[...]
    a_im = aim_ref[...]
    dt = jnp.exp(ldt_ref[...])
    mag = jnp.exp(dt * a_re)
    ab_re = mag * jnp.cos(dt * a_im)
    ab_im = mag * jnp.sin(dt * a_im)
    den = a_re * a_re + a_im * a_im
    nr = ab_re - 1.0
    k_re = (nr * a_re + ab_im * a_im) / den
    k_im = (ab_im * a_re - nr * a_im) / den
    b_re = bre_ref[...]
    b_im = bim_ref[...]
    abre_ref[...] = ab_re
    abim_ref[...] = ab_im
    bbre_ref[...] = k_re * b_re - k_im * b_im
    bbim_ref[...] = k_re * b_im + k_im * b_re


def _s5_prep(a_re, a_im, log_dt, b_re, b_im):
    g, p, h = N_GROUPS, STATE_DIM, GROUP_CH
    s_small = jax.ShapeDtypeStruct((2, g, 1, p), F32)
    s_big = jax.ShapeDtypeStruct((2, g, h, p), F32)
    return pl.pallas_call(
        _s5_prep_kernel,
        out_shape=(s_small, s_small, s_big, s_big),
        compiler_params=pltpu.CompilerParams(vmem_limit_bytes=V7X_VMEM_LIMIT_BYTES),
        name="s5_prep",
    )(a_re.reshape(2, g, 1, p), a_im.reshape(2, g, 1, p), log_dt.reshape(2, g, 1, 1), b_re, b_im)


def _s5_layer_tables(a_re, a_im, log_dt, b_re, b_im, c_re, c_im):
    ab_re, ab_im, bb_re, bb_im = _s5_prep(a_re, a_im, log_dt,
                                          b_re.transpose(0, 1, 3, 2), b_im.transpose(0, 1, 3, 2))
    nb, gl, h, p = N_GROUP_BLOCKS, GROUPS_PER_BLOCK, GROUP_CH, STATE_DIM
    eye = jnp.eye(gl, dtype=F32)
    coef = jnp.stack([ab_re[0], ab_im[0], ab_re[1], ab_im[1]], axis=0).reshape(4, nb, gl * p)
    coef = coef.transpose(1, 0, 2)
    bb = jnp.stack([bb_re, bb_im], axis=1).reshape(2, 2, nb, gl, h, p)
    wb = jnp.einsum('drbghp,gk->bghdrkp', bb, eye).reshape(nb, gl * h, 2 * 2 * gl * p).astype(BF16)
    cc = jnp.stack([c_re, c_im], axis=1).reshape(2, 2, nb, gl, h, p)
    wc = jnp.einsum('drbghp,gk->bdrkpgh', cc, eye).reshape(nb, 2 * 2 * gl * p, gl * h).astype(BF16)
    return coef, wb, wc


def _s5_in_kernel(x_ref, m_ref, w_ref, o_ref, *, nb):
    b = pl.program_id(0)

    @pl.when(b < nb)
    def _():
        m = m_ref[0]
        h = x_ref[...] * (1.0 + m[1:2]) + m[0:1]
        o_ref[...] = _dot(h.astype(BF16), w_ref[...])

    @pl.when(b >= nb)
    def _():
        o_ref[...] = jnp.zeros_like(o_ref)


def _s5_in(x, mods, w_in, nb, seq, tm):
    nt = seq // tm
    slots = -(-nb // 8) * 8
    per_batch = mods.shape[0] > 1
    return pl.pallas_call(
        functools.partial(_s5_in_kernel, nb=nb),
        grid=(slots, nt),
        in_specs=[pl.BlockSpec((tm, D_MODEL), lambda b, t: (jnp.minimum(b, nb - 1) * nt + t, 0)),
                  pl.BlockSpec((1, 6, D_MODEL),
                               lambda b, t: (jnp.minimum(b, nb - 1) if per_batch else 0, 0, 0)),
                  pl.BlockSpec((D_MODEL, D_MODEL), lambda b, t: (0, 0))],
        out_specs=pl.BlockSpec((tm, D_MODEL), lambda b, t: (t, b)),
        out_shape=jax.ShapeDtypeStruct((seq, slots * D_MODEL), F32),
        compiler_params=_params(2),
        name="s5_in",
    )(x, mods, w_in)


def _s5_scan_kernel(*refs, seq, tc, has_h0, has_fin):
    refs = list(refs)
    u_ref = refs.pop(0)
    h0_ref = refs.pop(0) if has_h0 else None
    coef_ref, wb_ref, wc_ref, d_ref = refs[:4]
    y_ref = refs[4]
    fin_ref = refs[5] if has_fin else None
    sf_ref, sb_ref = refs[-2:]
    ns = SUBLANES
    rows = tc * ns
    nst = BLOCK_STATES

    def get_u(t0):
        return u_ref[pl.ds(t0, tc)].reshape(rows, LANES)

    def add_y(t0, val):
        idx = pl.ds(t0, tc)
        y_ref[idx] = y_ref[idx] + val.reshape(tc, ns, LANES)

    y_ref[...] = u_ref[...] * d_ref[...].reshape(1, 1, LANES)

    coef = coef_ref[0]
    a_fr = jnp.broadcast_to(coef[0:1], (ns, nst))
    a_fi = jnp.broadcast_to(coef[1:2], (ns, nst))
    a_br = jnp.broadcast_to(coef[2:3], (ns, nst))
    a_bi = jnp.broadcast_to(coef[3:4], (ns, nst))

    if has_h0:
        init = (h0_ref[0, 0], h0_ref[0, 1], h0_ref[1, 0], h0_ref[1, 1])
    else:
        z = jnp.zeros((ns, nst), F32)
        init = (z, z, z, z)

    def step(i, carry):
        fr, fi, br, bi = carry
        rf = pl.ds(pl.multiple_of(i * ns, ns), ns)
        rb = pl.ds(pl.multiple_of((tc - 1 - i) * ns, ns), ns)
        nfr = a_fr * fr - a_fi * fi + sf_ref[rf, 0:nst]
        nfi = a_fr * fi + a_fi * fr + sf_ref[rf, nst:2 * nst]
        nbr = a_br * br - a_bi * bi + sb_ref[rb, 0:nst]
        nbi = a_br * bi + a_bi * br + sb_ref[rb, nst:2 * nst]
        sf_ref[rf, 0:nst] = nfr
        sf_ref[rf, nst:2 * nst] = nfi
        sb_ref[rb, 0:nst] = nbr
        sb_ref[rb, nst:2 * nst] = nbi
        return nfr, nfi, nbr, nbi

    def chunk(c, carry):
        t_f = c * tc
        t_b = seq - (c + 1) * tc
        sf_ref[...] = _dot(get_u(t_f).astype(BF16), wb_ref[0, :, 0:2 * nst])
        sb_ref[...] = _dot(get_u(t_b).astype(BF16), wb_ref[0, :, 2 * nst:4 * nst])
        carry = lax.fori_loop(0, tc, step, carry)
        yf = (_dot(sf_ref[:, 0:nst].astype(BF16), wc_ref[0, 0:nst, :])
              - _dot(sf_ref[:, nst:2 * nst].astype(BF16), wc_ref[0, nst:2 * nst, :]))
        yb = (_dot(sb_ref[:, 0:nst].astype(BF16), wc_ref[0, 2 * nst:3 * nst, :])
              - _dot(sb_ref[:, nst:2 * nst].astype(BF16), wc_ref[0, 3 * nst:4 * nst, :]))
        add_y(t_f, yf)
        add_y(t_b, yb)
        return carry

    fr, fi, br, bi = lax.fori_loop(0, seq // tc, chunk, init)
    if has_fin:
        fin_ref[0, 0] = fr
        fin_ref[0, 1] = fi
        fin_ref[1, 0] = br
        fin_ref[1, 1] = bi


def _s5_scan(u, h0, coef, wb, wc, d_skip, seq, want_final):
    tc = 128
    ns = SUBLANES
    nb = u.shape[1] // D_MODEL
    nsg = nb // ns
    nst = BLOCK_STATES
    u_in = u.reshape(seq, nb, D_MODEL)
    u_spec = pl.BlockSpec((seq, ns, LANES), lambda s, g: (0, s, g))
    st_spec = pl.BlockSpec((2, 2, ns, nst), lambda s, g: (0, 0, s, g))
    in_specs, args = [u_spec], [u_in]
    if h0 is not None:
        in_specs.append(st_spec)
        args.append(h0)
    in_specs += [pl.BlockSpec((1, 4, nst), lambda s, g: (g, 0, 0)),
                 pl.BlockSpec((1, LANES, 4 * nst), lambda s, g: (g, 0, 0)),
                 pl.BlockSpec((1, 4 * nst, LANES), lambda s, g: (g, 0, 0)),
                 pl.BlockSpec((1, LANES), lambda s, g: (0, g))]
    args += [coef, wb, wc, d_skip.reshape(1, D_MODEL)]
    out_specs = [u_spec]
    out_shape = [jax.ShapeDtypeStruct(u_in.shape, F32)]
    if want_final:
        out_specs.append(st_spec)
        out_shape.append(jax.ShapeDtypeStruct((2, 2, nb, N_GROUPS * STATE_DIM), F32))
    res = pl.pallas_call(
        functools.partial(_s5_scan_kernel, seq=seq, tc=tc, has_h0=h0 is not None,
                          has_fin=want_final),
        grid=(nsg, N_GROUP_BLOCKS),
        in_specs=in_specs,
        out_specs=out_specs,
        out_shape=out_shape,
        scratch_shapes=[pltpu.VMEM((tc * ns, 2 * nst), F32), pltpu.VMEM((tc * ns, 2 * nst), F32)],
        compiler_params=_params(2),
        name="s5_scan",
    )(*args)
    y = res[0].reshape(seq, nb * D_MODEL)
    return (y, res[1]) if want_final else (y, None)


def _glu_kernel(*refs, mode, nj):
    if mode == "s5":
        hs_ref, x_ref, m_ref, wa_ref, wb_ref, wo_ref, lng_ref, lnb_ref, o_ref, h_sc, acc_sc = refs
    else:
        x_ref, m_ref, wa_ref, wb_ref, wo_ref, lng_ref, lnb_ref, o_ref, h_sc, acc_sc = refs
    j = pl.program_id(2)
    m = m_ref[0]

    @pl.when(j == 0)
    def _():
        if mode == "s5":
            h = _gelu_tanh(hs_ref[...])
        else:
            h = x_ref[...] * (1.0 + m[4:5]) + m[3:4]
        h_sc[...] = h.astype(BF16)

    hb = h_sc[...]
    a = _dot(hb, wa_ref[...])
    b = _dot(hb, wb_ref[...])
    if mode == "s5":
        z = a * _sigmoid(b)
    else:
        z = (a * _sigmoid(a)) * b
    part = _dot(z.astype(BF16), wo_ref[...])

    @pl.when(j == 0)
    def _():
        acc_sc[...] = part

    @pl.when(j > 0)
    def _():
        acc_sc[...] = acc_sc[...] + part

    @pl.when(j == nj - 1)
    def _():
        gate = m[2:3] if mode == "s5" else m[5:6]
        r = DEEPNORM_ALPHA * x_ref[...] + gate * acc_sc[...]
        o_ref[...] = _layer_norm(r, lng_ref[...], lnb_ref[...])


def _glu_mlp(mode, x, mods, w_ab, w_o, ln_g, ln_b, nb, seq, tm, tf, hs=None):
    f = w_o.shape[0]
    nj = f // tf
    nt = seq // tm
    per_batch = mods.shape[0] > 1
    tok = pl.BlockSpec((tm, D_MODEL), lambda b, t, j: (b * nt + t, 0))
    in_specs, args = [], []
    if mode == "s5":
        in_specs.append(pl.BlockSpec((tm, D_MODEL), lambda b, t, j: (t, b)))
        args.append(hs)
    in_specs += [tok,
                 pl.BlockSpec((1, 6, D_MODEL), lambda b, t, j: (b if per_batch else 0, 0, 0)),
                 pl.BlockSpec((D_MODEL, tf), lambda b, t, j: (0, j)),
                 pl.BlockSpec((D_MODEL, tf), lambda b, t, j: (0, nj + j)),
                 pl.BlockSpec((tf, D_MODEL), lambda b, t, j: (j, 0)),
                 pl.BlockSpec((1, D_MODEL), lambda b, t, j: (0, 0)),
                 pl.BlockSpec((1, D_MODEL), lambda b, t, j: (0, 0))]
    args += [x, mods, w_ab, w_ab, w_o, ln_g.reshape(1, D_MODEL), ln_b.reshape(1, D_MODEL)]
    return pl.pallas_call(
        functools.partial(_glu_kernel, mode=mode, nj=nj),
        grid=(nb, nt, nj),
        in_specs=in_specs,
        out_specs=tok,
        out_shape=jax.ShapeDtypeStruct((nb * seq, D_MODEL), F32),
        scratch_shapes=[pltpu.VMEM((tm, D_MODEL), BF16), pltpu.VMEM((tm, D_MODEL), F32)],
        compiler_params=_params(3),
        name="glu_" + mode,
    )(*args)


def _qkv_kernel(*refs, rope):
    if rope:
        x_ref, m_ref, w_ref, qg_ref, kg_ref, cos_ref, sin_ref, q_ref, k_ref, v_ref = refs
    else:
        x_ref, m_ref, w_ref, qg_ref, kg_ref, q_ref, k_ref, v_ref = refs
    m = m_ref[0]
    h = x_ref[...] * (1.0 + m[1:2]) + m[0:1]
    qkv = _dot(h.astype(BF16), w_ref[...])
    if rope:
        cos = cos_ref[...]
        sin = sin_ref[...]
        lane = lax.broadcasted_iota(jnp.int32, cos.shape, 1)
        first = jnp.bitwise_and(lane, AXIS_PAIRS) == 0
    for hd in range(N_HEADS + N_KV_HEADS):
        xh = qkv[:, hd * HEAD_DIM:(hd + 1) * HEAD_DIM]
        gain = qg_ref[...] if hd < N_HEADS else kg_ref[...]
        n = xh * lax.rsqrt(jnp.mean(xh * xh, axis=-1, keepdims=True) + RMS_EPS) * gain
        if rope:
            up = pltpu.roll(n, HEAD_DIM - AXIS_PAIRS, 1)
            down = pltpu.roll(n, AXIS_PAIRS, 1)
            n = n * cos + jnp.where(first, up, down) * sin
        if hd < N_HEADS:
            q_ref[:, hd * HEAD_DIM:(hd + 1) * HEAD_DIM] = n.astype(BF16)
        else:
            k_ref[:, (hd - N_HEADS) * HEAD_DIM:(hd - N_HEADS + 1) * HEAD_DIM] = n
    v_ref[...] = qkv[:, D_Q + D_KV:]


def _rope_tables(seq):
    pos = jnp.arange(seq, dtype=jnp.int32)
    row = (pos // GRID_W).astype(F32)
    col = (pos % GRID_W).astype(F32)
    inv = ROPE_THETA ** (-jnp.arange(AXIS_PAIRS, dtype=F32) / AXIS_PAIRS)
    ar = row[:, None] * inv
    ac = col[:, None] * inv
    cos = jnp.concatenate([jnp.cos(ar), jnp.cos(ar), jnp.cos(ac), jnp.cos(ac)], axis=-1)
    sin = jnp.concatenate([-jnp.sin(ar), jnp.sin(ar), -jnp.sin(ac), jnp.sin(ac)], axis=-1)
    return cos, sin


def _qkv(x, mods, w_qkv, q_gain, k_gain, nb, seq, tm, rope):
    nt = seq // tm
    per_batch = mods.shape[0] > 1
    in_specs = [pl.BlockSpec((tm, D_MODEL), lambda b, t: (b * nt + t, 0)),
                pl.BlockSpec((1, 6, D_MODEL), lambda b, t: (b if per_batch else 0, 0, 0)),
                pl.BlockSpec((D_MODEL, QKV_DIM), lambda b, t: (0, 0)),
                pl.BlockSpec((1, HEAD_DIM), lambda b, t: (0, 0)),
                pl.BlockSpec((1, HEAD_DIM), lambda b, t: (0, 0))]
    args = [x, mods, w_qkv, q_gain.reshape(1, HEAD_DIM), k_gain.reshape(1, HEAD_DIM)]
    if rope:
        cos, sin = _rope_tables(seq)
        in_specs += [pl.BlockSpec((tm, HEAD_DIM), lambda b, t: (t, 0))] * 2
        args += [cos, sin]
    n_tok = nb * seq
    return pl.pallas_call(
        functools.partial(_qkv_kernel, rope=rope),
        grid=(nb, nt),
        in_specs=in_specs,
        out_specs=[pl.BlockSpec((tm, D_Q), lambda b, t: (b * nt + t, 0)),
                   pl.BlockSpec((tm, D_KV), lambda b, t: (b * nt + t, 0)),
                   pl.BlockSpec((tm, D_KV), lambda b, t: (b * nt + t, 0))],
        out_shape=[jax.ShapeDtypeStruct((n_tok, D_Q), BF16),
                   jax.ShapeDtypeStruct((n_tok, D_KV), F32),
                   jax.ShapeDtypeStruct((n_tok, D_KV), F32)],
        compiler_params=_params(2),
        name="qkv_rope" if rope else "qkv",
    )(*args)


def _attn_kernel(*refs, has_cache):
    if has_cache:
        (q_ref, k_ref, v_ref, ck_ref, cv_ref, x_ref, m_ref, wo_ref, lng_ref, lnb_ref,
         o_ref, oh_sc) = refs
    else:
        q_ref, k_ref, v_ref, x_ref, m_ref, wo_ref, lng_ref, lnb_ref, o_ref, oh_sc = refs
    nt_dims = (((1,), (1,)), ((), ()))
    for g in range(N_KV_HEADS):
        sl = slice(g * HEAD_DIM, (g + 1) * HEAD_DIM)
        kg = k_ref[:, sl].astype(BF16)
        vg = v_ref[:, sl].astype(BF16)
        if has_cache:
            ckg = ck_ref[:, sl].astype(BF16)
            cvg = cv_ref[:, sl].astype(BF16)
        for r in range(KV_REP):
            hsl = slice((g * KV_REP + r) * HEAD_DIM, (g * KV_REP + r + 1) * HEAD_DIM)
            qh = q_ref[:, hsl]
            s1 = lax.dot_general(qh, kg, nt_dims, preferred_element_type=F32) * ATTN_SCALE
            mx = jnp.max(s1, axis=-1, keepdims=True)
            if has_cache:
                s2 = lax.dot_general(qh, ckg, nt_dims, preferred_element_type=F32) * ATTN_SCALE
                mx = jnp.maximum(mx, jnp.max(s2, axis=-1, keepdims=True))
            p1 = jnp.exp(s1 - mx)
            den = jnp.sum(p1, axis=-1, keepdims=True)
            o = _dot(p1.astype(BF16), vg)
            if has_cache:
                p2 = jnp.exp(s2 - mx)
                den = den + jnp.sum(p2, axis=-1, keepdims=True)
                o = o + _dot(p2.astype(BF16), cvg)
            oh_sc[:, hsl] = (o / den).astype(BF16)
    mix = _dot(oh_sc[...], wo_ref[...])
    m = m_ref[0]
    r = DEEPNORM_ALPHA * x_ref[...] + m[2:3] * mix
    o_ref[...] = _layer_norm(r, lng_ref[...], lnb_ref[...])


def _attention(q, k, v, cache_k, cache_v, layer_j, x, mods, w_o, ln_g, ln_b, nb, seq, tq):
    nt = seq // tq
    per_batch = mods.shape[0] > 1
    has_cache = cache_k is not None
    tok = pl.BlockSpec((tq, D_MODEL), lambda b, t: (b * nt + t, 0))
    kv = pl.BlockSpec((seq, D_KV), lambda b, t: (b, 0))
    in_specs, args = [tok, kv, kv], [q, k, v]
    if has_cache:
        past = cache_k.shape[2]
        cspec = pl.BlockSpec((None, None, past, D_KV), lambda b, t: (b, layer_j, 0, 0))
        in_specs += [cspec, cspec]
        args += [cache_k.reshape(cache_k.shape[0], cache_k.shape[1], past, D_KV),
                 cache_v.reshape(cache_v.shape[0], cache_v.shape[1], past, D_KV)]
    in_specs += [tok,
                 pl.BlockSpec((1, 6, D_MODEL), lambda b, t: (b if per_batch else 0, 0, 0)),
                 pl.BlockSpec((D_Q, D_MODEL), lambda b, t: (0, 0)),
                 pl.BlockSpec((1, D_MODEL), lambda b, t: (0, 0)),
                 pl.BlockSpec((1, D_MODEL), lambda b, t: (0, 0))]
    args += [x, mods, w_o, ln_g.reshape(1, D_MODEL), ln_b.reshape(1, D_MODEL)]
    return pl.pallas_call(
        functools.partial(_attn_kernel, has_cache=has_cache),
        grid=(nb, nt),
        in_specs=in_specs,
        out_specs=tok,
        out_shape=jax.ShapeDtypeStruct((nb * seq, D_MODEL), F32),
        scratch_shapes=[pltpu.VMEM((tq, D_Q), BF16)],
        compiler_params=_params(2),
        name="attn_cache" if has_cache else "attn",
    )(*args)


def kernel(x_prompt, x_sample, c, cache_k, cache_v, state_s5, c_ctx, w_mod, b_mod, ln_g, ln_b, w_s5_in, s5_a_re, s5_a_im, s5_log_dt, s5_b_re, s5_b_im, s5_c_re, s5_c_im, s5_d, w_s5_glu, w_s5_out, w_qkv, q_norm_g, k_norm_g, w_o, w_ffn_in, w_ffn_out):
    nbp, seqp, _ = x_prompt.shape
    nbs, seqs, _ = x_sample.shape
    xp = x_prompt.reshape(nbp * seqp, D_MODEL)
    xs = x_sample.reshape(nbs * seqs, D_MODEL)

    cond = jnp.concatenate([c_ctx[None, :], c, jnp.zeros((8 - 1 - nbs, D_MODEL), F32)], axis=0)
    mods = _adaln(cond, w_mod, b_mod)
    mods_p = mods[:, 0:1].reshape(DEPTH, 1, 6, D_MODEL)
    mods_s = mods[:, 1:1 + nbs].reshape(DEPTH, nbs, 6, D_MODEL)

    new_k, new_v, new_s = [], [], []
    for layer in range(DEPTH):
        j = layer // 2
        mp, ms = mods_p[layer], mods_s[layer]
        lg0, lb0, lg1, lb1 = ln_g[layer, 0], ln_b[layer, 0], ln_g[layer, 1], ln_b[layer, 1]
        if layer % 2 == 0:
            coef, wb, wc = _s5_layer_tables(s5_a_re[j], s5_a_im[j], s5_log_dt[j], s5_b_re[j],
                                            s5_b_im[j], s5_c_re[j], s5_c_im[j])
            w_in = w_s5_in[j].astype(BF16)
            w_glu = w_s5_glu[j].astype(BF16)
            w_out = w_s5_out[j].astype(BF16)
            h0 = state_s5[:, j].transpose(1, 2, 0, 3, 4).reshape(2, 2, nbs, N_GROUPS * STATE_DIM)
            h0 = jnp.pad(h0, ((0, 0), (0, 0), (0, -nbs % SUBLANES), (0, 0)))
            up = _s5_in(xp, mp, w_in, nbp, seqp, 256)
            us = _s5_in(xs, ms, w_in, nbs, seqs, 512)
            yp, fin = _s5_scan(up, None, coef, wb, wc, s5_d[j], seqp, True)
            ys, _ = _s5_scan(us, h0, coef, wb, wc, s5_d[j], seqs, False)
            new_s.append(fin.reshape(2, 2, nbp, N_GROUPS, STATE_DIM).transpose(2, 0, 1, 3, 4))
            xp = _glu_mlp("s5", xp, mp, w_glu, w_out, lg0, lb0, nbp, seqp, 256, D_MODEL, hs=yp)
            xs = _glu_mlp("s5", xs, ms, w_glu, w_out, lg0, lb0, nbs, seqs, 512, D_MODEL, hs=ys)
        else:
            wq = w_qkv[j].astype(BF16)
            wo = w_o[j].astype(BF16)
            qp, kp, vp = _qkv(xp, mp, wq, q_norm_g[j], k_norm_g[j], nbp, seqp, 256, False)
            qs, ks, vs = _qkv(xs, ms, wq, q_norm_g[j], k_norm_g[j], nbs, seqs, 512, True)
            new_k.append(kp.reshape(nbp, seqp, N_KV_HEADS, HEAD_DIM))
            new_v.append(vp.reshape(nbp, seqp, N_KV_HEADS, HEAD_DIM))
            xp = _attention(qp, kp, vp, None, None, j, xp, mp, wo, lg0, lb0, nbp, seqp, 256)
            xs = _attention(qs, ks, vs, cache_k, cache_v, j, xs, ms, wo, lg0, lb0, nbs, seqs, 256)
        w_fi = w_ffn_in[layer].astype(BF16)
        w_fo = w_ffn_out[layer].astype(BF16)
        xp = _glu_mlp("ffn", xp, mp, w_fi, w_fo, lg1, lb1, 1, nbp * seqp, 512, D_FF // 2)
        xs = _glu_mlp("ffn", xs, ms, w_fi, w_fo, lg1, lb1, nbs, seqs, 512, D_FF // 2)

    y_prompt = xp.reshape(nbp, seqp, D_MODEL)
    y_sample = xs.reshape(nbs, seqs, D_MODEL)
    return (y_prompt, y_sample, jnp.stack(new_k, axis=1), jnp.stack(new_v, axis=1),
            jnp.stack(new_s, axis=1))
```

```python
import functools
import math

import jax
import jax.numpy as jnp
from jax import lax
from jax.experimental import pallas as pl
from jax.experimental.pallas import tpu as pltpu

F32 = jnp.float32
BF16 = jnp.bfloat16

D_MODEL = 1024
DEPTH = 4
N_GROUPS = 64
GROUP_CH = 16
STATE_DIM = 64
HEAD_DIM = 128
N_HEADS = 8
N_KV_HEADS = 2
KV_REP = N_HEADS // N_KV_HEADS
D_Q = N_HEADS * HEAD_DIM
D_KV = N_KV_HEADS * HEAD_DIM
QKV_DIM = D_Q + 2 * D_KV
GRID_W = 64
ROPE_THETA = 10000.0
AXIS_PAIRS = HEAD_DIM // 4
ATTN_SCALE = HEAD_DIM ** -0.5
D_FF = 2816
DEEPNORM_ALPHA = (2.0 * DEPTH) ** 0.25
LN_EPS = 1e-6
RMS_EPS = 1e-6

V7X_VMEM_LIMIT_BYTES = 56 * 1024 * 1024
LANES = 128
SUBLANES = 8
MXU_TILE = 256
GROUPS_PER_BLOCK = LANES // GROUP_CH
N_GROUP_BLOCKS = N_GROUPS // GROUPS_PER_BLOCK
BLOCK_STATES = GROUPS_PER_BLOCK * STATE_DIM


def _params(n_axes):
    return pltpu.CompilerParams(dimension_semantics=("arbitrary",) * n_axes,
                                vmem_limit_bytes=V7X_VMEM_LIMIT_BYTES)


def _sigmoid(x):
    return 1.0 / (1.0 + jnp.exp(-x))


def _gelu_tanh(x):
    cdf = 0.5 * (1.0 + jnp.tanh(math.sqrt(2.0 / math.pi) * (x + 0.044715 * (x * x * x))))
    return x * cdf


def _layer_norm(r, g, b):
    mu = jnp.mean(r, axis=-1, keepdims=True)
    d = r - mu
    var = jnp.mean(d * d, axis=-1, keepdims=True)
    return d * lax.rsqrt(var + LN_EPS) * g + b


def _dot(a, b):
    return jnp.dot(a, b, preferred_element_type=F32)


def _adaln_kernel(c_ref, w_ref, b_ref, o_ref):
    c = c_ref[...]
    s = c * _sigmoid(c)
    o_ref[0] = _dot(s.astype(BF16), w_ref[0].astype(BF16)) + b_ref[0]


def _adaln(cond, w_mod, b_mod):
    tn = 1536
    n = 6 * D_MODEL
    return pl.pallas_call(
        _adaln_kernel,
        grid=(DEPTH, n // tn),
        in_specs=[pl.BlockSpec((8, D_MODEL), lambda l, j: (0, 0)),
                  pl.BlockSpec((1, D_MODEL, tn), lambda l, j: (l, 0, j)),
                  pl.BlockSpec((1, 1, tn), lambda l, j: (l, 0, j))],
        out_specs=pl.BlockSpec((1, 8, tn), lambda l, j: (l, 0, j)),
        out_shape=jax.ShapeDtypeStruct((DEPTH, 8, n), F32),
        compiler_params=_params(2),
        name="adaln",
    )(cond, w_mod, b_mod.reshape(DEPTH, 1, n))


def _s5_prep_kernel(are_ref, aim_ref, ldt_ref, bre_ref, bim_ref,
                    abre_ref, abim_ref, bbre_ref, bbim_ref):
    a_re = are_ref[...]
    a_im = aim_ref[...]
    dt = jnp.exp(ldt_ref[...])
    mag = jnp.exp(dt * a_re)
    ab_re = mag * jnp.cos(dt * a_im)
    ab_im = mag * jnp.sin(dt * a_im)
    den = a_re * a_re + a_im * a_im
    nr = ab_re - 1.0
    k_re = (nr * a_re + ab_im * a_im) / den
    k_im = (ab_im * a_re - nr * a_im) / den
    b_re = bre_ref[...]
    b_im = bim_ref[...]
    abre_ref[...] = ab_re
    abim_ref[...] = ab_im
    bbre_ref[...] = k_re * b_re - k_im * b_im
    bbim_ref[...] = k_re * b_im + k_im * b_re


def _s5_prep(a_re, a_im, log_dt, b_re, b_im):
    g, p, h = N_GROUPS, STATE_DIM, GROUP_CH
    s_small = jax.ShapeDtypeStruct((2, g, 1, p), F32)
    s_big = jax.ShapeDtypeStruct((2, g, h, p), F32)
    return pl.pallas_call(
        _s5_prep_kernel,
        out_shape=(s_small, s_small, s_big, s_big),
        compiler_params=pltpu.CompilerParams(vmem_limit_bytes=V7X_VMEM_LIMIT_BYTES),
        name="s5_prep",
    )(a_re.reshape(2, g, 1, p), a_im.reshape(2, g, 1, p), log_dt.reshape(2, g, 1, 1), b_re, b_im)


def _s5_layer_tables(a_re, a_im, log_dt, b_re, b_im, c_re, c_im):
    ab_re, ab_im, bb_re, bb_im = _s5_prep(a_re, a_im, log_dt,
                                          b_re.transpose(0, 1, 3, 2), b_im.transpose(0, 1, 3, 2))
    nb, gl, h, p = N_GROUP_BLOCKS, GROUPS_PER_BLOCK, GROUP_CH, STATE_DIM
    eye = jnp.eye(gl, dtype=F32)
    coef = jnp.stack([ab_re[0], ab_im[0], ab_re[1], ab_im[1]], axis=0).reshape(4, nb, gl * p)
    coef = coef.transpose(1, 0, 2)
    bb = jnp.stack([bb_re, bb_im], axis=1).reshape(2, 2, nb, gl, h, p)
    wb = jnp.einsum('drbghp,gk->bghdrkp', bb, eye).reshape(nb, gl * h, 2 * 2 * gl * p).astype(BF16)
    cc = jnp.stack([c_re, c_im], axis=1).reshape(2, 2, nb, gl, h, p)
    wc = jnp.einsum('drbghp,gk->bdrkpgh', cc, eye).reshape(nb, 2 * 2 * gl * p, gl * h).astype(BF16)
    return coef, wb, wc


def _s5_in_kernel(x_ref, m_ref, w_ref, o_ref, *, nb):
    b = pl.program_id(0)

    @pl.when(b < nb)
    def _():
        m = m_ref[0]
        h = x_ref[...] * (1.0 + m[1:2]) + m[0:1]
        o_ref[...] = _dot(h.astype(BF16), w_ref[...])

    @pl.when(b >= nb)
    def _():
        o_ref[...] = jnp.zeros_like(o_ref)


def _s5_in(x, mods, w_in, wl, nb, seq, tm):
    nt = seq // tm
    slots = -(-nb // 8) * 8
    per_batch = mods.shape[0] > 1
    return pl.pallas_call(
        functools.partial(_s5_in_kernel, nb=nb),
        grid=(slots, nt),
        in_specs=[pl.BlockSpec((tm, D_MODEL), lambda b, t: (jnp.minimum(b, nb - 1) * nt + t, 0)),
                  pl.BlockSpec((1, 6, D_MODEL),
                               lambda b, t: (jnp.minimum(b, nb - 1) if per_batch else 0, 0, 0)),
                  pl.BlockSpec((None, D_MODEL, D_MODEL), lambda b, t: (wl, 0, 0))],
        out_specs=pl.BlockSpec((tm, D_MODEL), lambda b, t: (t, b)),
        out_shape=jax.ShapeDtypeStruct((seq, slots * D_MODEL), F32),
        compiler_params=_params(2),
        name="s5_in",
    )(x, mods, w_in)


def _s5_scan_kernel(*refs, seq, tc, has_h0, has_fin):
    refs = list(refs)
    u_ref = refs.pop(0)
    h0_ref = refs.pop(0) if has_h0 else None
    coef_ref, wb_ref, wc_ref, d_ref = refs[:4]
    y_ref = refs[4]
    fin_ref = refs[5] if has_fin else None
    sf_ref, sb_ref = refs[-2:]
    ns = SUBLANES
    rows = tc * ns
    nst = BLOCK_STATES

    def get_u(t0):
        return u_ref[pl.ds(t0, tc)].reshape(rows, LANES)

    def add_y(t0, val):
        idx = pl.ds(t0, tc)
        y_ref[idx] = y_ref[idx] + val.reshape(tc, ns, LANES)

    y_ref[...] = u_ref[...] * d_ref[...].reshape(1, 1, LANES)

    coef = coef_ref[0]
    a_fr = jnp.broadcast_to(coef[0:1], (ns, nst))
    a_fi = jnp.broadcast_to(coef[1:2], (ns, nst))
    a_br = jnp.broadcast_to(coef[2:3], (ns, nst))
    a_bi = jnp.broadcast_to(coef[3:4], (ns, nst))

    if has_h0:
        init = (h0_ref[0, 0], h0_ref[0, 1], h0_ref[1, 0], h0_ref[1, 1])
    else:
        z = jnp.zeros((ns, nst), F32)
        init = (z, z, z, z)

    def step(i, carry):
        fr, fi, br, bi = carry
        rf = pl.ds(pl.multiple_of(i * ns, ns), ns)
        rb = pl.ds(pl.multiple_of((tc - 1 - i) * ns, ns), ns)
        nfr = a_fr * fr - a_fi * fi + sf_ref[rf, 0:nst]
        nfi = a_fr * fi + a_fi * fr + sf_ref[rf, nst:2 * nst]
        nbr = a_br * br - a_bi * bi + sb_ref[rb, 0:nst]
        nbi = a_br * bi + a_bi * br + sb_ref[rb, nst:2 * nst]
        sf_ref[rf, 0:nst] = nfr
        sf_ref[rf, nst:2 * nst] = nfi
        sb_ref[rb, 0:nst] = nbr
        sb_ref[rb, nst:2 * nst] = nbi
        return nfr, nfi, nbr, nbi

    def chunk(c, carry):
        t_f = c * tc
        t_b = seq - (c + 1) * tc
        sf_ref[...] = _dot(get_u(t_f).astype(BF16), wb_ref[0, :, 0:2 * nst])
        sb_ref[...] = _dot(get_u(t_b).astype(BF16), wb_ref[0, :, 2 * nst:4 * nst])
        carry = lax.fori_loop(0, tc, step, carry)
        yf = (_dot(sf_ref[:, 0:nst].astype(BF16), wc_ref[0, 0:nst, :])
              - _dot(sf_ref[:, nst:2 * nst].astype(BF16), wc_ref[0, nst:2 * nst, :]))
        yb = (_dot(sb_ref[:, 0:nst].astype(BF16), wc_ref[0, 2 * nst:3 * nst, :])
              - _dot(sb_ref[:, nst:2 * nst].astype(BF16), wc_ref[0, 3 * nst:4 * nst, :]))
        add_y(t_f, yf)
        add_y(t_b, yb)
        return carry

    fr, fi, br, bi = lax.fori_loop(0, seq // tc, chunk, init)
    if has_fin:
        fin_ref[0, 0] = fr
        fin_ref[0, 1] = fi
        fin_ref[1, 0] = br
        fin_ref[1, 1] = bi


def _s5_scan(u, h0, coef, wb, wc, d_skip, seq, want_final):
    tc = 128
    ns = SUBLANES
    nb = u.shape[1] // D_MODEL
    nsg = nb // ns
    nst = BLOCK_STATES
    u_in = u.reshape(seq, nb, D_MODEL)
    u_spec = pl.BlockSpec((seq, ns, LANES), lambda s, g: (0, s, g))
    st_spec = pl.BlockSpec((2, 2, ns, nst), lambda s, g: (0, 0, s, g))
    in_specs, args = [u_spec], [u_in]
    if h0 is not None:
        in_specs.append(st_spec)
        args.append(h0)
    in_specs += [pl.BlockSpec((1, 4, nst), lambda s, g: (g, 0, 0)),
                 pl.BlockSpec((1, LANES, 4 * nst), lambda s, g: (g, 0, 0)),
                 pl.BlockSpec((1, 4 * nst, LANES), lambda s, g: (g, 0, 0)),
                 pl.BlockSpec((1, LANES), lambda s, g: (0, g))]
    args += [coef, wb, wc, d_skip.reshape(1, D_MODEL)]
    out_specs = [u_spec]
    out_shape = [jax.ShapeDtypeStruct(u_in.shape, F32)]
    if want_final:
        out_specs.append(st_spec)
        out_shape.append(jax.ShapeDtypeStruct((2, 2, nb, N_GROUPS * STATE_DIM), F32))
    res = pl.pallas_call(
        functools.partial(_s5_scan_kernel, seq=seq, tc=tc, has_h0=h0 is not None,
                          has_fin=want_final),
        grid=(nsg, N_GROUP_BLOCKS),
        in_specs=in_specs,
        out_specs=out_specs,
        out_shape=out_shape,
        scratch_shapes=[pltpu.VMEM((tc * ns, 2 * nst), F32), pltpu.VMEM((tc * ns, 2 * nst), F32)],
        compiler_params=_params(2),
        name="s5_scan",
    )(*args)
    y = res[0].reshape(seq, nb * D_MODEL)
    return (y, res[1]) if want_final else (y, None)


def _glu_kernel(*refs, mode, f):
    if mode == "s5":
        hs_ref, x_ref, m_ref, wab_ref, wo_ref, lng_ref, lnb_ref, o_ref, h_sc = refs
    else:
        x_ref, m_ref, wab_ref, wo_ref, lng_ref, lnb_ref, o_ref, h_sc = refs
    m = m_ref[0]
    if mode == "s5":
        h = _gelu_tanh(hs_ref[...])
    else:
        h = x_ref[...] * (1.0 + m[4:5]) + m[3:4]
    h_sc[...] = h.astype(BF16)

    acc = None
    for c0 in range(0, f, MXU_TILE):
        a = _dot(h_sc[...], wab_ref[:, c0:c0 + MXU_TILE])
        b = _dot(h_sc[...], wab_ref[:, f + c0:f + c0 + MXU_TILE])
        if mode == "s5":
            z = a * _sigmoid(b)
        else:
            z = (a * _sigmoid(a)) * b
        part = _dot(z.astype(BF16), wo_ref[c0:c0 + MXU_TILE, :])
        acc = part if acc is None else acc + part

    gate = m[2:3] if mode == "s5" else m[5:6]
    r = DEEPNORM_ALPHA * x_ref[...] + gate * acc
    o_ref[...] = _layer_norm(r, lng_ref[...], lnb_ref[...])


def _glu_mlp(mode, x, mods, w_ab, w_o, wl, ln_g, ln_b, nb, seq, tm, hs=None):
    f = w_o.shape[1]
    nt = seq // tm
    per_batch = mods.shape[0] > 1
    tok = pl.BlockSpec((tm, D_MODEL), lambda b, t: (b * nt + t, 0))
    resident = pl.Buffered(1)
    in_specs, args = [], []
    if mode == "s5":
        in_specs.append(pl.BlockSpec((tm, D_MODEL), lambda b, t: (t, b)))
        args.append(hs)
    in_specs += [tok,
                 pl.BlockSpec((1, 6, D_MODEL), lambda b, t: (b if per_batch else 0, 0, 0)),
                 pl.BlockSpec((None, D_MODEL, 2 * f), lambda b, t: (wl, 0, 0), pipeline_mode=resident),
                 pl.BlockSpec((None, f, D_MODEL), lambda b, t: (wl, 0, 0), pipeline_mode=resident),
                 pl.BlockSpec((1, D_MODEL), lambda b, t: (0, 0)),
                 pl.BlockSpec((1, D_MODEL), lambda b, t: (0, 0))]
    args += [x, mods, w_ab, w_o, ln_g.reshape(1, D_MODEL), ln_b.reshape(1, D_MODEL)]
    return pl.pallas_call(
        functools.partial(_glu_kernel, mode=mode, f=f),
        grid=(nb, nt),
        in_specs=in_specs,
        out_specs=tok,
        out_shape=jax.ShapeDtypeStruct((nb * seq, D_MODEL), F32),
        scratch_shapes=[pltpu.VMEM((tm, D_MODEL), BF16)],
        compiler_params=_params(2),
        name="glu_" + mode,
    )(*args)


def _qkv_kernel(*refs, rope):
    if rope:
        x_ref, m_ref, w_ref, qg_ref, kg_ref, cos_ref, sin_ref, q_ref, k_ref, v_ref = refs
    else:
        x_ref, m_ref, w_ref, qg_ref, kg_ref, q_ref, k_ref, v_ref = refs
    m = m_ref[0]
    h = x_ref[...] * (1.0 + m[1:2]) + m[0:1]
    qkv = _dot(h.astype(BF16), w_ref[...])
    if rope:
        cos = cos_ref[...]
        sin = sin_ref[...]
        lane = lax.broadcasted_iota(jnp.int32, cos.shape, 1)
        first = jnp.bitwise_and(lane, AXIS_PAIRS) == 0
    for hd in range(N_HEADS + N_KV_HEADS):
        xh = qkv[:, hd * HEAD_DIM:(hd + 1) * HEAD_DIM]
        gain = qg_ref[...] if hd < N_HEADS else kg_ref[...]
        n = xh * lax.rsqrt(jnp.mean(xh * xh, axis=-1, keepdims=True) + RMS_EPS) * gain
        if rope:
            up = pltpu.roll(n, HEAD_DIM - AXIS_PAIRS, 1)
            down = pltpu.roll(n, AXIS_PAIRS, 1)
            n = n * cos + jnp.where(first, up, down) * sin
        if hd < N_HEADS:
            q_ref[:, hd * HEAD_DIM:(hd + 1) * HEAD_DIM] = n.astype(BF16)
        else:
            k_ref[:, (hd - N_HEADS) * HEAD_DIM:(hd - N_HEADS + 1) * HEAD_DIM] = n
    v_ref[...] = qkv[:, D_Q + D_KV:]


def _rope_tables(seq):
    pos = jnp.arange(seq, dtype=jnp.int32)
    row = (pos // GRID_W).astype(F32)
    col = (pos % GRID_W).astype(F32)
    inv = ROPE_THETA ** (-jnp.arange(AXIS_PAIRS, dtype=F32) / AXIS_PAIRS)
    ar = row[:, None] * inv
    ac = col[:, None] * inv
    cos = jnp.concatenate([jnp.cos(ar), jnp.cos(ar), jnp.cos(ac), jnp.cos(ac)], axis=-1)
    sin = jnp.concatenate([-jnp.sin(ar), jnp.sin(ar), -jnp.sin(ac), jnp.sin(ac)], axis=-1)
    return cos, sin


def _qkv(x, mods, w_qkv, wl, q_gain, k_gain, nb, seq, tm, rope):
    nt = seq // tm
    per_batch = mods.shape[0] > 1
    in_specs = [pl.BlockSpec((tm, D_MODEL), lambda b, t: (b * nt + t, 0)),
                pl.BlockSpec((1, 6, D_MODEL), lambda b, t: (b if per_batch else 0, 0, 0)),
                pl.BlockSpec((None, D_MODEL, QKV_DIM), lambda b, t: (wl, 0, 0)),
                pl.BlockSpec((1, HEAD_DIM), lambda b, t: (0, 0)),
                pl.BlockSpec((1, HEAD_DIM), lambda b, t: (0, 0))]
    args = [x, mods, w_qkv, q_gain.reshape(1, HEAD_DIM), k_gain.reshape(1, HEAD_DIM)]
    if rope:
        cos, sin = _rope_tables(seq)
        in_specs += [pl.BlockSpec((tm, HEAD_DIM), lambda b, t: (t, 0))] * 2
        args += [cos, sin]
    n_tok = nb * seq
    return pl.pallas_call(
        functools.partial(_qkv_kernel, rope=rope),
        grid=(nb, nt),
        in_specs=in_specs,
        out_specs=[pl.BlockSpec((tm, D_Q), lambda b, t: (b * nt + t, 0)),
                   pl.BlockSpec((tm, D_KV), lambda b, t: (b * nt + t, 0)),
                   pl.BlockSpec((tm, D_KV), lambda b, t: (b * nt + t, 0))],
        out_shape=[jax.ShapeDtypeStruct((n_tok, D_Q), BF16),
                   jax.ShapeDtypeStruct((n_tok, D_KV), F32),
                   jax.ShapeDtypeStruct((n_tok, D_KV), F32)],
        compiler_params=_params(2),
        name="qkv_rope" if rope else "qkv",
    )(*args)


def _attn_kernel(*refs, has_cache):
    if has_cache:
        (q_ref, k_ref, v_ref, ck_ref, cv_ref, x_ref, m_ref, wo_ref, lng_ref, lnb_ref,
         o_ref, oh_sc) = refs
    else:
        q_ref, k_ref, v_ref, x_ref, m_ref, wo_ref, lng_ref, lnb_ref, o_ref, oh_sc = refs
    nt_dims = (((1,), (1,)), ((), ()))
    for g in range(N_KV_HEADS):
        sl = slice(g * HEAD_DIM, (g + 1) * HEAD_DIM)
        kg = k_ref[:, sl].astype(BF16)
        vg = v_ref[:, sl].astype(BF16)
        if has_cache:
            ckg = ck_ref[:, sl].astype(BF16)
            cvg = cv_ref[:, sl].astype(BF16)
        for r in range(KV_REP):
            hsl = slice((g * KV_REP + r) * HEAD_DIM, (g * KV_REP + r + 1) * HEAD_DIM)
            qh = q_ref[:, hsl]
            s1 = lax.dot_general(qh, kg, nt_dims, preferred_element_type=F32) * ATTN_SCALE
            mx = jnp.max(s1, axis=-1, keepdims=True)
            if has_cache:
                s2 = lax.dot_general(qh, ckg, nt_dims, preferred_element_type=F32) * ATTN_SCALE
                mx = jnp.maximum(mx, jnp.max(s2, axis=-1, keepdims=True))
            p1 = jnp.exp(s1 - mx)
            den = jnp.sum(p1, axis=-1, keepdims=True)
            o = _dot(p1.astype(BF16), vg)
            if has_cache:
                p2 = jnp.exp(s2 - mx)
                den = den + jnp.sum(p2, axis=-1, keepdims=True)
                o = o + _dot(p2.astype(BF16), cvg)
            oh_sc[:, hsl] = (o / den).astype(BF16)
    mix = _dot(oh_sc[...], wo_ref[...])
    m = m_ref[0]
    r = DEEPNORM_ALPHA * x_ref[...] + m[2:3] * mix
    o_ref[...] = _layer_norm(r, lng_ref[...], lnb_ref[...])


def _attention(q, k, v, cache_k, cache_v, layer_j, x, mods, w_o, ln_g, ln_b, nb, seq, tq):
    nt = seq // tq
    per_batch = mods.shape[0] > 1
    has_cache = cache_k is not None
    tok = pl.BlockSpec((tq, D_MODEL), lambda b, t: (b * nt + t, 0))
    kv = pl.BlockSpec((seq, D_KV), lambda b, t: (b, 0))
    in_specs, args = [tok, kv, kv], [q, k, v]
    if has_cache:
        past = cache_k.shape[2]
        cspec = pl.BlockSpec((None, None, past, D_KV), lambda b, t: (b, layer_j, 0, 0))
        in_specs += [cspec, cspec]
        args += [cache_k.reshape(cache_k.shape[0], cache_k.shape[1], past, D_KV),
                 cache_v.reshape(cache_v.shape[0], cache_v.shape[1], past, D_KV)]
    in_specs += [tok,
                 pl.BlockSpec((1, 6, D_MODEL), lambda b, t: (b if per_batch else 0, 0, 0)),
                 pl.BlockSpec((None, D_Q, D_MODEL), lambda b, t: (layer_j, 0, 0)),
                 pl.BlockSpec((1, D_MODEL), lambda b, t: (0, 0)),
                 pl.BlockSpec((1, D_MODEL), lambda b, t: (0, 0))]
    args += [x, mods, w_o, ln_g.reshape(1, D_MODEL), ln_b.reshape(1, D_MODEL)]
    return pl.pallas_call(
        functools.partial(_attn_kernel, has_cache=has_cache),
        grid=(nb, nt),
        in_specs=in_specs,
        out_specs=tok,
        out_shape=jax.ShapeDtypeStruct((nb * seq, D_MODEL), F32),
        scratch_shapes=[pltpu.VMEM((tq, D_Q), BF16)],
        compiler_params=_params(2),
        name="attn_cache" if has_cache else "attn",
    )(*args)


def kernel(x_prompt, x_sample, c, cache_k, cache_v, state_s5, c_ctx, w_mod, b_mod, ln_g, ln_b, w_s5_in, s5_a_re, s5_a_im, s5_log_dt, s5_b_re, s5_b_im, s5_c_re, s5_c_im, s5_d, w_s5_glu, w_s5_out, w_qkv, q_norm_g, k_norm_g, w_o, w_ffn_in, w_ffn_out):
    nbp, seqp, _ = x_prompt.shape
    nbs, seqs, _ = x_sample.shape
    xp = x_prompt.reshape(nbp * seqp, D_MODEL)
    xs = x_sample.reshape(nbs * seqs, D_MODEL)

    cond = jnp.concatenate([c_ctx[None, :], c, jnp.zeros((8 - 1 - nbs, D_MODEL), F32)], axis=0)
    mods = _adaln(cond, w_mod, b_mod)
    mods_p = mods[:, 0:1].reshape(DEPTH, 1, 6, D_MODEL)
    mods_s = mods[:, 1:1 + nbs].reshape(DEPTH, nbs, 6, D_MODEL)

    w_in, w_glu, w_out = w_s5_in.astype(BF16), w_s5_glu.astype(BF16), w_s5_out.astype(BF16)
    wq, wo = w_qkv.astype(BF16), w_o.astype(BF16)
    w_fi, w_fo = w_ffn_in.astype(BF16), w_ffn_out.astype(BF16)

    new_k, new_v, new_s = [], [], []
    for layer in range(DEPTH):
        j = layer // 2
        mp, ms = mods_p[layer], mods_s[layer]
        lg0, lb0, lg1, lb1 = ln_g[layer, 0], ln_b[layer, 0], ln_g[layer, 1], ln_b[layer, 1]
        if layer % 2 == 0:
            coef, wb, wc = _s5_layer_tables(s5_a_re[j], s5_a_im[j], s5_log_dt[j], s5_b_re[j],
                                            s5_b_im[j], s5_c_re[j], s5_c_im[j])
            h0 = state_s5[:, j].transpose(1, 2, 0, 3, 4).reshape(2, 2, nbs, N_GROUPS * STATE_DIM)
            h0 = jnp.pad(h0, ((0, 0), (0, 0), (0, -nbs % SUBLANES), (0, 0)))
            up = _s5_in(xp, mp, w_in, j, nbp, seqp, 256)
            us = _s5_in(xs, ms, w_in, j, nbs, seqs, 512)
            yp, fin = _s5_scan(up, None, coef, wb, wc, s5_d[j], seqp, True)
            ys, _ = _s5_scan(us, h0, coef, wb, wc, s5_d[j], seqs, False)
            new_s.append(fin.reshape(2, 2, nbp, N_GROUPS, STATE_DIM).transpose(2, 0, 1, 3, 4))
            xp = _glu_mlp("s5", xp, mp, w_glu, w_out, j, lg0, lb0, nbp, seqp, 256, hs=yp)
            xs = _glu_mlp("s5", xs, ms, w_glu, w_out, j, lg0, lb0, nbs, seqs, 512, hs=ys)
        else:
            qp, kp, vp = _qkv(xp, mp, wq, j, q_norm_g[j], k_norm_g[j], nbp, seqp, 256, False)
            qs, ks, vs = _qkv(xs, ms, wq, j, q_norm_g[j], k_norm_g[j], nbs, seqs, 512, True)
            new_k.append(kp.reshape(nbp, seqp, N_KV_HEADS, HEAD_DIM))
            new_v.append(vp.reshape(nbp, seqp, N_KV_HEADS, HEAD_DIM))
            xp = _attention(qp, kp, vp, None, None, j, xp, mp, wo, lg0, lb0, nbp, seqp, 256)
            xs = _attention(qs, ks, vs, cache_k, cache_v, j, xs, ms, wo, lg0, lb0, nbs, seqs, 256)
        xp = _glu_mlp("ffn", xp, mp, w_fi, w_fo, layer, lg1, lb1, 1, nbp * seqp, 512)
        xs = _glu_mlp("ffn", xs, ms, w_fi, w_fo, layer, lg1, lb1, nbs, seqs, 512)

    y_prompt = xp.reshape(nbp, seqp, D_MODEL)
    y_sample = xs.reshape(nbs, seqs, D_MODEL)
    return (y_prompt, y_sample, jnp.stack(new_k, axis=1), jnp.stack(new_v, axis=1),
            jnp.stack(new_s, axis=1))
```

```python
import functools
import math

import jax
import jax.numpy as jnp
from jax import lax
from jax.experimental import pallas as pl
from jax.experimental.pallas import tpu as pltpu

F32 = jnp.float32
BF16 = jnp.bfloat16

D_MODEL = 1024
DEPTH = 4
N_GROUPS = 64
GROUP_CH = 16
STATE_DIM = 64
HEAD_DIM = 128
N_HEADS = 8
N_KV_HEADS = 2
KV_REP = N_HEADS // N_KV_HEADS
D_Q = N_HEADS * HEAD_DIM
D_KV = N_KV_HEADS * HEAD_DIM
QKV_DIM = D_Q + 2 * D_KV
GRID_W = 64
ROPE_THETA = 10000.0
AXIS_PAIRS = HEAD_DIM // 4
ATTN_SCALE = HEAD_DIM ** -0.5
DEEPNORM_ALPHA = (2.0 * DEPTH) ** 0.25
LN_EPS = 1e-6
RMS_EPS = 1e-6

V7X_VMEM_LIMIT_BYTES = 56 * 1024 * 1024
LANES = 128
SUBLANES = 8
MXU_TILE = 256
GROUPS_PER_BLOCK = LANES // GROUP_CH
N_GROUP_BLOCKS = N_GROUPS // GROUPS_PER_BLOCK
S5_CHUNK = MXU_TILE // GROUP_CH
CHUNK_LANES = S5_CHUNK * GROUP_CH
PERM_SEQS = SUBLANES
PERM_STEPS = MXU_TILE // PERM_SEQS
NT_DIMS = (((1,), (1,)), ((), ()))


def _params(n_axes):
    return pltpu.CompilerParams(dimension_semantics=("arbitrary",) * n_axes,
                                vmem_limit_bytes=V7X_VMEM_LIMIT_BYTES)


def _sigmoid(x):
    return 1.0 / (1.0 + jnp.exp(-x))


def _gelu_tanh(x):
    cdf = 0.5 * (1.0 + jnp.tanh(math.sqrt(2.0 / math.pi) * (x + 0.044715 * (x * x * x))))
    return x * cdf


def _layer_norm(r, g, b):
    mu = jnp.mean(r, axis=-1, keepdims=True)
    d = r - mu
    var = jnp.mean(d * d, axis=-1, keepdims=True)
    return d * lax.rsqrt(var + LN_EPS) * g + b


def _dot(a, b):
    return jnp.dot(a, b, preferred_element_type=F32)


def _dot_nt(a, b):
    return lax.dot_general(a, b, NT_DIMS, preferred_element_type=F32)


def _adaln_kernel(c_ref, w_ref, b_ref, o_ref):
    c = c_ref[...]
    s = c * _sigmoid(c)
    o_ref[0] = _dot(s.astype(BF16), w_ref[0].astype(BF16)) + b_ref[0]


def _adaln(cond, w_mod, b_mod):
    tn = 1536
    n = 6 * D_MODEL
    return pl.pallas_call(
        _adaln_kernel,
        grid=(DEPTH, n // tn),
        in_specs=[pl.BlockSpec((8, D_MODEL), lambda l, j: (0, 0)),
                  pl.BlockSpec((1, D_MODEL, tn), lambda l, j: (l, 0, j)),
                  pl.BlockSpec((1, 1, tn), lambda l, j: (l, 0, j))],
        out_specs=pl.BlockSpec((1, 8, tn), lambda l, j: (l, 0, j)),
        out_shape=jax.ShapeDtypeStruct((DEPTH, 8, n), F32),
        compiler_params=_params(2),
        name="adaln",
    )(cond, w_mod, b_mod.reshape(DEPTH, 1, n))


def _s5_prep_kernel(a_ref, ldt_ref, bt_ref, c_ref, w1_ref, w2_ref, lam_ref):
    half = STATE_DIM
    lane1 = lax.broadcasted_iota(jnp.int32, (1, LANES), 1)
    sgn1 = jnp.where(lane1 < half, -1.0, 1.0)
    lane_h = lax.broadcasted_iota(jnp.int32, (GROUP_CH, LANES), 1)
    first_h = lane_h < half
    conj_h = jnp.where(first_h, 1.0, -1.0)
    lane_c = lax.broadcasted_iota(jnp.int32, (GROUP_CH, CHUNK_LANES), 1)
    t = S5_CHUNK

    def cmul(pr, pi, x):
        return pr * x + (pi * sgn1) * pltpu.roll(x, half, 1)

    def pack_states(f, b):
        return (jnp.where(first_h, f, pltpu.roll(b, half, 1)),
                jnp.where(first_h, pltpu.roll(f, half, 1), b))

    def group(g, carry):
        kt, qs, cks, lam_t = [], [], [], []
        for d in range(2):
            a_re = a_ref[0, d, g]
            a_im = a_ref[1, d, g]
            dt = jnp.exp(ldt_ref[d, g])
            mag = jnp.exp(dt * a_re)
            lr = mag * jnp.cos(dt * a_im)
            li = mag * jnp.sin(dt * a_im)
            den = a_re * a_re + a_im * a_im
            nr = lr - 1.0
            k_re = (nr * a_re + li * a_im) / den
            k_im = (li * a_re - nr * a_im) / den
            bb = cmul(k_re, k_im, bt_ref[d, g])
            cc = c_ref[d, g]
            pr = jnp.ones((1, LANES), F32)
            pi = jnp.zeros((1, LANES), F32)
            ck, q = [], []
            for k in range(t + 1):
                ck.append(cmul(pr, pi, cc))
                if k < t:
                    q.append(cmul(pr, pi, bb))
                    pr, pi = pr * lr - pi * li, pr * li + pi * lr
            lam_t.append((pr, pi))
            order = range(t) if d == 0 else range(t - 1, -1, -1)
            rhs = jnp.concatenate([ck[k] for k in order], axis=0).astype(BF16)
            kt.append(_dot_nt((bb * conj_h).astype(BF16), rhs))
            qs.append(q)
            cks.append(ck)
        for j in range(t):
            rows = slice(j * GROUP_CH, (j + 1) * GROUP_CH)
            tf = kt[0] if j == 0 else pltpu.roll(kt[0], GROUP_CH * j, 1)
            tf = jnp.where(lane_c >= GROUP_CH * j, tf, 0.0)
            back = GROUP_CH * (t - 1 - j)
            tb = kt[1] if back == 0 else pltpu.roll(kt[1], CHUNK_LANES - back, 1)
            tb = jnp.where(lane_c < GROUP_CH * (j + 1), tb, 0.0)
            w1_ref[g, rows, 0:CHUNK_LANES] = (tf + tb).astype(BF16)
            s_re, s_im = pack_states(qs[0][t - 1 - j], qs[1][j])
            w1_ref[g, rows, CHUNK_LANES:CHUNK_LANES + LANES] = s_re.astype(BF16)
            w1_ref[g, rows, CHUNK_LANES + LANES:CHUNK_LANES + 2 * LANES] = s_im.astype(BF16)
            c_re, c_im = pack_states(cks[0][j + 1] * conj_h, cks[1][t - j] * conj_h)
            w2_ref[g, rows, 0:LANES] = c_re.astype(BF16)
            w2_ref[g, rows, LANES:2 * LANES] = c_im.astype(BF16)
        fwd1 = lane1 < half
        lam_ref[g, 0:1, :] = jnp.where(fwd1, lam_t[0][0], lam_t[1][0])
        lam_ref[g, 1:2, :] = jnp.where(fwd1, lam_t[0][1], lam_t[1][1])
        return carry

    lax.fori_loop(0, GROUPS_PER_BLOCK, group, 0)


def _s5_tables(a_re, a_im, log_dt, b_re, b_im, c_re, c_im):
    g, p, h = N_GROUPS, STATE_DIM, GROUP_CH
    dup = lambda x: jnp.concatenate([x, x], axis=-1)
    a2 = jnp.stack([dup(a_re), dup(a_im)]).reshape(2, 2, g, 1, 2 * p)
    bt = jnp.concatenate([b_re.transpose(0, 1, 3, 2), b_im.transpose(0, 1, 3, 2)], axis=-1)
    cc = jnp.concatenate([c_re, c_im], axis=-1)
    gb = GROUPS_PER_BLOCK
    return pl.pallas_call(
        _s5_prep_kernel,
        grid=(N_GROUP_BLOCKS,),
        in_specs=[pl.BlockSpec((2, 2, gb, 1, 2 * p), lambda i: (0, 0, i, 0, 0)),
                  pl.BlockSpec((2, gb, 1, 1), lambda i: (0, i, 0, 0)),
                  pl.BlockSpec((2, gb, h, 2 * p), lambda i: (0, i, 0, 0)),
                  pl.BlockSpec((2, gb, h, 2 * p), lambda i: (0, i, 0, 0))],
        out_specs=[pl.BlockSpec((gb, CHUNK_LANES, CHUNK_LANES + 2 * LANES), lambda i: (i, 0, 0)),
                   pl.BlockSpec((gb, CHUNK_LANES, 2 * LANES), lambda i: (i, 0, 0)),
                   pl.BlockSpec((gb, 2, LANES), lambda i: (i, 0, 0))],
        out_shape=[jax.ShapeDtypeStruct((g, CHUNK_LANES, CHUNK_LANES + 2 * LANES), BF16),
                   jax.ShapeDtypeStruct((g, CHUNK_LANES, 2 * LANES), BF16),
                   jax.ShapeDtypeStruct((g, 2, LANES), F32)],
        compiler_params=_params(1),
        name="s5_prep",
    )(a2, log_dt.reshape(2, g, 1, 1), bt, cc)


def _row_perm(to_time_major):
    n = PERM_SEQS * PERM_STEPS
    r = lax.broadcasted_iota(jnp.int32, (n, n), 0)
    c = lax.broadcasted_iota(jnp.int32, (n, n), 1)
    if to_time_major:
        src = jnp.bitwise_and(r, PERM_SEQS - 1) * PERM_STEPS + lax.shift_right_logical(r, 3)
    else:
        src = jnp.bitwise_and(r, PERM_STEPS - 1) * PERM_SEQS + lax.shift_right_logical(r, 5)
    return jnp.where(c == src, 1.0, 0.0).astype(BF16)


def _s5_in_kernel(x_ref, m_ref, w_ref, o_ref, *, tt):
    scale = 1.0 + m_ref[:, 1:2, :]
    shift = m_ref[:, 0:1, :]
    perm = _row_perm(True)
    n = PERM_SEQS * PERM_STEPS
    for k in range(tt // (2 * PERM_STEPS)):
        pieces = []
        for hf in range(2):
            t0 = (2 * k + hf) * PERM_STEPS
            h = x_ref[:, t0:t0 + PERM_STEPS, :] * scale + shift
            hb = h.reshape(n, D_MODEL).astype(BF16)
            pieces.append(_dot(perm, hb).astype(BF16))
        u = _dot(jnp.concatenate(pieces, axis=0), w_ref[...])
        o_ref[2 * k * PERM_STEPS:(2 * k + 2) * PERM_STEPS] = u.reshape(2 * PERM_STEPS, PERM_SEQS, D_MODEL)


def _s5_in(x, mods, w_in, wl, tt):
    slots, seq, _ = x.shape
    return pl.pallas_call(
        functools.partial(_s5_in_kernel, tt=tt),
        grid=(slots // PERM_SEQS, seq // tt),
        in_specs=[pl.BlockSpec((PERM_SEQS, tt, D_MODEL), lambda s, t: (s, t, 0)),
                  pl.BlockSpec((PERM_SEQS, 6, D_MODEL), lambda s, t: (s, 0, 0)),
                  pl.BlockSpec((None, D_MODEL, D_MODEL), lambda s, t: (wl, 0, 0))],
        out_specs=pl.BlockSpec((tt, PERM_SEQS, D_MODEL), lambda s, t: (t, s, 0)),
        out_shape=jax.ShapeDtypeStruct((seq, slots, D_MODEL), F32),
        compiler_params=_params(2),
        name="s5_in",
    )(x, mods, w_in)


def _block_transpose8(v):
    lane = lax.broadcasted_iota(jnp.int32, (SUBLANES, LANES), 1)
    v = list(v)
    for d in (4, 2, 1):
        low = jnp.bitwise_and(lane, GROUP_CH * d) == 0
        nxt = list(v)
        for i in range(8):
            if i & d == 0:
                a, b = v[i], v[i + d]
                nxt[i] = jnp.where(low, a, pltpu.roll(b, GROUP_CH * d, 1))
                nxt[i + d] = jnp.where(low, pltpu.roll(a, LANES - GROUP_CH * d, 1), b)
        v = nxt
    return v


def _s5_core_kernel(*refs, seq, nq, paired, has_fin):
    refs = list(refs)
    u_ref, w1_ref, w2_ref, lam_ref, d_ref = refs[:5]
    pos = 5
    h0_ref = None
    if paired:
        h0_ref = refs[pos]
        pos += 1
    y_ref = refs[pos]
    pos += 1
    fin_ref = refs[pos] if has_fin else None
    z_sc, yt_sc, bs_sc, sp_sc = refs[-4:]
    gpb = GROUPS_PER_BLOCK
    nc = seq // S5_CHUNK
    ns = nq * SUBLANES
    half = STATE_DIM

    def fold(c, carry):
        for qi in range(nq):
            sl = slice(qi * SUBLANES, (qi + 1) * SUBLANES)
            r0 = pl.multiple_of(c * ns + qi * SUBLANES, SUBLANES)
            for hf in range(2):
                v = [u_ref[c * S5_CHUNK + hf * 8 + t, sl, :] for t in range(8)]
                w = _block_transpose8(v)
                for g in range(gpb):
                    z_sc[g, pl.ds(r0, SUBLANES), hf * LANES:(hf + 1) * LANES] = w[g]
        return carry

    lax.fori_loop(0, nc, fold, 0)

    for g in range(gpb):
        m1 = _dot(z_sc[g].astype(BF16), w1_ref[g])
        yt_sc[g] = m1[:, 0:CHUNK_LANES]
        bs_sc[g] = m1[:, CHUNK_LANES:]

    lane = lax.broadcasted_iota(jnp.int32, (SUBLANES, LANES), 1)
    fwd = jnp.bitwise_and(lane, half) == 0
    lam = [(jnp.broadcast_to(lam_ref[g, 0:1, :], (SUBLANES, LANES)),
            jnp.broadcast_to(lam_ref[g, 1:2, :], (SUBLANES, LANES))) for g in range(gpb)]

    def run_pass(init, keep):
        def body(i, carry):
            out = []
            for g in range(gpb):
                l_re, l_im = lam[g]
                for qi in range(nq):
                    s_re, s_im = carry[2 * (g * nq + qi)], carry[2 * (g * nq + qi) + 1]
                    ri = pl.ds(pl.multiple_of(i * ns + qi * SUBLANES, SUBLANES), SUBLANES)
                    rr = pl.ds(pl.multiple_of((nc - 1 - i) * ns + qi * SUBLANES, SUBLANES), SUBLANES)
                    for row, lo, val in ((ri, 0, s_re), (ri, LANES, s_im),
                                         (rr, half, s_re), (rr, LANES + half, s_im)):
                        piece = val[:, (lo % LANES):(lo % LANES) + half]
                        if keep is not None:
                            piece = jnp.where(keep[:, (lo % LANES):(lo % LANES) + half],
                                              sp_sc[g, row, lo:lo + half], piece)
                        sp_sc[g, row, lo:lo + half] = piece
                    x_re = jnp.where(fwd, bs_sc[g, ri, 0:LANES], bs_sc[g, rr, 0:LANES])
                    x_im = jnp.where(fwd, bs_sc[g, ri, LANES:2 * LANES], bs_sc[g, rr, LANES:2 * LANES])
                    out.append(l_re * s_re - l_im * s_im + x_re)
                    out.append(l_re * s_im + l_im * s_re + x_im)
            return tuple(out)

        return lax.fori_loop(0, nc, body, init)

    if paired:
        sub = lax.broadcasted_iota(jnp.int32, (SUBLANES, LANES), 0)
        first_half = jnp.bitwise_and(sub, 1) == 0
        keep = first_half == fwd
        init = []
        for g in range(gpb):
            init += [h0_ref[g, 0], h0_ref[g, 1]]
        mid = run_pass(tuple(init), None)
        handed = [jnp.where(fwd, pltpu.roll(s, 1, 0), pltpu.roll(s, SUBLANES - 1, 0)) for s in mid]
        fin = run_pass(tuple(handed), keep)
    else:
        zero = jnp.zeros((SUBLANES, LANES), F32)
        fin = run_pass((zero,) * (2 * gpb * nq), None)

    if has_fin:
        for g in range(gpb):
            for qi in range(nq):
                sl = slice(qi * SUBLANES, (qi + 1) * SUBLANES)
                fin_ref[g, 0, sl, :] = fin[2 * (g * nq + qi)]
                fin_ref[g, 1, sl, :] = fin[2 * (g * nq + qi) + 1]

    for g in range(gpb):
        yt_sc[g] = yt_sc[g] + _dot_nt(sp_sc[g].astype(BF16), w2_ref[g])

    d = jnp.broadcast_to(d_ref[...], (SUBLANES, LANES))

    def unfold(c, carry):
        for qi in range(nq):
            sl = slice(qi * SUBLANES, (qi + 1) * SUBLANES)
            r0 = pl.multiple_of(c * ns + qi * SUBLANES, SUBLANES)
            for hf in range(2):
                w = [yt_sc[g, pl.ds(r0, SUBLANES), hf * LANES:(hf + 1) * LANES] for g in range(gpb)]
                v = _block_transpose8(w)
                for t in range(8):
                    step = c * S5_CHUNK + hf * 8 + t
                    y_ref[step, sl, :] = v[t] + d * u_ref[step, sl, :]
        return carry

    lax.fori_loop(0, nc, unfold, 0)


def _s5_core(u, w1, w2, lam, d_skip, h0, want_final):
    seq, slots, _ = u.shape
    paired = h0 is not None
    nq = 1 if paired else slots // SUBLANES
    ns = nq * SUBLANES
    gpb = GROUPS_PER_BLOCK
    rows = (seq // S5_CHUNK) * ns
    u_spec = pl.BlockSpec((seq, ns, LANES), lambda s, g: (0, s, g))
    in_specs = [u_spec,
                pl.BlockSpec((gpb, CHUNK_LANES, CHUNK_LANES + 2 * LANES), lambda s, g: (g, 0, 0)),
                pl.BlockSpec((gpb, CHUNK_LANES, 2 * LANES), lambda s, g: (g, 0, 0)),
                pl.BlockSpec((gpb, 2, LANES), lambda s, g: (g, 0, 0)),
                pl.BlockSpec((1, LANES), lambda s, g: (0, g))]
    args = [u, w1, w2, lam, d_skip.reshape(1, D_MODEL)]
    if paired:
        in_specs.append(pl.BlockSpec((gpb, 2, SUBLANES, LANES), lambda s, g: (g, 0, s, 0)))
        args.append(h0)
    out_specs = [u_spec]
    out_shape = [jax.ShapeDtypeStruct(u.shape, F32)]
    if want_final:
        out_specs.append(pl.BlockSpec((gpb, 2, ns, LANES), lambda s, g: (g, 0, s, 0)))
        out_shape.append(jax.ShapeDtypeStruct((N_GROUPS, 2, slots, LANES), F32))
    res = pl.pallas_call(
        functools.partial(_s5_core_kernel, seq=seq, nq=nq, paired=paired, has_fin=want_final),
        grid=(slots // ns, N_GROUP_BLOCKS),
        in_specs=in_specs,
        out_specs=out_specs,
        out_shape=out_shape,
        scratch_shapes=[pltpu.VMEM((gpb, rows, CHUNK_LANES), F32) for _ in range(4)],
        compiler_params=_params(2),
        name="s5_core",
    )(*args)
    return (res[0], res[1]) if want_final else (res[0], None)


def _s5_out_kernel(y_ref, x_ref, m_ref, wab_ref, wo_ref, lng_ref, lnb_ref, o_ref, *, tt, f):
    gate = m_ref[:, 2:3, :]
    perm = _row_perm(False)
    n = PERM_SEQS * PERM_STEPS
    lng = lng_ref[...].reshape(1, 1, D_MODEL)
    lnb = lnb_ref[...].reshape(1, 1, D_MODEL)
    for k in range(tt // (2 * PERM_STEPS)):
        t0 = 2 * k * PERM_STEPS
        hb = _gelu_tanh(y_ref[t0:t0 + 2 * PERM_STEPS].reshape(2 * n, D_MODEL)).astype(BF16)
        acc = None
        for c0 in range(0, f, MXU_TILE):
            val = _dot(hb, wab_ref[:, c0:c0 + MXU_TILE])
            gte = _dot(hb, wab_ref[:, f + c0:f + c0 + MXU_TILE])
            z = (val * _sigmoid(gte)).astype(BF16)
            zp = jnp.concatenate([_dot(perm, z[0:n]).astype(BF16), _dot(perm, z[n:2 * n]).astype(BF16)],
                                 axis=0)
            part = _dot(zp, wo_ref[c0:c0 + MXU_TILE, :])
            acc = part if acc is None else acc + part
        for hf in range(2):
            ts = slice(t0 + hf * PERM_STEPS, t0 + (hf + 1) * PERM_STEPS)
            mix = acc[hf * n:(hf + 1) * n].reshape(PERM_SEQS, PERM_STEPS, D_MODEL)
            r = DEEPNORM_ALPHA * x_ref[:, ts, :] + gate * mix
            o_ref[:, ts, :] = _layer_norm(r, lng, lnb)


def _s5_out(y, x, mods, w_glu, w_out, wl, ln_g, ln_b, tt):
    slots, seq, _ = x.shape
    f = w_out.shape[1]
    resident = pl.Buffered(1)
    tok = pl.BlockSpec((PERM_SEQS, tt, D_MODEL), lambda s, t: (s, t, 0))
    return pl.pallas_call(
        functools.partial(_s5_out_kernel, tt=tt, f=f),
        grid=(slots // PERM_SEQS, seq // tt),
        in_specs=[pl.BlockSpec((tt, PERM_SEQS, D_MODEL), lambda s, t: (t, s, 0)),
                  tok,
                  pl.BlockSpec((PERM_SEQS, 6, D_MODEL), lambda s, t: (s, 0, 0)),
                  pl.BlockSpec((None, D_MODEL, 2 * f), lambda s, t: (wl, 0, 0), pipeline_mode=resident),
                  pl.BlockSpec((None, f, D_MODEL), lambda s, t: (wl, 0, 0), pipeline_mode=resident),
                  pl.BlockSpec((1, D_MODEL), lambda s, t: (0, 0)),
                  pl.BlockSpec((1, D_MODEL), lambda s, t: (0, 0))],
        out_specs=tok,
        out_shape=jax.ShapeDtypeStruct(x.shape, F32),
        compiler_params=_params(2),
        name="s5_out",
    )(y, x, mods, w_glu, w_out, ln_g.reshape(1, D_MODEL), ln_b.reshape(1, D_MODEL))


def _ffn_kernel(x_ref, m_ref, wab_ref, wo_ref, lng_ref, lnb_ref, o_ref, h_sc, *, f):
    m = m_ref[0]
    h = x_ref[...] * (1.0 + m[4:5]) + m[3:4]
    h_sc[...] = h.astype(BF16)

    acc = None
    for c0 in range(0, f, MXU_TILE):
        a = _dot(h_sc[...], wab_ref[:, c0:c0 + MXU_TILE])
        b = _dot(h_sc[...], wab_ref[:, f + c0:f + c0 + MXU_TILE])
        z = (a * _sigmoid(a)) * b
        part = _dot(z.astype(BF16), wo_ref[c0:c0 + MXU_TILE, :])
        acc = part if acc is None else acc + part

    r = DEEPNORM_ALPHA * x_ref[...] + m[5:6] * acc
    o_ref[...] = _layer_norm(r, lng_ref[...], lnb_ref[...])


def _ffn(x, mods, w_ab, w_o, wl, ln_g, ln_b, nb, seq, tm):
    f = w_o.shape[1]
    nt = seq // tm
    per_batch = mods.shape[0] > 1
    tok = pl.BlockSpec((tm, D_MODEL), lambda b, t: (b * nt + t, 0))
    resident = pl.Buffered(1)
    return pl.pallas_call(
        functools.partial(_ffn_kernel, f=f),
        grid=(nb, nt),
        in_specs=[tok,
                  pl.BlockSpec((1, 6, D_MODEL), lambda b, t: (b if per_batch else 0, 0, 0)),
                  pl.BlockSpec((None, D_MODEL, 2 * f), lambda b, t: (wl, 0, 0), pipeline_mode=resident),
                  pl.BlockSpec((None, f, D_MODEL), lambda b, t: (wl, 0, 0), pipeline_mode=resident),
                  pl.BlockSpec((1, D_MODEL), lambda b, t: (0, 0)),
                  pl.BlockSpec((1, D_MODEL), lambda b, t: (0, 0))],
        out_specs=tok,
        out_shape=jax.ShapeDtypeStruct((nb * seq, D_MODEL), F32),
        scratch_shapes=[pltpu.VMEM((tm, D_MODEL), BF16)],
        compiler_params=_params(2),
        name="ffn",
    )(x, mods, w_ab, w_o, ln_g.reshape(1, D_MODEL), ln_b.reshape(1, D_MODEL))


def _qkv_kernel(*refs, rope):
    if rope:
        x_ref, m_ref, w_ref, qg_ref, kg_ref, cos_ref, sin_ref, q_ref, k_ref, v_ref = refs
    else:
        x_ref, m_ref, w_ref, qg_ref, kg_ref, q_ref, k_ref, v_ref = refs
    m = m_ref[0]
    h = x_ref[...] * (1.0 + m[1:2]) + m[0:1]
    qkv = _dot(h.astype(BF16), w_ref[...])
    if rope:
        cos = cos_ref[...]
        sin = sin_ref[...]
        lane = lax.broadcasted_iota(jnp.int32, cos.shape, 1)
        first = jnp.bitwise_and(lane, AXIS_PAIRS) == 0
    for hd in range(N_HEADS + N_KV_HEADS):
        xh = qkv[:, hd * HEAD_DIM:(hd + 1) * HEAD_DIM]
        gain = qg_ref[...] if hd < N_HEADS else kg_ref[...]
        n = xh * lax.rsqrt(jnp.mean(xh * xh, axis=-1, keepdims=True) + RMS_EPS) * gain
        if rope:
            up = pltpu.roll(n, HEAD_DIM - AXIS_PAIRS, 1)
            down = pltpu.roll(n, AXIS_PAIRS, 1)
            n = n * cos + jnp.where(first, up, down) * sin
        if hd < N_HEADS:
            q_ref[:, hd * HEAD_DIM:(hd + 1) * HEAD_DIM] = n.astype(BF16)
        else:
            k_ref[:, (hd - N_HEADS) * HEAD_DIM:(hd - N_HEADS + 1) * HEAD_DIM] = n
    v_ref[...] = qkv[:, D_Q + D_KV:]


def _rope_tables(seq):
    pos = jnp.arange(seq, dtype=jnp.int32)
    row = (pos // GRID_W).astype(F32)
    col = (pos % GRID_W).astype(F32)
    inv = ROPE_THETA ** (-jnp.arange(AXIS_PAIRS, dtype=F32) / AXIS_PAIRS)
    ar = row[:, None] * inv
    ac = col[:, None] * inv
    cos = jnp.concatenate([jnp.cos(ar), jnp.cos(ar), jnp.cos(ac), jnp.cos(ac)], axis=-1)
    sin = jnp.concatenate([-jnp.sin(ar), jnp.sin(ar), -jnp.sin(ac), jnp.sin(ac)], axis=-1)
    return cos, sin


def _qkv(x, mods, w_qkv, wl, q_gain, k_gain, nb, seq, tm, rope):
    nt = seq // tm
    per_batch = mods.shape[0] > 1
    in_specs = [pl.BlockSpec((tm, D_MODEL), lambda b, t: (b * nt + t, 0)),
                pl.BlockSpec((1, 6, D_MODEL), lambda b, t: (b if per_batch else 0, 0, 0)),
                pl.BlockSpec((None, D_MODEL, QKV_DIM), lambda b, t: (wl, 0, 0)),
                pl.BlockSpec((1, HEAD_DIM), lambda b, t: (0, 0)),
                pl.BlockSpec((1, HEAD_DIM), lambda b, t: (0, 0))]
    args = [x, mods, w_qkv, q_gain.reshape(1, HEAD_DIM), k_gain.reshape(1, HEAD_DIM)]
    if rope:
        cos, sin = _rope_tables(seq)
        in_specs += [pl.BlockSpec((tm, HEAD_DIM), lambda b, t: (t, 0))] * 2
        args += [cos, sin]
    n_tok = nb * seq
    return pl.pallas_call(
        functools.partial(_qkv_kernel, rope=rope),
        grid=(nb, nt),
        in_specs=in_specs,
        out_specs=[pl.BlockSpec((tm, D_Q), lambda b, t: (b * nt + t, 0)),
                   pl.BlockSpec((tm, D_KV), lambda b, t: (b * nt + t, 0)),
                   pl.BlockSpec((tm, D_KV), lambda b, t: (b * nt + t, 0))],
        out_shape=[jax.ShapeDtypeStruct((n_tok, D_Q), BF16),
                   jax.ShapeDtypeStruct((n_tok, D_KV), F32),
                   jax.ShapeDtypeStruct((n_tok, D_KV), F32)],
        compiler_params=_params(2),
        name="qkv_rope" if rope else "qkv",
    )(*args)


def _attn_kernel(*refs, has_cache):
    if has_cache:
        (q_ref, k_ref, v_ref, ck_ref, cv_ref, x_ref, m_ref, wo_ref, lng_ref, lnb_ref,
         o_ref, oh_sc) = refs
    else:
        q_ref, k_ref, v_ref, x_ref, m_ref, wo_ref, lng_ref, lnb_ref, o_ref, oh_sc = refs
    for g in range(N_KV_HEADS):
        sl = slice(g * HEAD_DIM, (g + 1) * HEAD_DIM)
        kg = k_ref[:, sl].astype(BF16)
        vg = v_ref[:, sl].astype(BF16)
        if has_cache:
            ckg = ck_ref[:, sl].astype(BF16)
            cvg = cv_ref[:, sl].astype(BF16)
        for r in range(KV_REP):
            hsl = slice((g * KV_REP + r) * HEAD_DIM, (g * KV_REP + r + 1) * HEAD_DIM)
            qh = q_ref[:, hsl]
            s1 = _dot_nt(qh, kg) * ATTN_SCALE
            mx = jnp.max(s1, axis=-1, keepdims=True)
            if has_cache:
                s2 = _dot_nt(qh, ckg) * ATTN_SCALE
                mx = jnp.maximum(mx, jnp.max(s2, axis=-1, keepdims=True))
            p1 = jnp.exp(s1 - mx)
            den = jnp.sum(p1, axis=-1, keepdims=True)
            o = _dot(p1.astype(BF16), vg)
            if has_cache:
                p2 = jnp.exp(s2 - mx)
                den = den + jnp.sum(p2, axis=-1, keepdims=True)
                o = o + _dot(p2.astype(BF16), cvg)
            oh_sc[:, hsl] = (o / den).astype(BF16)
    mix = _dot(oh_sc[...], wo_ref[...])
    m = m_ref[0]
    r = DEEPNORM_ALPHA * x_ref[...] + m[2:3] * mix
    o_ref[...] = _layer_norm(r, lng_ref[...], lnb_ref[...])


def _attention(q, k, v, cache_k, cache_v, layer_j, x, mods, w_o, ln_g, ln_b, nb, seq, tq):
    nt = seq // tq
    per_batch = mods.shape[0] > 1
    has_cache = cache_k is not None
    tok = pl.BlockSpec((tq, D_MODEL), lambda b, t: (b * nt + t, 0))
    kv = pl.BlockSpec((seq, D_KV), lambda b, t: (b, 0))
    in_specs, args = [tok, kv, kv], [q, k, v]
    if has_cache:
        past = cache_k.shape[2]
        cspec = pl.BlockSpec((None, None, past, D_KV), lambda b, t: (b, layer_j, 0, 0))
        in_specs += [cspec, cspec]
        args += [cache_k.reshape(cache_k.shape[0], cache_k.shape[1], past, D_KV),
                 cache_v.reshape(cache_v.shape[0], cache_v.shape[1], past, D_KV)]
    in_specs += [tok,
                 pl.BlockSpec((1, 6, D_MODEL), lambda b, t: (b if per_batch else 0, 0, 0)),
                 pl.BlockSpec((None, D_Q, D_MODEL), lambda b, t: (layer_j, 0, 0)),
                 pl.BlockSpec((1, D_MODEL), lambda b, t: (0, 0)),
                 pl.BlockSpec((1, D_MODEL), lambda b, t: (0, 0))]
    args += [x, mods, w_o, ln_g.reshape(1, D_MODEL), ln_b.reshape(1, D_MODEL)]
    return pl.pallas_call(
        functools.partial(_attn_kernel, has_cache=has_cache),
        grid=(nb, nt),
        in_specs=in_specs,
        out_specs=tok,
        out_shape=jax.ShapeDtypeStruct((nb * seq, D_MODEL), F32),
        scratch_shapes=[pltpu.VMEM((tq, D_Q), BF16)],
        compiler_params=_params(2),
        name="attn_cache" if has_cache else "attn",
    )(*args)


def _latent_h0(st):
    f = st[:, 0].transpose(2, 1, 0, 3)
    b = st[:, 1].transpose(2, 1, 0, 3)
    z = jnp.zeros_like(f)
    even = jnp.concatenate([f, z], axis=-1)
    odd = jnp.concatenate([z, b], axis=-1)
    h0 = jnp.stack([even, odd], axis=3)
    return h0.reshape(N_GROUPS, 2, 2 * st.shape[0], 2 * STATE_DIM)


def kernel(x_prompt, x_sample, c, cache_k, cache_v, state_s5, c_ctx, w_mod, b_mod, ln_g, ln_b, w_s5_in, s5_a_re, s5_a_im, s5_log_dt, s5_b_re, s5_b_im, s5_c_re, s5_c_im, s5_d, w_s5_glu, w_s5_out, w_qkv, q_norm_g, k_norm_g, w_o, w_ffn_in, w_ffn_out):
    nbp, seqp, _ = x_prompt.shape
    nbs, seqs, _ = x_sample.shape
    xp = x_prompt.reshape(nbp * seqp, D_MODEL)
    xs = x_sample.reshape(nbs * seqs, D_MODEL)
    s5_slots_s, s5_seq_s = 2 * nbs, seqs // 2

    cond = jnp.concatenate([c_ctx[None, :], c, jnp.zeros((8 - 1 - nbs, D_MODEL), F32)], axis=0)
    mods = _adaln(cond, w_mod, b_mod)
    mods_p = mods[:, 0:1].reshape(DEPTH, 1, 6, D_MODEL)
    mods_s = mods[:, 1:1 + nbs].reshape(DEPTH, nbs, 6, D_MODEL)

    w_in, w_glu, w_out = w_s5_in.astype(BF16), w_s5_glu.astype(BF16), w_s5_out.astype(BF16)
    wq, wo = w_qkv.astype(BF16), w_o.astype(BF16)
    w_fi, w_fo = w_ffn_in.astype(BF16), w_ffn_out.astype(BF16)

    new_k, new_v, new_s = [], [], []
    for layer in range(DEPTH):
        j = layer // 2
        mp, ms = mods_p[layer], mods_s[layer]
        lg0, lb0, lg1, lb1 = ln_g[layer, 0], ln_b[layer, 0], ln_g[layer, 1], ln_b[layer, 1]
        if layer % 2 == 0:
            w1, w2, lam = _s5_tables(s5_a_re[j], s5_a_im[j], s5_log_dt[j], s5_b_re[j], s5_b_im[j],
                                     s5_c_re[j], s5_c_im[j])
            mp_slots = jnp.broadcast_to(mp, (nbp, 6, D_MODEL))
            ms_slots = jnp.repeat(ms, 2, axis=0)
            xp3 = xp.reshape(nbp, seqp, D_MODEL)
            xs3 = xs.reshape(s5_slots_s, s5_seq_s, D_MODEL)
            up = _s5_in(xp3, mp_slots, w_in, j, 128)
            us = _s5_in(xs3, ms_slots, w_in, j, 128)
            yp, fin = _s5_core(up, w1, w2, lam, s5_d[j], None, True)
            ys, _ = _s5_core(us, w1, w2, lam, s5_d[j], _latent_h0(state_s5[:, j]), False)
            fin = fin.reshape(N_GROUPS, 2, nbp, 2, STATE_DIM)
            new_s.append(fin.transpose(2, 3, 1, 0, 4))
            xp = _s5_out(yp, xp3, mp_slots, w_glu, w_out, j, lg0, lb0, 128).reshape(nbp * seqp, D_MODEL)
            xs = _s5_out(ys, xs3, ms_slots, w_glu, w_out, j, lg0, lb0, 128).reshape(nbs * seqs, D_MODEL)
        else:
            qp, kp, vp = _qkv(xp, mp, wq, j, q_norm_g[j], k_norm_g[j], nbp, seqp, 256, False)
            qs, ks, vs = _qkv(xs, ms, wq, j, q_norm_g[j], k_norm_g[j], nbs, seqs, 512, True)
            new_k.append(kp.reshape(nbp, seqp, N_KV_HEADS, HEAD_DIM))
            new_v.append(vp.reshape(nbp, seqp, N_KV_HEADS, HEAD_DIM))
            xp = _attention(qp, kp, vp, None, None, j, xp, mp, wo, lg0, lb0, nbp, seqp, 256)
            xs = _attention(qs, ks, vs, cache_k, cache_v, j, xs, ms, wo, lg0, lb0, nbs, seqs, 256)
        xp = _ffn(xp, mp, w_fi, w_fo, layer, lg1, lb1, 1, nbp * seqp, 512)
        xs = _ffn(xs, ms, w_fi, w_fo, layer, lg1, lb1, nbs, seqs, 512)

    y_prompt = xp.reshape(nbp, seqp, D_MODEL)
    y_sample = xs.reshape(nbs, seqs, D_MODEL)
    return (y_prompt, y_sample, jnp.stack(new_k, axis=1), jnp.stack(new_v, axis=1),
            jnp.stack(new_s, axis=1))
```

```python
import functools
import math

import jax
import jax.numpy as jnp
from jax import lax
from jax.experimental import pallas as pl
from jax.experimental.pallas import tpu as pltpu

F32 = jnp.float32
BF16 = jnp.bfloat16

D_MODEL = 1024
DEPTH = 4
N_GROUPS = 64
GROUP_CH = 16
STATE_DIM = 64
HEAD_DIM = 128
N_HEADS = 8
N_KV_HEADS = 2
KV_REP = N_HEADS // N_KV_HEADS
D_Q = N_HEADS * HEAD_DIM
D_KV = N_KV_HEADS * HEAD_DIM
QKV_DIM = D_Q + 2 * D_KV
GRID_W = 64
ROPE_THETA = 10000.0
AXIS_PAIRS = HEAD_DIM // 4
ATTN_SCALE = HEAD_DIM ** -0.5
DEEPNORM_ALPHA = (2.0 * DEPTH) ** 0.25
LN_EPS = 1e-6
RMS_EPS = 1e-6

V7X_VMEM_LIMIT_BYTES = 56 * 1024 * 1024
LANES = 128
SUBLANES = 8
MXU_TILE = 256
GROUPS_PER_BLOCK = LANES // GROUP_CH
N_GROUP_BLOCKS = N_GROUPS // GROUPS_PER_BLOCK
S5_CHUNK = MXU_TILE // GROUP_CH
CHUNK_LANES = S5_CHUNK * GROUP_CH
FOLD_UNROLL = 4
PERM_SEQS = SUBLANES
PERM_STEPS = MXU_TILE // PERM_SEQS
NT_DIMS = (((1,), (1,)), ((), ()))


def _params(n_axes):
    return pltpu.CompilerParams(dimension_semantics=("arbitrary",) * n_axes,
                                vmem_limit_bytes=V7X_VMEM_LIMIT_BYTES)


def _sigmoid(x):
    return 1.0 / (1.0 + jnp.exp(-x))


def _gelu_tanh(x):
    cdf = 0.5 * (1.0 + jnp.tanh(math.sqrt(2.0 / math.pi) * (x + 0.044715 * (x * x * x))))
    return x * cdf


def _layer_norm(r, g, b):
    mu = jnp.mean(r, axis=-1, keepdims=True)
    d = r - mu
    var = jnp.mean(d * d, axis=-1, keepdims=True)
    return d * lax.rsqrt(var + LN_EPS) * g + b


def _dot(a, b):
    return jnp.dot(a, b, preferred_element_type=F32)


def _dot_nt(a, b):
    return lax.dot_general(a, b, NT_DIMS, preferred_element_type=F32)


def _adaln_kernel(c_ref, w_ref, b_ref, o_ref):
    c = c_ref[...]
    s = c * _sigmoid(c)
    o_ref[0] = _dot(s.astype(BF16), w_ref[0].astype(BF16)) + b_ref[0]


def _adaln(cond, w_mod, b_mod):
    tn = 1536
    n = 6 * D_MODEL
    return pl.pallas_call(
        _adaln_kernel,
        grid=(DEPTH, n // tn),
        in_specs=[pl.BlockSpec((8, D_MODEL), lambda l, j: (0, 0)),
                  pl.BlockSpec((1, D_MODEL, tn), lambda l, j: (l, 0, j)),
                  pl.BlockSpec((1, 1, tn), lambda l, j: (l, 0, j))],
        out_specs=pl.BlockSpec((1, 8, tn), lambda l, j: (l, 0, j)),
        out_shape=jax.ShapeDtypeStruct((DEPTH, 8, n), F32),
        compiler_params=_params(2),
        name="adaln",
    )(cond, w_mod, b_mod.reshape(DEPTH, 1, n))


def _s5_prep_kernel(a_ref, ldt_ref, bt_ref, c_ref, w1_ref, w2_ref, lam_ref):
    half = STATE_DIM
    lane1 = lax.broadcasted_iota(jnp.int32, (1, LANES), 1)
    sgn1 = jnp.where(lane1 < half, -1.0, 1.0)
    lane_h = lax.broadcasted_iota(jnp.int32, (GROUP_CH, LANES), 1)
    first_h = lane_h < half
    conj_h = jnp.where(first_h, 1.0, -1.0)
    lane_c = lax.broadcasted_iota(jnp.int32, (GROUP_CH, CHUNK_LANES), 1)
    t = S5_CHUNK

    def cmul(pr, pi, x):
        return pr * x + (pi * sgn1) * pltpu.roll(x, half, 1)

    def pack_states(f, b):
        return (jnp.where(first_h, f, pltpu.roll(b, half, 1)),
                jnp.where(first_h, pltpu.roll(f, half, 1), b))

    def group(g, carry):
        kt, qs, cks, lam_t = [], [], [], []
        for d in range(2):
            a_re = a_ref[0, d, g]
            a_im = a_ref[1, d, g]
            dt = jnp.exp(ldt_ref[d, g])
            mag = jnp.exp(dt * a_re)
            lr = mag * jnp.cos(dt * a_im)
            li = mag * jnp.sin(dt * a_im)
            den = a_re * a_re + a_im * a_im
            nr = lr - 1.0
            k_re = (nr * a_re + li * a_im) / den
            k_im = (li * a_re - nr * a_im) / den
            bb = cmul(k_re, k_im, bt_ref[d, g])
            cc = c_ref[d, g]
            pr = jnp.ones((1, LANES), F32)
            pi = jnp.zeros((1, LANES), F32)
            ck, q = [], []
            for k in range(t + 1):
                ck.append(cmul(pr, pi, cc))
                if k < t:
                    q.append(cmul(pr, pi, bb))
                    pr, pi = pr * lr - pi * li, pr * li + pi * lr
            lam_t.append((pr, pi))
            order = range(t) if d == 0 else range(t - 1, -1, -1)
            rhs = jnp.concatenate([ck[k] for k in order], axis=0).astype(BF16)
            kt.append(_dot_nt((bb * conj_h).astype(BF16), rhs))
            qs.append(q)
            cks.append(ck)
        for j in range(t):
            rows = slice(j * GROUP_CH, (j + 1) * GROUP_CH)
            tf = kt[0] if j == 0 else pltpu.roll(kt[0], GROUP_CH * j, 1)
            tf = jnp.where(lane_c >= GROUP_CH * j, tf, 0.0)
            back = GROUP_CH * (t - 1 - j)
            tb = kt[1] if back == 0 else pltpu.roll(kt[1], CHUNK_LANES - back, 1)
            tb = jnp.where(lane_c < GROUP_CH * (j + 1), tb, 0.0)
            w1_ref[g, rows, 0:CHUNK_LANES] = (tf + tb).astype(BF16)
            s_re, s_im = pack_states(qs[0][t - 1 - j], qs[1][j])
            w1_ref[g, rows, CHUNK_LANES:CHUNK_LANES + LANES] = s_re.astype(BF16)
            w1_ref[g, rows, CHUNK_LANES + LANES:CHUNK_LANES + 2 * LANES] = s_im.astype(BF16)
            c_re, c_im = pack_states(cks[0][j + 1] * conj_h, cks[1][t - j] * conj_h)
            w2_ref[g, rows, 0:LANES] = c_re.astype(BF16)
            w2_ref[g, rows, LANES:2 * LANES] = c_im.astype(BF16)
        fwd1 = lane1 < half
        lam_ref[g, 0:1, :] = jnp.where(fwd1, lam_t[0][0], lam_t[1][0])
        lam_ref[g, 1:2, :] = jnp.where(fwd1, lam_t[0][1], lam_t[1][1])
        return carry

    lax.fori_loop(0, GROUPS_PER_BLOCK, group, 0, unroll=2)


def _s5_tables(a_re, a_im, log_dt, b_re, b_im, c_re, c_im):
    g, p, h = N_GROUPS, STATE_DIM, GROUP_CH
    dup = lambda x: jnp.concatenate([x, x], axis=-1)
    a2 = jnp.stack([dup(a_re), dup(a_im)]).reshape(2, 2, g, 1, 2 * p)
    bt = jnp.concatenate([b_re.transpose(0, 1, 3, 2), b_im.transpose(0, 1, 3, 2)], axis=-1)
    cc = jnp.concatenate([c_re, c_im], axis=-1)
    gb = GROUPS_PER_BLOCK
    return pl.pallas_call(
        _s5_prep_kernel,
        grid=(N_GROUP_BLOCKS,),
        in_specs=[pl.BlockSpec((2, 2, gb, 1, 2 * p), lambda i: (0, 0, i, 0, 0)),
                  pl.BlockSpec((2, gb, 1, 1), lambda i: (0, i, 0, 0)),
                  pl.BlockSpec((2, gb, h, 2 * p), lambda i: (0, i, 0, 0)),
                  pl.BlockSpec((2, gb, h, 2 * p), lambda i: (0, i, 0, 0))],
        out_specs=[pl.BlockSpec((gb, CHUNK_LANES, CHUNK_LANES + 2 * LANES), lambda i: (i, 0, 0)),
                   pl.BlockSpec((gb, CHUNK_LANES, 2 * LANES), lambda i: (i, 0, 0)),
                   pl.BlockSpec((gb, 2, LANES), lambda i: (i, 0, 0))],
        out_shape=[jax.ShapeDtypeStruct((g, CHUNK_LANES, CHUNK_LANES + 2 * LANES), BF16),
                   jax.ShapeDtypeStruct((g, CHUNK_LANES, 2 * LANES), BF16),
                   jax.ShapeDtypeStruct((g, 2, LANES), F32)],
        compiler_params=_params(1),
        name="s5_prep",
    )(a2, log_dt.reshape(2, g, 1, 1), bt, cc)


def _row_perm(to_time_major):
    n = PERM_SEQS * PERM_STEPS
    r = lax.broadcasted_iota(jnp.int32, (n, n), 0)
    c = lax.broadcasted_iota(jnp.int32, (n, n), 1)
    if to_time_major:
        src = jnp.bitwise_and(r, PERM_SEQS - 1) * PERM_STEPS + lax.shift_right_logical(r, 3)
    else:
        src = jnp.bitwise_and(r, PERM_STEPS - 1) * PERM_SEQS + lax.shift_right_logical(r, 5)
    return jnp.where(c == src, 1.0, 0.0).astype(BF16)


def _s5_in_kernel(x_ref, m_ref, w_ref, o_ref, *, tt):
    scale = 1.0 + m_ref[:, 1:2, :]
    shift = m_ref[:, 0:1, :]
    perm = _row_perm(True)
    n = PERM_SEQS * PERM_STEPS
    for k in range(tt // (2 * PERM_STEPS)):
        pieces = []
        for hf in range(2):
            t0 = (2 * k + hf) * PERM_STEPS
            h = x_ref[:, t0:t0 + PERM_STEPS, :] * scale + shift
            hb = h.reshape(n, D_MODEL).astype(BF16)
            pieces.append(_dot(perm, hb).astype(BF16))
        u = _dot(jnp.concatenate(pieces, axis=0), w_ref[...])
        o_ref[2 * k * PERM_STEPS:(2 * k + 2) * PERM_STEPS] = u.reshape(2 * PERM_STEPS, PERM_SEQS, D_MODEL)


def _s5_in(x, mods, w_in, wl, tt):
    slots, seq, _ = x.shape
    return pl.pallas_call(
        functools.partial(_s5_in_kernel, tt=tt),
        grid=(slots // PERM_SEQS, seq // tt),
        in_specs=[pl.BlockSpec((PERM_SEQS, tt, D_MODEL), lambda s, t: (s, t, 0)),
                  pl.BlockSpec((PERM_SEQS, 6, D_MODEL), lambda s, t: (s, 0, 0)),
                  pl.BlockSpec((None, D_MODEL, D_MODEL), lambda s, t: (wl, 0, 0))],
        out_specs=pl.BlockSpec((tt, PERM_SEQS, D_MODEL), lambda s, t: (t, s, 0)),
        out_shape=jax.ShapeDtypeStruct((seq, slots, D_MODEL), F32),
        compiler_params=_params(2),
        name="s5_in",
    )(x, mods, w_in)


def _block_transpose8(v):
    lane = lax.broadcasted_iota(jnp.int32, (SUBLANES, LANES), 1)
    v = list(v)
    for d in (4, 2, 1):
        low = jnp.bitwise_and(lane, GROUP_CH * d) == 0
        nxt = list(v)
        for i in range(8):
            if i & d == 0:
                a, b = v[i], v[i + d]
                nxt[i] = jnp.where(low, a, pltpu.roll(b, GROUP_CH * d, 1))
                nxt[i + d] = jnp.where(low, pltpu.roll(a, LANES - GROUP_CH * d, 1), b)
        v = nxt
    return v


def _s5_core_kernel(*refs, seq, nq, paired, has_fin):
    refs = list(refs)
    u_ref, w1_ref, w2_ref, lam_ref, d_ref = refs[:5]
    pos = 5
    h0_ref = None
    if paired:
        h0_ref = refs[pos]
        pos += 1
    y_ref = refs[pos]
    pos += 1
    fin_ref = refs[pos] if has_fin else None
    z_sc, yt_sc, bs_sc, sp_sc = refs[-4 - int(paired):][:4]
    sp2_sc = refs[-1] if paired else None
    gpb = GROUPS_PER_BLOCK
    nc = seq // S5_CHUNK
    ns = nq * SUBLANES
    half = STATE_DIM

    def fold(c, carry):
        for qi in range(nq):
            sl = slice(qi * SUBLANES, (qi + 1) * SUBLANES)
            r0 = pl.multiple_of(c * ns + qi * SUBLANES, SUBLANES)
            for hf in range(2):
                v = [u_ref[c * S5_CHUNK + hf * 8 + t, sl, :] for t in range(8)]
                w = _block_transpose8(v)
                for g in range(gpb):
                    z_sc[g, pl.ds(r0, SUBLANES), hf * LANES:(hf + 1) * LANES] = w[g]
        return carry

    lax.fori_loop(0, nc, fold, 0, unroll=FOLD_UNROLL // nq)

    for g in range(gpb):
        m1 = _dot(z_sc[g].astype(BF16), w1_ref[g])
        yt_sc[g] = m1[:, 0:CHUNK_LANES]
        bs_sc[g] = m1[:, CHUNK_LANES:]

    lane = lax.broadcasted_iota(jnp.int32, (SUBLANES, LANES), 1)
    fwd = jnp.bitwise_and(lane, half) == 0
    lam = [(jnp.broadcast_to(lam_ref[g, 0:1, :], (SUBLANES, LANES)),
            jnp.broadcast_to(lam_ref[g, 1:2, :], (SUBLANES, LANES))) for g in range(gpb)]

    def run_pass(init, dst):
        def body(i, carry):
            out = []
            for g in range(gpb):
                l_re, l_im = lam[g]
                for qi in range(nq):
                    s_re, s_im = carry[2 * (g * nq + qi)], carry[2 * (g * nq + qi) + 1]
                    ri = pl.ds(pl.multiple_of(i * ns + qi * SUBLANES, SUBLANES), SUBLANES)
                    rr = pl.ds(pl.multiple_of((nc - 1 - i) * ns + qi * SUBLANES, SUBLANES), SUBLANES)
                    dst[g, ri, 0:half] = s_re[:, 0:half]
                    dst[g, ri, LANES:LANES + half] = s_im[:, 0:half]
                    dst[g, rr, half:LANES] = s_re[:, half:LANES]
                    dst[g, rr, LANES + half:2 * LANES] = s_im[:, half:LANES]
                    x_re = jnp.where(fwd, bs_sc[g, ri, 0:LANES], bs_sc[g, rr, 0:LANES])
                    x_im = jnp.where(fwd, bs_sc[g, ri, LANES:2 * LANES], bs_sc[g, rr, LANES:2 * LANES])
                    out.append(l_re * s_re - l_im * s_im + x_re)
                    out.append(l_re * s_im + l_im * s_re + x_im)
            return tuple(out)

        return lax.fori_loop(0, nc, body, init)

    if paired:
        init = []
        for g in range(gpb):
            init += [h0_ref[g, 0], h0_ref[g, 1]]
        mid = run_pass(tuple(init), sp_sc)
        handed = [jnp.where(fwd, pltpu.roll(s, 1, 0), pltpu.roll(s, SUBLANES - 1, 0)) for s in mid]
        fin = run_pass(tuple(handed), sp2_sc)
    else:
        zero = jnp.zeros((SUBLANES, LANES), F32)
        fin = run_pass((zero,) * (2 * gpb * nq), sp_sc)

    if has_fin:
        for g in range(gpb):
            for qi in range(nq):
                sl = slice(qi * SUBLANES, (qi + 1) * SUBLANES)
                fin_ref[g, 0, sl, :] = fin[2 * (g * nq + qi)]
                fin_ref[g, 1, sl, :] = fin[2 * (g * nq + qi) + 1]

    if paired:
        shape = (nc * ns, 2 * LANES)
        row = lax.broadcasted_iota(jnp.int32, shape, 0)
        col = lax.broadcasted_iota(jnp.int32, shape, 1)
        first_pass = (jnp.bitwise_and(row, 1) == 0) == (jnp.bitwise_and(col, half) == 0)
    for g in range(gpb):
        states = sp_sc[g]
        if paired:
            states = jnp.where(first_pass, states, sp2_sc[g])
        yt_sc[g] = yt_sc[g] + _dot_nt(states.astype(BF16), w2_ref[g])

    d = jnp.broadcast_to(d_ref[...], (SUBLANES, LANES))

    def unfold(c, carry):
        for qi in range(nq):
            sl = slice(qi * SUBLANES, (qi + 1) * SUBLANES)
            r0 = pl.multiple_of(c * ns + qi * SUBLANES, SUBLANES)
            for hf in range(2):
                w = [yt_sc[g, pl.ds(r0, SUBLANES), hf * LANES:(hf + 1) * LANES] for g in range(gpb)]
                v = _block_transpose8(w)
                for t in range(8):
                    step = c * S5_CHUNK + hf * 8 + t
                    y_ref[step, sl, :] = v[t] + d * u_ref[step, sl, :]
        return carry

    lax.fori_loop(0, nc, unfold, 0, unroll=FOLD_UNROLL // nq)


def _s5_core(u, w1, w2, lam, d_skip, h0, want_final):
    seq, slots, _ = u.shape
    paired = h0 is not None
    nq = 1 if paired else slots // SUBLANES
    ns = nq * SUBLANES
    gpb = GROUPS_PER_BLOCK
    rows = (seq // S5_CHUNK) * ns
    u_spec = pl.BlockSpec((seq, ns, LANES), lambda s, g: (0, s, g))
    in_specs = [u_spec,
                pl.BlockSpec((gpb, CHUNK_LANES, CHUNK_LANES + 2 * LANES), lambda s, g: (g, 0, 0)),
                pl.BlockSpec((gpb, CHUNK_LANES, 2 * LANES), lambda s, g: (g, 0, 0)),
                pl.BlockSpec((gpb, 2, LANES), lambda s, g: (g, 0, 0)),
                pl.BlockSpec((1, LANES), lambda s, g: (0, g))]
    args = [u, w1, w2, lam, d_skip.reshape(1, D_MODEL)]
    if paired:
        in_specs.append(pl.BlockSpec((gpb, 2, SUBLANES, LANES), lambda s, g: (g, 0, s, 0)))
        args.append(h0)
    out_specs = [u_spec]
    out_shape = [jax.ShapeDtypeStruct(u.shape, F32)]
    if want_final:
        out_specs.append(pl.BlockSpec((gpb, 2, ns, LANES), lambda s, g: (g, 0, s, 0)))
        out_shape.append(jax.ShapeDtypeStruct((N_GROUPS, 2, slots, LANES), F32))
    res = pl.pallas_call(
        functools.partial(_s5_core_kernel, seq=seq, nq=nq, paired=paired, has_fin=want_final),
        grid=(slots // ns, N_GROUP_BLOCKS),
        in_specs=in_specs,
        out_specs=out_specs,
        out_shape=out_shape,
        scratch_shapes=[pltpu.VMEM((gpb, rows, CHUNK_LANES), F32) for _ in range(5 if paired else 4)],
        compiler_params=_params(2),
        name="s5_core",
    )(*args)
    return (res[0], res[1]) if want_final else (res[0], None)


def _s5_out_kernel(y_ref, x_ref, m_ref, wab_ref, wo_ref, lng_ref, lnb_ref, o_ref, *, tt, f):
    gate = m_ref[:, 2:3, :]
    perm = _row_perm(False)
    n = PERM_SEQS * PERM_STEPS
    lng = lng_ref[...].reshape(1, 1, D_MODEL)
    lnb = lnb_ref[...].reshape(1, 1, D_MODEL)
    for k in range(tt // (2 * PERM_STEPS)):
        t0 = 2 * k * PERM_STEPS
        hb = _gelu_tanh(y_ref[t0:t0 + 2 * PERM_STEPS].reshape(2 * n, D_MODEL)).astype(BF16)
        acc = None
        for c0 in range(0, f, MXU_TILE):
            val = _dot(hb, wab_ref[:, c0:c0 + MXU_TILE])
            gte = _dot(hb, wab_ref[:, f + c0:f + c0 + MXU_TILE])
            z = (val * _sigmoid(gte)).astype(BF16)
            zp = jnp.concatenate([_dot(perm, z[0:n]).astype(BF16), _dot(perm, z[n:2 * n]).astype(BF16)],
                                 axis=0)
            part = _dot(zp, wo_ref[c0:c0 + MXU_TILE, :])
            acc = part if acc is None else acc + part
        for hf in range(2):
            ts = slice(t0 + hf * PERM_STEPS, t0 + (hf + 1) * PERM_STEPS)
            mix = acc[hf * n:(hf + 1) * n].reshape(PERM_SEQS, PERM_STEPS, D_MODEL)
            r = DEEPNORM_ALPHA * x_ref[:, ts, :] + gate * mix
            o_ref[:, ts, :] = _layer_norm(r, lng, lnb)


def _s5_out(y, x, mods, w_glu, w_out, wl, ln_g, ln_b, tt):
    slots, seq, _ = x.shape
    f = w_out.shape[1]
    resident = pl.Buffered(1)
    tok = pl.BlockSpec((PERM_SEQS, tt, D_MODEL), lambda s, t: (s, t, 0))
    return pl.pallas_call(
        functools.partial(_s5_out_kernel, tt=tt, f=f),
        grid=(slots // PERM_SEQS, seq // tt),
        in_specs=[pl.BlockSpec((tt, PERM_SEQS, D_MODEL), lambda s, t: (t, s, 0)),
                  tok,
                  pl.BlockSpec((PERM_SEQS, 6, D_MODEL), lambda s, t: (s, 0, 0)),
                  pl.BlockSpec((None, D_MODEL, 2 * f), lambda s, t: (wl, 0, 0), pipeline_mode=resident),
                  pl.BlockSpec((None, f, D_MODEL), lambda s, t: (wl, 0, 0), pipeline_mode=resident),
                  pl.BlockSpec((1, D_MODEL), lambda s, t: (0, 0)),
                  pl.BlockSpec((1, D_MODEL), lambda s, t: (0, 0))],
        out_specs=tok,
        out_shape=jax.ShapeDtypeStruct(x.shape, F32),
        compiler_params=_params(2),
        name="s5_out",
    )(y, x, mods, w_glu, w_out, ln_g.reshape(1, D_MODEL), ln_b.reshape(1, D_MODEL))


def _ffn_kernel(x_ref, m_ref, wab_ref, wo_ref, lng_ref, lnb_ref, o_ref, h_sc, *, f):
    m = m_ref[0]
    h = x_ref[...] * (1.0 + m[4:5]) + m[3:4]
    h_sc[...] = h.astype(BF16)

    acc = None
    for c0 in range(0, f, MXU_TILE):
        a = _dot(h_sc[...], wab_ref[:, c0:c0 + MXU_TILE])
        b = _dot(h_sc[...], wab_ref[:, f + c0:f + c0 + MXU_TILE])
        z = (a * _sigmoid(a)) * b
        part = _dot(z.astype(BF16), wo_ref[c0:c0 + MXU_TILE, :])
        acc = part if acc is None else acc + part

    r = DEEPNORM_ALPHA * x_ref[...] + m[5:6] * acc
    o_ref[...] = _layer_norm(r, lng_ref[...], lnb_ref[...])


def _ffn(x, mods, w_ab, w_o, wl, ln_g, ln_b, nb, seq, tm):
    f = w_o.shape[1]
    nt = seq // tm
    per_batch = mods.shape[0] > 1
    tok = pl.BlockSpec((tm, D_MODEL), lambda b, t: (b * nt + t, 0))
    resident = pl.Buffered(1)
    return pl.pallas_call(
        functools.partial(_ffn_kernel, f=f),
        grid=(nb, nt),
        in_specs=[tok,
                  pl.BlockSpec((1, 6, D_MODEL), lambda b, t: (b if per_batch else 0, 0, 0)),
                  pl.BlockSpec((None, D_MODEL, 2 * f), lambda b, t: (wl, 0, 0), pipeline_mode=resident),
                  pl.BlockSpec((None, f, D_MODEL), lambda b, t: (wl, 0, 0), pipeline_mode=resident),
                  pl.BlockSpec((1, D_MODEL), lambda b, t: (0, 0)),
                  pl.BlockSpec((1, D_MODEL), lambda b, t: (0, 0))],
        out_specs=tok,
        out_shape=jax.ShapeDtypeStruct((nb * seq, D_MODEL), F32),
        scratch_shapes=[pltpu.VMEM((tm, D_MODEL), BF16)],
        compiler_params=_params(2),
        name="ffn",
    )(x, mods, w_ab, w_o, ln_g.reshape(1, D_MODEL), ln_b.reshape(1, D_MODEL))


def _qkv_kernel(*refs, rope):
    if rope:
        x_ref, m_ref, w_ref, qg_ref, kg_ref, cos_ref, sin_ref, q_ref, k_ref, v_ref = refs
    else:
        x_ref, m_ref, w_ref, qg_ref, kg_ref, q_ref, k_ref, v_ref = refs
    m = m_ref[0]
    h = x_ref[...] * (1.0 + m[1:2]) + m[0:1]
    qkv = _dot(h.astype(BF16), w_ref[...])
    if rope:
        cos = cos_ref[...]
        sin = sin_ref[...]
        lane = lax.broadcasted_iota(jnp.int32, cos.shape, 1)
        first = jnp.bitwise_and(lane, AXIS_PAIRS) == 0
    for hd in range(N_HEADS + N_KV_HEADS):
        xh = qkv[:, hd * HEAD_DIM:(hd + 1) * HEAD_DIM]
        gain = qg_ref[...] if hd < N_HEADS else kg_ref[...]
        n = xh * lax.rsqrt(jnp.mean(xh * xh, axis=-1, keepdims=True) + RMS_EPS) * gain
        if rope:
            up = pltpu.roll(n, HEAD_DIM - AXIS_PAIRS, 1)
            down = pltpu.roll(n, AXIS_PAIRS, 1)
            n = n * cos + jnp.where(first, up, down) * sin
        if hd < N_HEADS:
            q_ref[:, hd * HEAD_DIM:(hd + 1) * HEAD_DIM] = n.astype(BF16)
        else:
            k_ref[:, (hd - N_HEADS) * HEAD_DIM:(hd - N_HEADS + 1) * HEAD_DIM] = n
    v_ref[...] = qkv[:, D_Q + D_KV:]


def _rope_tables(seq):
    pos = jnp.arange(seq, dtype=jnp.int32)
    row = (pos // GRID_W).astype(F32)
    col = (pos % GRID_W).astype(F32)
    inv = ROPE_THETA ** (-jnp.arange(AXIS_PAIRS, dtype=F32) / AXIS_PAIRS)
    ar = row[:, None] * inv
    ac = col[:, None] * inv
    cos = jnp.concatenate([jnp.cos(ar), jnp.cos(ar), jnp.cos(ac), jnp.cos(ac)], axis=-1)
    sin = jnp.concatenate([-jnp.sin(ar), jnp.sin(ar), -jnp.sin(ac), jnp.sin(ac)], axis=-1)
    return cos, sin


def _qkv(x, mods, w_qkv, wl, q_gain, k_gain, nb, seq, tm, rope):
    nt = seq // tm
    per_batch = mods.shape[0] > 1
    in_specs = [pl.BlockSpec((tm, D_MODEL), lambda b, t: (b * nt + t, 0)),
                pl.BlockSpec((1, 6, D_MODEL), lambda b, t: (b if per_batch else 0, 0, 0)),
                pl.BlockSpec((None, D_MODEL, QKV_DIM), lambda b, t: (wl, 0, 0)),
                pl.BlockSpec((1, HEAD_DIM), lambda b, t: (0, 0)),
                pl.BlockSpec((1, HEAD_DIM), lambda b, t: (0, 0))]
    args = [x, mods, w_qkv, q_gain.reshape(1, HEAD_DIM), k_gain.reshape(1, HEAD_DIM)]
    if rope:
        cos, sin = _rope_tables(seq)
        in_specs += [pl.BlockSpec((tm, HEAD_DIM), lambda b, t: (t, 0))] * 2
        args += [cos, sin]
    n_tok = nb * seq
    return pl.pallas_call(
        functools.partial(_qkv_kernel, rope=rope),
        grid=(nb, nt),
        in_specs=in_specs,
        out_specs=[pl.BlockSpec((tm, D_Q), lambda b, t: (b * nt + t, 0)),
                   pl.BlockSpec((tm, D_KV), lambda b, t: (b * nt + t, 0)),
                   pl.BlockSpec((tm, D_KV), lambda b, t: (b * nt + t, 0))],
        out_shape=[jax.ShapeDtypeStruct((n_tok, D_Q), BF16),
                   jax.ShapeDtypeStruct((n_tok, D_KV), F32),
                   jax.ShapeDtypeStruct((n_tok, D_KV), F32)],
        compiler_params=_params(2),
        name="qkv_rope" if rope else "qkv",
    )(*args)


def _attn_kernel(*refs, has_cache):
    if has_cache:
        (q_ref, k_ref, v_ref, ck_ref, cv_ref, x_ref, m_ref, wo_ref, lng_ref, lnb_ref,
         o_ref, oh_sc) = refs
    else:
        q_ref, k_ref, v_ref, x_ref, m_ref, wo_ref, lng_ref, lnb_ref, o_ref, oh_sc = refs
    for g in range(N_KV_HEADS):
        sl = slice(g * HEAD_DIM, (g + 1) * HEAD_DIM)
        kg = k_ref[:, sl].astype(BF16)
        vg = v_ref[:, sl].astype(BF16)
        if has_cache:
            ckg = ck_ref[:, sl].astype(BF16)
            cvg = cv_ref[:, sl].astype(BF16)
        for r in range(KV_REP):
            hsl = slice((g * KV_REP + r) * HEAD_DIM, (g * KV_REP + r + 1) * HEAD_DIM)
            qh = q_ref[:, hsl]
            s1 = _dot_nt(qh, kg) * ATTN_SCALE
            mx = jnp.max(s1, axis=-1, keepdims=True)
            if has_cache:
                s2 = _dot_nt(qh, ckg) * ATTN_SCALE
                mx = jnp.maximum(mx, jnp.max(s2, axis=-1, keepdims=True))
            p1 = jnp.exp(s1 - mx)
            den = jnp.sum(p1, axis=-1, keepdims=True)
            o = _dot(p1.astype(BF16), vg)
            if has_cache:
                p2 = jnp.exp(s2 - mx)
                den = den + jnp.sum(p2, axis=-1, keepdims=True)
                o = o + _dot(p2.astype(BF16), cvg)
            oh_sc[:, hsl] = (o / den).astype(BF16)
    mix = _dot(oh_sc[...], wo_ref[...])
    m = m_ref[0]
    r = DEEPNORM_ALPHA * x_ref[...] + m[2:3] * mix
    o_ref[...] = _layer_norm(r, lng_ref[...], lnb_ref[...])


def _attention(q, k, v, cache_k, cache_v, layer_j, x, mods, w_o, ln_g, ln_b, nb, seq, tq):
    nt = seq // tq
    per_batch = mods.shape[0] > 1
    has_cache = cache_k is not None
    tok = pl.BlockSpec((tq, D_MODEL), lambda b, t: (b * nt + t, 0))
    kv = pl.BlockSpec((seq, D_KV), lambda b, t: (b, 0))
    in_specs, args = [tok, kv, kv], [q, k, v]
    if has_cache:
        past = cache_k.shape[2]
        cspec = pl.BlockSpec((None, None, past, D_KV), lambda b, t: (b, layer_j, 0, 0))
        in_specs += [cspec, cspec]
        args += [cache_k.reshape(cache_k.shape[0], cache_k.shape[1], past, D_KV),
                 cache_v.reshape(cache_v.shape[0], cache_v.shape[1], past, D_KV)]
    in_specs += [tok,
                 pl.BlockSpec((1, 6, D_MODEL), lambda b, t: (b if per_batch else 0, 0, 0)),
                 pl.BlockSpec((None, D_Q, D_MODEL), lambda b, t: (layer_j, 0, 0)),
                 pl.BlockSpec((1, D_MODEL), lambda b, t: (0, 0)),
                 pl.BlockSpec((1, D_MODEL), lambda b, t: (0, 0))]
    args += [x, mods, w_o, ln_g.reshape(1, D_MODEL), ln_b.reshape(1, D_MODEL)]
    return pl.pallas_call(
        functools.partial(_attn_kernel, has_cache=has_cache),
        grid=(nb, nt),
        in_specs=in_specs,
        out_specs=tok,
        out_shape=jax.ShapeDtypeStruct((nb * seq, D_MODEL), F32),
        scratch_shapes=[pltpu.VMEM((tq, D_Q), BF16)],
        compiler_params=_params(2),
        name="attn_cache" if has_cache else "attn",
    )(*args)


def _latent_h0(st):
    f = st[:, 0].transpose(2, 1, 0, 3)
    b = st[:, 1].transpose(2, 1, 0, 3)
    z = jnp.zeros_like(f)
    even = jnp.concatenate([f, z], axis=-1)
    odd = jnp.concatenate([z, b], axis=-1)
    h0 = jnp.stack([even, odd], axis=3)
    return h0.reshape(N_GROUPS, 2, 2 * st.shape[0], 2 * STATE_DIM)


def kernel(x_prompt, x_sample, c, cache_k, cache_v, state_s5, c_ctx, w_mod, b_mod, ln_g, ln_b, w_s5_in, s5_a_re, s5_a_im, s5_log_dt, s5_b_re, s5_b_im, s5_c_re, s5_c_im, s5_d, w_s5_glu, w_s5_out, w_qkv, q_norm_g, k_norm_g, w_o, w_ffn_in, w_ffn_out):
    nbp, seqp, _ = x_prompt.shape
    nbs, seqs, _ = x_sample.shape
    xp = x_prompt.reshape(nbp * seqp, D_MODEL)
    xs = x_sample.reshape(nbs * seqs, D_MODEL)
    s5_slots_s, s5_seq_s = 2 * nbs, seqs // 2

    cond = jnp.concatenate([c_ctx[None, :], c, jnp.zeros((8 - 1 - nbs, D_MODEL), F32)], axis=0)
    mods = _adaln(cond, w_mod, b_mod)
    mods_p = mods[:, 0:1].reshape(DEPTH, 1, 6, D_MODEL)
    mods_s = mods[:, 1:1 + nbs].reshape(DEPTH, nbs, 6, D_MODEL)

    w_in, w_glu, w_out = w_s5_in.astype(BF16), w_s5_glu.astype(BF16), w_s5_out.astype(BF16)
    wq, wo = w_qkv.astype(BF16), w_o.astype(BF16)
    w_fi, w_fo = w_ffn_in.astype(BF16), w_ffn_out.astype(BF16)

    new_k, new_v, new_s = [], [], []
    for layer in range(DEPTH):
        j = layer // 2
        mp, ms = mods_p[layer], mods_s[layer]
        lg0, lb0, lg1, lb1 = ln_g[layer, 0], ln_b[layer, 0], ln_g[layer, 1], ln_b[layer, 1]
        if layer % 2 == 0:
            w1, w2, lam = _s5_tables(s5_a_re[j], s5_a_im[j], s5_log_dt[j], s5_b_re[j], s5_b_im[j],
                                     s5_c_re[j], s5_c_im[j])
            mp_slots = jnp.broadcast_to(mp, (nbp, 6, D_MODEL))
            ms_slots = jnp.repeat(ms, 2, axis=0)
            xp3 = xp.reshape(nbp, seqp, D_MODEL)
            xs3 = xs.reshape(s5_slots_s, s5_seq_s, D_MODEL)
            up = _s5_in(xp3, mp_slots, w_in, j, 128)
            us = _s5_in(xs3, ms_slots, w_in, j, 128)
            yp, fin = _s5_core(up, w1, w2, lam, s5_d[j], None, True)
            ys, _ = _s5_core(us, w1, w2, lam, s5_d[j], _latent_h0(state_s5[:, j]), False)
            fin = fin.reshape(N_GROUPS, 2, nbp, 2, STATE_DIM)
            new_s.append(fin.transpose(2, 3, 1, 0, 4))
            xp = _s5_out(yp, xp3, mp_slots, w_glu, w_out, j, lg0, lb0, 128).reshape(nbp * seqp, D_MODEL)
            xs = _s5_out(ys, xs3, ms_slots, w_glu, w_out, j, lg0, lb0, 128).reshape(nbs * seqs, D_MODEL)
        else:
            qp, kp, vp = _qkv(xp, mp, wq, j, q_norm_g[j], k_norm_g[j], nbp, seqp, 256, False)
            qs, ks, vs = _qkv(xs, ms, wq, j, q_norm_g[j], k_norm_g[j], nbs, seqs, 512, True)
            new_k.append(kp.reshape(nbp, seqp, N_KV_HEADS, HEAD_DIM))
            new_v.append(vp.reshape(nbp, seqp, N_KV_HEADS, HEAD_DIM))
            xp = _attention(qp, kp, vp, None, None, j, xp, mp, wo, lg0, lb0, nbp, seqp, 256)
            xs = _attention(qs, ks, vs, cache_k, cache_v, j, xs, ms, wo, lg0, lb0, nbs, seqs, 256)
        xp = _ffn(xp, mp, w_fi, w_fo, layer, lg1, lb1, 1, nbp * seqp, 512)
        xs = _ffn(xs, ms, w_fi, w_fo, layer, lg1, lb1, nbs, seqs, 512)

    y_prompt = xp.reshape(nbp, seqp, D_MODEL)
    y_sample = xs.reshape(nbs, seqs, D_MODEL)
    return (y_prompt, y_sample, jnp.stack(new_k, axis=1), jnp.stack(new_v, axis=1),
            jnp.stack(new_s, axis=1))
```

```python
import functools
import math

import jax
import jax.numpy as jnp
from jax import lax
from jax.experimental import pallas as pl
from jax.experimental.pallas import tpu as pltpu

F32 = jnp.float32
BF16 = jnp.bfloat16

D_MODEL = 1024
DEPTH = 4
N_GROUPS = 64
GROUP_CH = 16
STATE_DIM = 64
HEAD_DIM = 128
N_HEADS = 8
N_KV_HEADS = 2
KV_REP = N_HEADS // N_KV_HEADS
D_Q = N_HEADS * HEAD_DIM
D_KV = N_KV_HEADS * HEAD_DIM
QKV_DIM = D_Q + 2 * D_KV
GRID_W = 64
ROPE_THETA = 10000.0
AXIS_PAIRS = HEAD_DIM // 4
ATTN_SCALE = HEAD_DIM ** -0.5
DEEPNORM_ALPHA = (2.0 * DEPTH) ** 0.25
LN_EPS = 1e-6
RMS_EPS = 1e-6

V7X_VMEM_LIMIT_BYTES = 56 * 1024 * 1024
LANES = 128
SUBLANES = 8
MXU_TILE = 256
GROUPS_PER_BLOCK = LANES // GROUP_CH
N_GROUP_BLOCKS = N_GROUPS // GROUPS_PER_BLOCK
S5_CHUNK = MXU_TILE // GROUP_CH
CHUNK_LANES = S5_CHUNK * GROUP_CH
FOLD_UNROLL = 4
PERM_SEQS = SUBLANES
PERM_STEPS = MXU_TILE // PERM_SEQS
NT_DIMS = (((1,), (1,)), ((), ()))


def _params(n_axes):
    return pltpu.CompilerParams(dimension_semantics=("arbitrary",) * n_axes,
                                vmem_limit_bytes=V7X_VMEM_LIMIT_BYTES)


def _sigmoid(x):
    return 1.0 / (1.0 + jnp.exp(-x))


def _gelu_tanh(x):
    cdf = 0.5 * (1.0 + jnp.tanh(math.sqrt(2.0 / math.pi) * (x + 0.044715 * (x * x * x))))
    return x * cdf


def _layer_norm(r, g, b):
    mu = jnp.mean(r, axis=-1, keepdims=True)
    d = r - mu
    var = jnp.mean(d * d, axis=-1, keepdims=True)
    return d * lax.rsqrt(var + LN_EPS) * g + b


def _dot(a, b):
    return jnp.dot(a, b, preferred_element_type=F32)


def _dot_nt(a, b):
    return lax.dot_general(a, b, NT_DIMS, preferred_element_type=F32)


def _adaln_kernel(c_ref, w_ref, b_ref, o_ref):
    c = c_ref[...]
    s = c * _sigmoid(c)
    o_ref[0] = _dot(s.astype(BF16), w_ref[0].astype(BF16)) + b_ref[0]


def _adaln(cond, w_mod, b_mod):
    tn = 1536
    n = 6 * D_MODEL
    return pl.pallas_call(
        _adaln_kernel,
        grid=(DEPTH, n // tn),
        in_specs=[pl.BlockSpec((8, D_MODEL), lambda l, j: (0, 0)),
                  pl.BlockSpec((1, D_MODEL, tn), lambda l, j: (l, 0, j)),
                  pl.BlockSpec((1, 1, tn), lambda l, j: (l, 0, j))],
        out_specs=pl.BlockSpec((1, 8, tn), lambda l, j: (l, 0, j)),
        out_shape=jax.ShapeDtypeStruct((DEPTH, 8, n), F32),
        compiler_params=_params(2),
        name="adaln",
    )(cond, w_mod, b_mod.reshape(DEPTH, 1, n))


def _s5_prep_kernel(a_ref, ldt_ref, bt_ref, c_ref, w1_ref, w2_ref, lam_ref):
    half = STATE_DIM
    lane1 = lax.broadcasted_iota(jnp.int32, (1, LANES), 1)
    sgn1 = jnp.where(lane1 < half, -1.0, 1.0)
    lane_h = lax.broadcasted_iota(jnp.int32, (GROUP_CH, LANES), 1)
    first_h = lane_h < half
    conj_h = jnp.where(first_h, 1.0, -1.0)
    lane_c = lax.broadcasted_iota(jnp.int32, (GROUP_CH, CHUNK_LANES), 1)
    t = S5_CHUNK

    def cmul(pr, pi, x):
        return pr * x + (pi * sgn1) * pltpu.roll(x, half, 1)

    def pack_states(f, b):
        return (jnp.where(first_h, f, pltpu.roll(b, half, 1)),
                jnp.where(first_h, pltpu.roll(f, half, 1), b))

    def group(g, carry):
        kt, qs, cks, lam_t = [], [], [], []
        for d in range(2):
            a_re = a_ref[0, d, g]
            a_im = a_ref[1, d, g]
            dt = jnp.exp(ldt_ref[d, g])
            mag = jnp.exp(dt * a_re)
            lr = mag * jnp.cos(dt * a_im)
            li = mag * jnp.sin(dt * a_im)
            den = a_re * a_re + a_im * a_im
            nr = lr - 1.0
            k_re = (nr * a_re + li * a_im) / den
            k_im = (li * a_re - nr * a_im) / den
            bb = cmul(k_re, k_im, bt_ref[d, g])
            cc = c_ref[d, g]
            pr = jnp.ones((1, LANES), F32)
            pi = jnp.zeros((1, LANES), F32)
            ck, q = [], []
            for k in range(t + 1):
                ck.append(cmul(pr, pi, cc))
                if k < t:
                    q.append(cmul(pr, pi, bb))
                    pr, pi = pr * lr - pi * li, pr * li + pi * lr
            lam_t.append((pr, pi))
            order = range(t) if d == 0 else range(t - 1, -1, -1)
            rhs = jnp.concatenate([ck[k] for k in order], axis=0).astype(BF16)
            kt.append(_dot_nt((bb * conj_h).astype(BF16), rhs))
            qs.append(q)
            cks.append(ck)
        for j in range(t):
            rows = slice(j * GROUP_CH, (j + 1) * GROUP_CH)
            tf = kt[0] if j == 0 else pltpu.roll(kt[0], GROUP_CH * j, 1)
            tf = jnp.where(lane_c >= GROUP_CH * j, tf, 0.0)
            back = GROUP_CH * (t - 1 - j)
            tb = kt[1] if back == 0 else pltpu.roll(kt[1], CHUNK_LANES - back, 1)
            tb = jnp.where(lane_c < GROUP_CH * (j + 1), tb, 0.0)
            w1_ref[g, rows, 0:CHUNK_LANES] = (tf + tb).astype(BF16)
            s_re, s_im = pack_states(qs[0][t - 1 - j], qs[1][j])
            w1_ref[g, rows, CHUNK_LANES:CHUNK_LANES + LANES] = s_re.astype(BF16)
            w1_ref[g, rows, CHUNK_LANES + LANES:CHUNK_LANES + 2 * LANES] = s_im.astype(BF16)
            c_re, c_im = pack_states(cks[0][j + 1] * conj_h, cks[1][t - j] * conj_h)
            w2_ref[g, rows, 0:LANES] = c_re.astype(BF16)
            w2_ref[g, rows, LANES:2 * LANES] = c_im.astype(BF16)
        fwd1 = lane1 < half
        lam_ref[g, 0:1, :] = jnp.where(fwd1, lam_t[0][0], lam_t[1][0])
        lam_ref[g, 1:2, :] = jnp.where(fwd1, lam_t[0][1], lam_t[1][1])
        return carry

    lax.fori_loop(0, GROUPS_PER_BLOCK, group, 0, unroll=2)


def _s5_tables(a_re, a_im, log_dt, b_re, b_im, c_re, c_im):
    g, p, h = N_GROUPS, STATE_DIM, GROUP_CH
    dup = lambda x: jnp.concatenate([x, x], axis=-1)
    a2 = jnp.stack([dup(a_re), dup(a_im)]).reshape(2, 2, g, 1, 2 * p)
    bt = jnp.concatenate([b_re.transpose(0, 1, 3, 2), b_im.transpose(0, 1, 3, 2)], axis=-1)
    cc = jnp.concatenate([c_re, c_im], axis=-1)
    gb = GROUPS_PER_BLOCK
    return pl.pallas_call(
        _s5_prep_kernel,
        grid=(N_GROUP_BLOCKS,),
        in_specs=[pl.BlockSpec((2, 2, gb, 1, 2 * p), lambda i: (0, 0, i, 0, 0)),
                  pl.BlockSpec((2, gb, 1, 1), lambda i: (0, i, 0, 0)),
                  pl.BlockSpec((2, gb, h, 2 * p), lambda i: (0, i, 0, 0)),
                  pl.BlockSpec((2, gb, h, 2 * p), lambda i: (0, i, 0, 0))],
        out_specs=[pl.BlockSpec((gb, CHUNK_LANES, CHUNK_LANES + 2 * LANES), lambda i: (i, 0, 0)),
                   pl.BlockSpec((gb, CHUNK_LANES, 2 * LANES), lambda i: (i, 0, 0)),
                   pl.BlockSpec((gb, 2, LANES), lambda i: (i, 0, 0))],
        out_shape=[jax.ShapeDtypeStruct((g, CHUNK_LANES, CHUNK_LANES + 2 * LANES), BF16),
                   jax.ShapeDtypeStruct((g, CHUNK_LANES, 2 * LANES), BF16),
                   jax.ShapeDtypeStruct((g, 2, LANES), F32)],
        compiler_params=_params(1),
        name="s5_prep",
    )(a2, log_dt.reshape(2, g, 1, 1), bt, cc)


def _row_perm(to_time_major):
    n = PERM_SEQS * PERM_STEPS
    r = lax.broadcasted_iota(jnp.int32, (n, n), 0)
    c = lax.broadcasted_iota(jnp.int32, (n, n), 1)
    if to_time_major:
        src = jnp.bitwise_and(r, PERM_SEQS - 1) * PERM_STEPS + lax.shift_right_logical(r, 3)
    else:
        src = jnp.bitwise_and(r, PERM_STEPS - 1) * PERM_SEQS + lax.shift_right_logical(r, 5)
    return jnp.where(c == src, 1.0, 0.0).astype(BF16)


def _s5_in_kernel(x_ref, m_ref, w_ref, o_ref, *, tt):
    scale = 1.0 + m_ref[:, 1:2, :]
    shift = m_ref[:, 0:1, :]
    perm = _row_perm(True)
    n = PERM_SEQS * PERM_STEPS
    for k in range(tt // (2 * PERM_STEPS)):
        pieces = []
        for hf in range(2):
            t0 = (2 * k + hf) * PERM_STEPS
            h = x_ref[:, t0:t0 + PERM_STEPS, :] * scale + shift
            hb = h.reshape(n, D_MODEL).astype(BF16)
            pieces.append(_dot(perm, hb).astype(BF16))
        u = _dot(jnp.concatenate(pieces, axis=0), w_ref[...])
        o_ref[2 * k * PERM_STEPS:(2 * k + 2) * PERM_STEPS] = u.reshape(2 * PERM_STEPS, PERM_SEQS, D_MODEL)


def _s5_in(x, mods, w_in, wl, tt):
    slots, seq, _ = x.shape
    return pl.pallas_call(
        functools.partial(_s5_in_kernel, tt=tt),
        grid=(slots // PERM_SEQS, seq // tt),
        in_specs=[pl.BlockSpec((PERM_SEQS, tt, D_MODEL), lambda s, t: (s, t, 0)),
                  pl.BlockSpec((PERM_SEQS, 6, D_MODEL), lambda s, t: (s, 0, 0)),
                  pl.BlockSpec((None, D_MODEL, D_MODEL), lambda s, t: (wl, 0, 0))],
        out_specs=pl.BlockSpec((tt, PERM_SEQS, D_MODEL), lambda s, t: (t, s, 0)),
        out_shape=jax.ShapeDtypeStruct((seq, slots, D_MODEL), F32),
        compiler_params=_params(2),
        name="s5_in",
    )(x, mods, w_in)


def _block_transpose8(v):
    lane = lax.broadcasted_iota(jnp.int32, (SUBLANES, LANES), 1)
    v = list(v)
    for d in (4, 2, 1):
        low = jnp.bitwise_and(lane, GROUP_CH * d) == 0
        nxt = list(v)
        for i in range(8):
            if i & d == 0:
                a, b = v[i], v[i + d]
                nxt[i] = jnp.where(low, a, pltpu.roll(b, GROUP_CH * d, 1))
                nxt[i + d] = jnp.where(low, pltpu.roll(a, LANES - GROUP_CH * d, 1), b)
        v = nxt
    return v


def _s5_core_kernel(*refs, seq, nq, paired, has_fin):
    refs = list(refs)
    u_ref, w1_ref, w2_ref, lam_ref, d_ref = refs[:5]
    pos = 5
    h0_ref = None
    if paired:
        h0_ref = refs[pos]
        pos += 1
    y_ref = refs[pos]
    pos += 1
    fin_ref = refs[pos] if has_fin else None
    z_sc, yt_sc, bs_sc, sp_sc = refs[-4 - int(paired):][:4]
    sp2_sc = refs[-1] if paired else None
    gpb = GROUPS_PER_BLOCK
    nc = seq // S5_CHUNK
    ns = nq * SUBLANES
    half = STATE_DIM

    def fold(c, carry):
        for qi in range(nq):
            sl = slice(qi * SUBLANES, (qi + 1) * SUBLANES)
            r0 = pl.multiple_of(c * ns + qi * SUBLANES, SUBLANES)
            for hf in range(2):
                v = [u_ref[c * S5_CHUNK + hf * 8 + t, sl, :] for t in range(8)]
                w = _block_transpose8(v)
                for g in range(gpb):
                    z_sc[g, pl.ds(r0, SUBLANES), hf * LANES:(hf + 1) * LANES] = w[g]
        return carry

    lax.fori_loop(0, nc, fold, 0, unroll=FOLD_UNROLL // nq)

    for g in range(gpb):
        m1 = _dot(z_sc[g].astype(BF16), w1_ref[g])
        yt_sc[g] = m1[:, 0:CHUNK_LANES]
        bs_sc[g] = m1[:, CHUNK_LANES:]

    lane = lax.broadcasted_iota(jnp.int32, (SUBLANES, LANES), 1)
    fwd = jnp.bitwise_and(lane, half) == 0
    lam = [(jnp.broadcast_to(lam_ref[g, 0:1, :], (SUBLANES, LANES)),
            jnp.broadcast_to(lam_ref[g, 1:2, :], (SUBLANES, LANES))) for g in range(gpb)]

    def run_pass(init, dst):
        def body(i, carry):
            out = []
            for g in range(gpb):
                l_re, l_im = lam[g]
                for qi in range(nq):
                    s_re, s_im = carry[2 * (g * nq + qi)], carry[2 * (g * nq + qi) + 1]
                    ri = pl.ds(pl.multiple_of(i * ns + qi * SUBLANES, SUBLANES), SUBLANES)
                    rr = pl.ds(pl.multiple_of((nc - 1 - i) * ns + qi * SUBLANES, SUBLANES), SUBLANES)
                    dst[g, ri, 0:half] = s_re[:, 0:half]
                    dst[g, ri, LANES:LANES + half] = s_im[:, 0:half]
                    dst[g, rr, half:LANES] = s_re[:, half:LANES]
                    dst[g, rr, LANES + half:2 * LANES] = s_im[:, half:LANES]
                    x_re = jnp.where(fwd, bs_sc[g, ri, 0:LANES], bs_sc[g, rr, 0:LANES])
                    x_im = jnp.where(fwd, bs_sc[g, ri, LANES:2 * LANES], bs_sc[g, rr, LANES:2 * LANES])
                    out.append(l_re * s_re - l_im * s_im + x_re)
                    out.append(l_re * s_im + l_im * s_re + x_im)
            return tuple(out)

        return lax.fori_loop(0, nc, body, init)

    if paired:
        init = []
        for g in range(gpb):
            init += [h0_ref[g, 0], h0_ref[g, 1]]
        mid = run_pass(tuple(init), sp_sc)
        handed = [jnp.where(fwd, pltpu.roll(s, 1, 0), pltpu.roll(s, SUBLANES - 1, 0)) for s in mid]
        fin = run_pass(tuple(handed), sp2_sc)
    else:
        zero = jnp.zeros((SUBLANES, LANES), F32)
        fin = run_pass((zero,) * (2 * gpb * nq), sp_sc)

    if has_fin:
        for g in range(gpb):
            for qi in range(nq):
                sl = slice(qi * SUBLANES, (qi + 1) * SUBLANES)
                fin_ref[g, 0, sl, :] = fin[2 * (g * nq + qi)]
                fin_ref[g, 1, sl, :] = fin[2 * (g * nq + qi) + 1]

    if paired:
        shape = (nc * ns, 2 * LANES)
        row = lax.broadcasted_iota(jnp.int32, shape, 0)
        col = lax.broadcasted_iota(jnp.int32, shape, 1)
        first_pass = (jnp.bitwise_and(row, 1) == 0) == (jnp.bitwise_and(col, half) == 0)
    for g in range(gpb):
        states = sp_sc[g]
        if paired:
            states = jnp.where(first_pass, states, sp2_sc[g])
        yt_sc[g] = yt_sc[g] + _dot_nt(states.astype(BF16), w2_ref[g])

    d = jnp.broadcast_to(d_ref[...], (SUBLANES, LANES))

    def unfold(c, carry):
        for qi in range(nq):
            sl = slice(qi * SUBLANES, (qi + 1) * SUBLANES)
            r0 = pl.multiple_of(c * ns + qi * SUBLANES, SUBLANES)
            for hf in range(2):
                w = [yt_sc[g, pl.ds(r0, SUBLANES), hf * LANES:(hf + 1) * LANES] for g in range(gpb)]
                v = _block_transpose8(w)
                for t in range(8):
                    step = c * S5_CHUNK + hf * 8 + t
                    y_ref[step, sl, :] = v[t] + d * u_ref[step, sl, :]
        return carry

    lax.fori_loop(0, nc, unfold, 0, unroll=FOLD_UNROLL // nq)


def _s5_core(u, w1, w2, lam, d_skip, h0, want_final):
    seq, slots, _ = u.shape
    paired = h0 is not None
    nq = 1 if paired else slots // SUBLANES
    ns = nq * SUBLANES
    gpb = GROUPS_PER_BLOCK
    rows = (seq // S5_CHUNK) * ns
    u_spec = pl.BlockSpec((seq, ns, LANES), lambda s, g: (0, s, g))
    in_specs = [u_spec,
                pl.BlockSpec((gpb, CHUNK_LANES, CHUNK_LANES + 2 * LANES), lambda s, g: (g, 0, 0)),
                pl.BlockSpec((gpb, CHUNK_LANES, 2 * LANES), lambda s, g: (g, 0, 0)),
                pl.BlockSpec((gpb, 2, LANES), lambda s, g: (g, 0, 0)),
                pl.BlockSpec((1, LANES), lambda s, g: (0, g))]
    args = [u, w1, w2, lam, d_skip.reshape(1, D_MODEL)]
    if paired:
        in_specs.append(pl.BlockSpec((gpb, 2, SUBLANES, LANES), lambda s, g: (g, 0, s, 0)))
        args.append(h0)
    out_specs = [u_spec]
    out_shape = [jax.ShapeDtypeStruct(u.shape, F32)]
    if want_final:
        out_specs.append(pl.BlockSpec((gpb, 2, ns, LANES), lambda s, g: (g, 0, s, 0)))
        out_shape.append(jax.ShapeDtypeStruct((N_GROUPS, 2, slots, LANES), F32))
    res = pl.pallas_call(
        functools.partial(_s5_core_kernel, seq=seq, nq=nq, paired=paired, has_fin=want_final),
        grid=(slots // ns, N_GROUP_BLOCKS),
        in_specs=in_specs,
        out_specs=out_specs,
        out_shape=out_shape,
        scratch_shapes=[pltpu.VMEM((gpb, rows, CHUNK_LANES), F32) for _ in range(5 if paired else 4)],
        compiler_params=_params(2),
        name="s5_core",
    )(*args)
    return (res[0], res[1]) if want_final else (res[0], None)


def _s5_out_kernel(y_ref, x_ref, m_ref, wab_ref, wo_ref, lng_ref, lnb_ref, o_ref, *, tt, f):
    gate = m_ref[:, 2:3, :]
    perm = _row_perm(False)
    n = PERM_SEQS * PERM_STEPS
    lng = lng_ref[...].reshape(1, 1, D_MODEL)
    lnb = lnb_ref[...].reshape(1, 1, D_MODEL)
    for k in range(tt // (2 * PERM_STEPS)):
        t0 = 2 * k * PERM_STEPS
        hb = _gelu_tanh(y_ref[t0:t0 + 2 * PERM_STEPS].reshape(2 * n, D_MODEL)).astype(BF16)
        acc = None
        for c0 in range(0, f, MXU_TILE):
            val = _dot(hb, wab_ref[:, c0:c0 + MXU_TILE])
            gte = _dot(hb, wab_ref[:, f + c0:f + c0 + MXU_TILE])
            z = (val * _sigmoid(gte)).astype(BF16)
            zp = jnp.concatenate([_dot(perm, z[0:n]).astype(BF16), _dot(perm, z[n:2 * n]).astype(BF16)],
                                 axis=0)
            part = _dot(zp, wo_ref[c0:c0 + MXU_TILE, :])
            acc = part if acc is None else acc + part
        for hf in range(2):
            ts = slice(t0 + hf * PERM_STEPS, t0 + (hf + 1) * PERM_STEPS)
            mix = acc[hf * n:(hf + 1) * n].reshape(PERM_SEQS, PERM_STEPS, D_MODEL)
            r = DEEPNORM_ALPHA * x_ref[:, ts, :] + gate * mix
            o_ref[:, ts, :] = _layer_norm(r, lng, lnb)


def _s5_out(y, x, mods, w_glu, w_out, wl, ln_g, ln_b, tt):
    slots, seq, _ = x.shape
    f = w_out.shape[1]
    resident = pl.Buffered(1)
    tok = pl.BlockSpec((PERM_SEQS, tt, D_MODEL), lambda s, t: (s, t, 0))
    return pl.pallas_call(
        functools.partial(_s5_out_kernel, tt=tt, f=f),
        grid=(slots // PERM_SEQS, seq // tt),
        in_specs=[pl.BlockSpec((tt, PERM_SEQS, D_MODEL), lambda s, t: (t, s, 0)),
                  tok,
                  pl.BlockSpec((PERM_SEQS, 6, D_MODEL), lambda s, t: (s, 0, 0)),
                  pl.BlockSpec((None, D_MODEL, 2 * f), lambda s, t: (wl, 0, 0), pipeline_mode=resident),
                  pl.BlockSpec((None, f, D_MODEL), lambda s, t: (wl, 0, 0), pipeline_mode=resident),
                  pl.BlockSpec((1, D_MODEL), lambda s, t: (0, 0)),
                  pl.BlockSpec((1, D_MODEL), lambda s, t: (0, 0))],
        out_specs=tok,
        out_shape=jax.ShapeDtypeStruct(x.shape, F32),
        compiler_params=_params(2),
        name="s5_out",
    )(y, x, mods, w_glu, w_out, ln_g.reshape(1, D_MODEL), ln_b.reshape(1, D_MODEL))


def _ffn_kernel(x_ref, m_ref, wi_hbm, wo_hbm, lng_ref, lnb_ref, o_ref,
                h_sc, wab_sc, wo_sc, stage_a, stage_b, stage_o, sem, *, f, wl):
    n_slab = f // MXU_TILE
    m = m_ref[0]
    h = x_ref[...] * (1.0 + m[4:5]) + m[3:4]
    h_sc[...] = h.astype(BF16)

    def slab_copies(k, slot):
        c0 = k * MXU_TILE
        return (pltpu.make_async_copy(wi_hbm.at[wl, :, pl.ds(c0, MXU_TILE)], stage_a.at[slot], sem.at[0, slot]),
                pltpu.make_async_copy(wi_hbm.at[wl, :, pl.ds(f + c0, MXU_TILE)], stage_b.at[slot], sem.at[1, slot]),
                pltpu.make_async_copy(wo_hbm.at[wl, pl.ds(c0, MXU_TILE), :], stage_o.at[slot], sem.at[2, slot]))

    def run(load_weights):
        if load_weights:
            for cp in slab_copies(0, 0):
                cp.start()
        acc = None
        for k in range(n_slab):
            c0 = k * MXU_TILE
            if load_weights:
                slot = k % 2
                if k + 1 < n_slab:
                    for cp in slab_copies(k + 1, 1 - slot):
                        cp.start()
                for cp in slab_copies(k, slot):
                    cp.wait()
                wab_sc[:, c0:c0 + MXU_TILE] = stage_a[slot].astype(BF16)
                wab_sc[:, f + c0:f + c0 + MXU_TILE] = stage_b[slot].astype(BF16)
                wo_sc[c0:c0 + MXU_TILE, :] = stage_o[slot].astype(BF16)
            a = _dot(h_sc[...], wab_sc[:, c0:c0 + MXU_TILE])
            b = _dot(h_sc[...], wab_sc[:, f + c0:f + c0 + MXU_TILE])
            z = (a * _sigmoid(a)) * b
            part = _dot(z.astype(BF16), wo_sc[c0:c0 + MXU_TILE, :])
            acc = part if acc is None else acc + part
        r = DEEPNORM_ALPHA * x_ref[...] + m[5:6] * acc
        o_ref[...] = _layer_norm(r, lng_ref[...], lnb_ref[...])

    first = jnp.logical_and(pl.program_id(0) == 0, pl.program_id(1) == 0)

    @pl.when(first)
    def _():
        run(True)

    @pl.when(jnp.logical_not(first))
    def _():
        run(False)


def _ffn(x, mods, w_in, w_out, wl, ln_g, ln_b, nb, seq, tm):
    f = w_out.shape[1]
    nt = seq // tm
    per_batch = mods.shape[0] > 1
    tok = pl.BlockSpec((tm, D_MODEL), lambda b, t: (b * nt + t, 0))
    return pl.pallas_call(
        functools.partial(_ffn_kernel, f=f, wl=wl),
        grid=(nb, nt),
        in_specs=[tok,
                  pl.BlockSpec((1, 6, D_MODEL), lambda b, t: (b if per_batch else 0, 0, 0)),
                  pl.BlockSpec(memory_space=pl.ANY),
                  pl.BlockSpec(memory_space=pl.ANY),
                  pl.BlockSpec((1, D_MODEL), lambda b, t: (0, 0)),
                  pl.BlockSpec((1, D_MODEL), lambda b, t: (0, 0))],
        out_specs=tok,
        out_shape=jax.ShapeDtypeStruct((nb * seq, D_MODEL), F32),
        scratch_shapes=[pltpu.VMEM((tm, D_MODEL), BF16),
                        pltpu.VMEM((D_MODEL, 2 * f), BF16),
                        pltpu.VMEM((f, D_MODEL), BF16),
                        pltpu.VMEM((2, D_MODEL, MXU_TILE), F32),
                        pltpu.VMEM((2, D_MODEL, MXU_TILE), F32),
                        pltpu.VMEM((2, MXU_TILE, D_MODEL), F32),
                        pltpu.SemaphoreType.DMA((3, 2))],
        compiler_params=_params(2),
        name="ffn",
    )(x, mods, w_in, w_out, ln_g.reshape(1, D_MODEL), ln_b.reshape(1, D_MODEL))


def _qkv_kernel(*refs, rope):
    if rope:
        x_ref, m_ref, w_ref, qg_ref, kg_ref, cos_ref, sin_ref, q_ref, k_ref, v_ref = refs
    else:
        x_ref, m_ref, w_ref, qg_ref, kg_ref, q_ref, k_ref, v_ref = refs
    m = m_ref[0]
    h = x_ref[...] * (1.0 + m[1:2]) + m[0:1]
    qkv = _dot(h.astype(BF16), w_ref[...])
    if rope:
        cos = cos_ref[...]
        sin = sin_ref[...]
        lane = lax.broadcasted_iota(jnp.int32, cos.shape, 1)
        first = jnp.bitwise_and(lane, AXIS_PAIRS) == 0
    for hd in range(N_HEADS + N_KV_HEADS):
        xh = qkv[:, hd * HEAD_DIM:(hd + 1) * HEAD_DIM]
        gain = qg_ref[...] if hd < N_HEADS else kg_ref[...]
        n = xh * lax.rsqrt(jnp.mean(xh * xh, axis=-1, keepdims=True) + RMS_EPS) * gain
        if rope:
            up = pltpu.roll(n, HEAD_DIM - AXIS_PAIRS, 1)
            down = pltpu.roll(n, AXIS_PAIRS, 1)
            n = n * cos + jnp.where(first, up, down) * sin
        if hd < N_HEADS:
            q_ref[:, hd * HEAD_DIM:(hd + 1) * HEAD_DIM] = n.astype(BF16)
        else:
            k_ref[:, (hd - N_HEADS) * HEAD_DIM:(hd - N_HEADS + 1) * HEAD_DIM] = n
    v_ref[...] = qkv[:, D_Q + D_KV:]


def _rope_tables(seq):
    pos = jnp.arange(seq, dtype=jnp.int32)
    row = (pos // GRID_W).astype(F32)
    col = (pos % GRID_W).astype(F32)
    inv = ROPE_THETA ** (-jnp.arange(AXIS_PAIRS, dtype=F32) / AXIS_PAIRS)
    ar = row[:, None] * inv
    ac = col[:, None] * inv
    cos = jnp.concatenate([jnp.cos(ar), jnp.cos(ar), jnp.cos(ac), jnp.cos(ac)], axis=-1)
    sin = jnp.concatenate([-jnp.sin(ar), jnp.sin(ar), -jnp.sin(ac), jnp.sin(ac)], axis=-1)
    return cos, sin


def _qkv(x, mods, w_qkv, wl, q_gain, k_gain, nb, seq, tm, rope):
    nt = seq // tm
    per_batch = mods.shape[0] > 1
    in_specs = [pl.BlockSpec((tm, D_MODEL), lambda b, t: (b * nt + t, 0)),
                pl.BlockSpec((1, 6, D_MODEL), lambda b, t: (b if per_batch else 0, 0, 0)),
                pl.BlockSpec((None, D_MODEL, QKV_DIM), lambda b, t: (wl, 0, 0)),
                pl.BlockSpec((1, HEAD_DIM), lambda b, t: (0, 0)),
                pl.BlockSpec((1, HEAD_DIM), lambda b, t: (0, 0))]
    args = [x, mods, w_qkv, q_gain.reshape(1, HEAD_DIM), k_gain.reshape(1, HEAD_DIM)]
    if rope:
        cos, sin = _rope_tables(seq)
        in_specs += [pl.BlockSpec((tm, HEAD_DIM), lambda b, t: (t, 0))] * 2
        args += [cos, sin]
    n_tok = nb * seq
    return pl.pallas_call(
        functools.partial(_qkv_kernel, rope=rope),
        grid=(nb, nt),
        in_specs=in_specs,
        out_specs=[pl.BlockSpec((tm, D_Q), lambda b, t: (b * nt + t, 0)),
                   pl.BlockSpec((tm, D_KV), lambda b, t: (b * nt + t, 0)),
                   pl.BlockSpec((tm, D_KV), lambda b, t: (b * nt + t, 0))],
        out_shape=[jax.ShapeDtypeStruct((n_tok, D_Q), BF16),
                   jax.ShapeDtypeStruct((n_tok, D_KV), F32),
                   jax.ShapeDtypeStruct((n_tok, D_KV), F32)],
        compiler_params=_params(2),
        name="qkv_rope" if rope else "qkv",
    )(*args)


def _attn_kernel(*refs, has_cache):
    if has_cache:
        (q_ref, k_ref, v_ref, ck_ref, cv_ref, x_ref, m_ref, wo_ref, lng_ref, lnb_ref,
         o_ref, oh_sc) = refs
    else:
        q_ref, k_ref, v_ref, x_ref, m_ref, wo_ref, lng_ref, lnb_ref, o_ref, oh_sc = refs
    for g in range(N_KV_HEADS):
        sl = slice(g * HEAD_DIM, (g + 1) * HEAD_DIM)
        kg = k_ref[:, sl].astype(BF16)
        vg = v_ref[:, sl].astype(BF16)
        if has_cache:
            ckg = ck_ref[:, sl].astype(BF16)
            cvg = cv_ref[:, sl].astype(BF16)
        for r in range(KV_REP):
            hsl = slice((g * KV_REP + r) * HEAD_DIM, (g * KV_REP + r + 1) * HEAD_DIM)
            qh = q_ref[:, hsl]
            s1 = _dot_nt(qh, kg) * ATTN_SCALE
            mx = jnp.max(s1, axis=-1, keepdims=True)
            if has_cache:
                s2 = _dot_nt(qh, ckg) * ATTN_SCALE
                mx = jnp.maximum(mx, jnp.max(s2, axis=-1, keepdims=True))
            p1 = jnp.exp(s1 - mx)
            den = jnp.sum(p1, axis=-1, keepdims=True)
            o = _dot(p1.astype(BF16), vg)
            if has_cache:
                p2 = jnp.exp(s2 - mx)
                den = den + jnp.sum(p2, axis=-1, keepdims=True)
                o = o + _dot(p2.astype(BF16), cvg)
            oh_sc[:, hsl] = (o / den).astype(BF16)
    mix = _dot(oh_sc[...], wo_ref[...])
    m = m_ref[0]
    r = DEEPNORM_ALPHA * x_ref[...] + m[2:3] * mix
    o_ref[...] = _layer_norm(r, lng_ref[...], lnb_ref[...])


def _attention(q, k, v, cache_k, cache_v, layer_j, x, mods, w_o, ln_g, ln_b, nb, seq, tq):
    nt = seq // tq
    per_batch = mods.shape[0] > 1
    has_cache = cache_k is not None
    tok = pl.BlockSpec((tq, D_MODEL), lambda b, t: (b * nt + t, 0))
    kv = pl.BlockSpec((seq, D_KV), lambda b, t: (b, 0))
    in_specs, args = [tok, kv, kv], [q, k, v]
    if has_cache:
        past = cache_k.shape[2]
        cspec = pl.BlockSpec((None, None, past, D_KV), lambda b, t: (b, layer_j, 0, 0))
        in_specs += [cspec, cspec]
        args += [cache_k.reshape(cache_k.shape[0], cache_k.shape[1], past, D_KV),
                 cache_v.reshape(cache_v.shape[0], cache_v.shape[1], past, D_KV)]
    in_specs += [tok,
                 pl.BlockSpec((1, 6, D_MODEL), lambda b, t: (b if per_batch else 0, 0, 0)),
                 pl.BlockSpec((None, D_Q, D_MODEL), lambda b, t: (layer_j, 0, 0)),
                 pl.BlockSpec((1, D_MODEL), lambda b, t: (0, 0)),
                 pl.BlockSpec((1, D_MODEL), lambda b, t: (0, 0))]
    args += [x, mods, w_o, ln_g.reshape(1, D_MODEL), ln_b.reshape(1, D_MODEL)]
    return pl.pallas_call(
        functools.partial(_attn_kernel, has_cache=has_cache),
        grid=(nb, nt),
        in_specs=in_specs,
        out_specs=tok,
        out_shape=jax.ShapeDtypeStruct((nb * seq, D_MODEL), F32),
        scratch_shapes=[pltpu.VMEM((tq, D_Q), BF16)],
        compiler_params=_params(2),
        name="attn_cache" if has_cache else "attn",
    )(*args)


def _latent_h0(st):
    f = st[:, 0].transpose(2, 1, 0, 3)
    b = st[:, 1].transpose(2, 1, 0, 3)
    z = jnp.zeros_like(f)
    even = jnp.concatenate([f, z], axis=-1)
    odd = jnp.concatenate([z, b], axis=-1)
    h0 = jnp.stack([even, odd], axis=3)
    return h0.reshape(N_GROUPS, 2, 2 * st.shape[0], 2 * STATE_DIM)


def kernel(x_prompt, x_sample, c, cache_k, cache_v, state_s5, c_ctx, w_mod, b_mod, ln_g, ln_b, w_s5_in, s5_a_re, s5_a_im, s5_log_dt, s5_b_re, s5_b_im, s5_c_re, s5_c_im, s5_d, w_s5_glu, w_s5_out, w_qkv, q_norm_g, k_norm_g, w_o, w_ffn_in, w_ffn_out):
    nbp, seqp, _ = x_prompt.shape
    nbs, seqs, _ = x_sample.shape
    xp = x_prompt.reshape(nbp * seqp, D_MODEL)
    xs = x_sample.reshape(nbs * seqs, D_MODEL)
    s5_slots_s, s5_seq_s = 2 * nbs, seqs // 2

    cond = jnp.concatenate([c_ctx[None, :], c, jnp.zeros((8 - 1 - nbs, D_MODEL), F32)], axis=0)
    mods = _adaln(cond, w_mod, b_mod)
    mods_p = mods[:, 0:1].reshape(DEPTH, 1, 6, D_MODEL)
    mods_s = mods[:, 1:1 + nbs].reshape(DEPTH, nbs, 6, D_MODEL)

    w_in, w_glu, w_out = w_s5_in.astype(BF16), w_s5_glu.astype(BF16), w_s5_out.astype(BF16)
    wq, wo = w_qkv.astype(BF16), w_o.astype(BF16)

    new_k, new_v, new_s = [], [], []
    for layer in range(DEPTH):
        j = layer // 2
        mp, ms = mods_p[layer], mods_s[layer]
        lg0, lb0, lg1, lb1 = ln_g[layer, 0], ln_b[layer, 0], ln_g[layer, 1], ln_b[layer, 1]
        if layer % 2 == 0:
            w1, w2, lam = _s5_tables(s5_a_re[j], s5_a_im[j], s5_log_dt[j], s5_b_re[j], s5_b_im[j],
                                     s5_c_re[j], s5_c_im[j])
            mp_slots = jnp.broadcast_to(mp, (nbp, 6, D_MODEL))
            ms_slots = jnp.repeat(ms, 2, axis=0)
            xp3 = xp.reshape(nbp, seqp, D_MODEL)
            xs3 = xs.reshape(s5_slots_s, s5_seq_s, D_MODEL)
            up = _s5_in(xp3, mp_slots, w_in, j, 128)
            us = _s5_in(xs3, ms_slots, w_in, j, 128)
            yp, fin = _s5_core(up, w1, w2, lam, s5_d[j], None, True)
            ys, _ = _s5_core(us, w1, w2, lam, s5_d[j], _latent_h0(state_s5[:, j]), False)
            fin = fin.reshape(N_GROUPS, 2, nbp, 2, STATE_DIM)
            new_s.append(fin.transpose(2, 3, 1, 0, 4))
            xp = _s5_out(yp, xp3, mp_slots, w_glu, w_out, j, lg0, lb0, 128).reshape(nbp * seqp, D_MODEL)
            xs = _s5_out(ys, xs3, ms_slots, w_glu, w_out, j, lg0, lb0, 128).reshape(nbs * seqs, D_MODEL)
        else:
            qp, kp, vp = _qkv(xp, mp, wq, j, q_norm_g[j], k_norm_g[j], nbp, seqp, 256, False)
            qs, ks, vs = _qkv(xs, ms, wq, j, q_norm_g[j], k_norm_g[j], nbs, seqs, 512, True)
            new_k.append(kp.reshape(nbp, seqp, N_KV_HEADS, HEAD_DIM))
            new_v.append(vp.reshape(nbp, seqp, N_KV_HEADS, HEAD_DIM))
            xp = _attention(qp, kp, vp, None, None, j, xp, mp, wo, lg0, lb0, nbp, seqp, 256)
            xs = _attention(qs, ks, vs, cache_k, cache_v, j, xs, ms, wo, lg0, lb0, nbs, seqs, 256)
        xp = _ffn(xp, mp, w_ffn_in, w_ffn_out, layer, lg1, lb1, 1, nbp * seqp, 512)
        xs = _ffn(xs, ms, w_ffn_in, w_ffn_out, layer, lg1, lb1, nbs, seqs, 512)

    y_prompt = xp.reshape(nbp, seqp, D_MODEL)
    y_sample = xs.reshape(nbs, seqs, D_MODEL)
    return (y_prompt, y_sample, jnp.stack(new_k, axis=1), jnp.stack(new_v, axis=1),
            jnp.stack(new_s, axis=1))
```

```python
import functools
import math

import jax
import jax.numpy as jnp
from jax import lax
from jax.experimental import pallas as pl
from jax.experimental.pallas import tpu as pltpu

F32 = jnp.float32
BF16 = jnp.bfloat16

D_MODEL = 1024
DEPTH = 4
N_GROUPS = 64
GROUP_CH = 16
STATE_DIM = 64
HEAD_DIM = 128
N_HEADS = 8
N_KV_HEADS = 2
KV_REP = N_HEADS // N_KV_HEADS
D_Q = N_HEADS * HEAD_DIM
D_KV = N_KV_HEADS * HEAD_DIM
QKV_DIM = D_Q + 2 * D_KV
GRID_W = 64
ROPE_THETA = 10000.0
AXIS_PAIRS = HEAD_DIM // 4
ATTN_SCALE = HEAD_DIM ** -0.5
DEEPNORM_ALPHA = (2.0 * DEPTH) ** 0.25
LN_EPS = 1e-6
RMS_EPS = 1e-6

V7X_VMEM_LIMIT_BYTES = 56 * 1024 * 1024
LANES = 128
SUBLANES = 8
MXU_TILE = 256
FFN_STAGE_SLOTS = 4
GROUPS_PER_BLOCK = LANES // GROUP_CH
N_GROUP_BLOCKS = N_GROUPS // GROUPS_PER_BLOCK
S5_CHUNK = MXU_TILE // GROUP_CH
CHUNK_LANES = S5_CHUNK * GROUP_CH
FOLD_UNROLL = 4
PERM_SEQS = SUBLANES
PERM_STEPS = MXU_TILE // PERM_SEQS
NT_DIMS = (((1,), (1,)), ((), ()))


def _params(n_axes):
    return pltpu.CompilerParams(dimension_semantics=("arbitrary",) * n_axes,
                                vmem_limit_bytes=V7X_VMEM_LIMIT_BYTES)


def _sigmoid(x):
    return 1.0 / (1.0 + jnp.exp(-x))


def _gelu_tanh(x):
    cdf = 0.5 * (1.0 + jnp.tanh(math.sqrt(2.0 / math.pi) * (x + 0.044715 * (x * x * x))))
    return x * cdf


def _layer_norm(r, g, b):
    mu = jnp.mean(r, axis=-1, keepdims=True)
    d = r - mu
    var = jnp.mean(d * d, axis=-1, keepdims=True)
    return d * lax.rsqrt(var + LN_EPS) * g + b


def _dot(a, b):
    return jnp.dot(a, b, preferred_element_type=F32)


def _dot_nt(a, b):
    return lax.dot_general(a, b, NT_DIMS, preferred_element_type=F32)


def _adaln_kernel(c_ref, w_ref, b_ref, o_ref):
    c = c_ref[...]
    s = c * _sigmoid(c)
    o_ref[0] = _dot(s.astype(BF16), w_ref[0].astype(BF16)) + b_ref[0]


def _adaln(cond, w_mod, b_mod):
    tn = 1536
    n = 6 * D_MODEL
    return pl.pallas_call(
        _adaln_kernel,
        grid=(DEPTH, n // tn),
        in_specs=[pl.BlockSpec((8, D_MODEL), lambda l, j: (0, 0)),
                  pl.BlockSpec((1, D_MODEL, tn), lambda l, j: (l, 0, j)),
                  pl.BlockSpec((1, 1, tn), lambda l, j: (l, 0, j))],
        out_specs=pl.BlockSpec((1, 8, tn), lambda l, j: (l, 0, j)),
        out_shape=jax.ShapeDtypeStruct((DEPTH, 8, n), F32),
        compiler_params=_params(2),
        name="adaln",
    )(cond, w_mod, b_mod.reshape(DEPTH, 1, n))


def _s5_prep_kernel(a_ref, ldt_ref, bt_ref, c_ref, w1_ref, w2_ref, lam_ref):
    half = STATE_DIM
    lane1 = lax.broadcasted_iota(jnp.int32, (1, LANES), 1)
    sgn1 = jnp.where(lane1 < half, -1.0, 1.0)
    lane_h = lax.broadcasted_iota(jnp.int32, (GROUP_CH, LANES), 1)
    first_h = lane_h < half
    conj_h = jnp.where(first_h, 1.0, -1.0)
    lane_c = lax.broadcasted_iota(jnp.int32, (GROUP_CH, CHUNK_LANES), 1)
    t = S5_CHUNK

    def cmul(pr, pi, x):
        return pr * x + (pi * sgn1) * pltpu.roll(x, half, 1)

    def pack_states(f, b):
        return (jnp.where(first_h, f, pltpu.roll(b, half, 1)),
                jnp.where(first_h, pltpu.roll(f, half, 1), b))

    def group(g, carry):
        kt, qs, cks, lam_t = [], [], [], []
        for d in range(2):
            a_re = a_ref[0, d, g]
            a_im = a_ref[1, d, g]
            dt = jnp.exp(ldt_ref[d, g])
            mag = jnp.exp(dt * a_re)
            lr = mag * jnp.cos(dt * a_im)
            li = mag * jnp.sin(dt * a_im)
            den = a_re * a_re + a_im * a_im
            nr = lr - 1.0
            k_re = (nr * a_re + li * a_im) / den
            k_im = (li * a_re - nr * a_im) / den
            bb = cmul(k_re, k_im, bt_ref[d, g])
            cc = c_ref[d, g]
            pr = jnp.ones((1, LANES), F32)
            pi = jnp.zeros((1, LANES), F32)
            ck, q = [], []
            for k in range(t + 1):
                ck.append(cmul(pr, pi, cc))
                if k < t:
                    q.append(cmul(pr, pi, bb))
                    pr, pi = pr * lr - pi * li, pr * li + pi * lr
            lam_t.append((pr, pi))
            order = range(t) if d == 0 else range(t - 1, -1, -1)
            rhs = jnp.concatenate([ck[k] for k in order], axis=0).astype(BF16)
            kt.append(_dot_nt((bb * conj_h).astype(BF16), rhs))
            qs.append(q)
            cks.append(ck)
        for j in range(t):
            rows = slice(j * GROUP_CH, (j + 1) * GROUP_CH)
            tf = kt[0] if j == 0 else pltpu.roll(kt[0], GROUP_CH * j, 1)
            tf = jnp.where(lane_c >= GROUP_CH * j, tf, 0.0)
            back = GROUP_CH * (t - 1 - j)
            tb = kt[1] if back == 0 else pltpu.roll(kt[1], CHUNK_LANES - back, 1)
            tb = jnp.where(lane_c < GROUP_CH * (j + 1), tb, 0.0)
            w1_ref[g, rows, 0:CHUNK_LANES] = (tf + tb).astype(BF16)
            s_re, s_im = pack_states(qs[0][t - 1 - j], qs[1][j])
            w1_ref[g, rows, CHUNK_LANES:CHUNK_LANES + LANES] = s_re.astype(BF16)
            w1_ref[g, rows, CHUNK_LANES + LANES:CHUNK_LANES + 2 * LANES] = s_im.astype(BF16)
            c_re, c_im = pack_states(cks[0][j + 1] * conj_h, cks[1][t - j] * conj_h)
            w2_ref[g, rows, 0:LANES] = c_re.astype(BF16)
            w2_ref[g, rows, LANES:2 * LANES] = c_im.astype(BF16)
        fwd1 = lane1 < half
        lam_ref[g, 0:1, :] = jnp.where(fwd1, lam_t[0][0], lam_t[1][0])
        lam_ref[g, 1:2, :] = jnp.where(fwd1, lam_t[0][1], lam_t[1][1])
        return carry

    lax.fori_loop(0, GROUPS_PER_BLOCK, group, 0, unroll=2)


def _s5_tables(a_re, a_im, log_dt, b_re, b_im, c_re, c_im):
    g, p, h = N_GROUPS, STATE_DIM, GROUP_CH
    dup = lambda x: jnp.concatenate([x, x], axis=-1)
    a2 = jnp.stack([dup(a_re), dup(a_im)]).reshape(2, 2, g, 1, 2 * p)
    bt = jnp.concatenate([b_re.transpose(0, 1, 3, 2), b_im.transpose(0, 1, 3, 2)], axis=-1)
    cc = jnp.concatenate([c_re, c_im], axis=-1)
    gb = GROUPS_PER_BLOCK
    return pl.pallas_call(
        _s5_prep_kernel,
        grid=(N_GROUP_BLOCKS,),
        in_specs=[pl.BlockSpec((2, 2, gb, 1, 2 * p), lambda i: (0, 0, i, 0, 0)),
                  pl.BlockSpec((2, gb, 1, 1), lambda i: (0, i, 0, 0)),
                  pl.BlockSpec((2, gb, h, 2 * p), lambda i: (0, i, 0, 0)),
                  pl.BlockSpec((2, gb, h, 2 * p), lambda i: (0, i, 0, 0))],
        out_specs=[pl.BlockSpec((gb, CHUNK_LANES, CHUNK_LANES + 2 * LANES), lambda i: (i, 0, 0)),
                   pl.BlockSpec((gb, CHUNK_LANES, 2 * LANES), lambda i: (i, 0, 0)),
                   pl.BlockSpec((gb, 2, LANES), lambda i: (i, 0, 0))],
        out_shape=[jax.ShapeDtypeStruct((g, CHUNK_LANES, CHUNK_LANES + 2 * LANES), BF16),
                   jax.ShapeDtypeStruct((g, CHUNK_LANES, 2 * LANES), BF16),
                   jax.ShapeDtypeStruct((g, 2, LANES), F32)],
        compiler_params=_params(1),
        name="s5_prep",
    )(a2, log_dt.reshape(2, g, 1, 1), bt, cc)


def _row_perm(to_time_major):
    n = PERM_SEQS * PERM_STEPS
    r = lax.broadcasted_iota(jnp.int32, (n, n), 0)
    c = lax.broadcasted_iota(jnp.int32, (n, n), 1)
    if to_time_major:
        src = jnp.bitwise_and(r, PERM_SEQS - 1) * PERM_STEPS + lax.shift_right_logical(r, 3)
    else:
        src = jnp.bitwise_and(r, PERM_STEPS - 1) * PERM_SEQS + lax.shift_right_logical(r, 5)
    return jnp.where(c == src, 1.0, 0.0).astype(BF16)


def _s5_in_kernel(x_ref, m_ref, w_ref, o_ref, *, tt):
    scale = 1.0 + m_ref[:, 1:2, :]
    shift = m_ref[:, 0:1, :]
    perm = _row_perm(True)
    n = PERM_SEQS * PERM_STEPS
    for k in range(tt // (2 * PERM_STEPS)):
        pieces = []
        for hf in range(2):
            t0 = (2 * k + hf) * PERM_STEPS
            h = x_ref[:, t0:t0 + PERM_STEPS, :] * scale + shift
            hb = h.reshape(n, D_MODEL).astype(BF16)
            pieces.append(_dot(perm, hb).astype(BF16))
        u = _dot(jnp.concatenate(pieces, axis=0), w_ref[...])
        o_ref[2 * k * PERM_STEPS:(2 * k + 2) * PERM_STEPS] = u.reshape(2 * PERM_STEPS, PERM_SEQS, D_MODEL)


def _s5_in(x, mods, w_in, wl, tt):
    slots, seq, _ = x.shape
    return pl.pallas_call(
        functools.partial(_s5_in_kernel, tt=tt),
        grid=(slots // PERM_SEQS, seq // tt),
        in_specs=[pl.BlockSpec((PERM_SEQS, tt, D_MODEL), lambda s, t: (s, t, 0)),
                  pl.BlockSpec((PERM_SEQS, 6, D_MODEL), lambda s, t: (s, 0, 0)),
                  pl.BlockSpec((None, D_MODEL, D_MODEL), lambda s, t: (wl, 0, 0))],
        out_specs=pl.BlockSpec((tt, PERM_SEQS, D_MODEL), lambda s, t: (t, s, 0)),
        out_shape=jax.ShapeDtypeStruct((seq, slots, D_MODEL), F32),
        compiler_params=_params(2),
        name="s5_in",
    )(x, mods, w_in)


def _block_transpose8(v):
    lane = lax.broadcasted_iota(jnp.int32, (SUBLANES, LANES), 1)
    v = list(v)
    for d in (4, 2, 1):
        low = jnp.bitwise_and(lane, GROUP_CH * d) == 0
        nxt = list(v)
        for i in range(8):
            if i & d == 0:
                a, b = v[i], v[i + d]
                nxt[i] = jnp.where(low, a, pltpu.roll(b, GROUP_CH * d, 1))
                nxt[i + d] = jnp.where(low, pltpu.roll(a, LANES - GROUP_CH * d, 1), b)
        v = nxt
    return v


def _s5_core_kernel(*refs, seq, nq, paired, has_fin):
    refs = list(refs)
    u_ref, w1_ref, w2_ref, lam_ref, d_ref = refs[:5]
    pos = 5
    h0_ref = None
    if paired:
        h0_ref = refs[pos]
        pos += 1
    y_ref = refs[pos]
    pos += 1
    fin_ref = refs[pos] if has_fin else None
    z_sc, yt_sc, bs_sc, sp_sc = refs[-4 - int(paired):][:4]
    sp2_sc = refs[-1] if paired else None
    gpb = GROUPS_PER_BLOCK
    nc = seq // S5_CHUNK
    ns = nq * SUBLANES
    half = STATE_DIM

    def fold(c, carry):
        for qi in range(nq):
            sl = slice(qi * SUBLANES, (qi + 1) * SUBLANES)
            r0 = pl.multiple_of(c * ns + qi * SUBLANES, SUBLANES)
            for hf in range(2):
                v = [u_ref[c * S5_CHUNK + hf * 8 + t, sl, :] for t in range(8)]
                w = _block_transpose8(v)
                for g in range(gpb):
                    z_sc[g, pl.ds(r0, SUBLANES), hf * LANES:(hf + 1) * LANES] = w[g]
        return carry

    lax.fori_loop(0, nc, fold, 0, unroll=FOLD_UNROLL // nq)

    for g in range(gpb):
        m1 = _dot(z_sc[g].astype(BF16), w1_ref[g])
        yt_sc[g] = m1[:, 0:CHUNK_LANES]
        bs_sc[g] = m1[:, CHUNK_LANES:]

    lane = lax.broadcasted_iota(jnp.int32, (SUBLANES, LANES), 1)
    fwd = jnp.bitwise_and(lane, half) == 0
    lam = [(jnp.broadcast_to(lam_ref[g, 0:1, :], (SUBLANES, LANES)),
            jnp.broadcast_to(lam_ref[g, 1:2, :], (SUBLANES, LANES))) for g in range(gpb)]

    def run_pass(init, dst):
        def body(i, carry):
            out = []
            for g in range(gpb):
                l_re, l_im = lam[g]
                for qi in range(nq):
                    s_re, s_im = carry[2 * (g * nq + qi)], carry[2 * (g * nq + qi) + 1]
                    ri = pl.ds(pl.multiple_of(i * ns + qi * SUBLANES, SUBLANES), SUBLANES)
                    rr = pl.ds(pl.multiple_of((nc - 1 - i) * ns + qi * SUBLANES, SUBLANES), SUBLANES)
                    dst[g, ri, 0:half] = s_re[:, 0:half]
                    dst[g, ri, LANES:LANES + half] = s_im[:, 0:half]
                    dst[g, rr, half:LANES] = s_re[:, half:LANES]
                    dst[g, rr, LANES + half:2 * LANES] = s_im[:, half:LANES]
                    x_re = jnp.where(fwd, bs_sc[g, ri, 0:LANES], bs_sc[g, rr, 0:LANES])
                    x_im = jnp.where(fwd, bs_sc[g, ri, LANES:2 * LANES], bs_sc[g, rr, LANES:2 * LANES])
                    out.append(l_re * s_re - l_im * s_im + x_re)
                    out.append(l_re * s_im + l_im * s_re + x_im)
            return tuple(out)

        return lax.fori_loop(0, nc, body, init)

    if paired:
        init = []
        for g in range(gpb):
            init += [h0_ref[g, 0], h0_ref[g, 1]]
        mid = run_pass(tuple(init), sp_sc)
        handed = [jnp.where(fwd, pltpu.roll(s, 1, 0), pltpu.roll(s, SUBLANES - 1, 0)) for s in mid]
        fin = run_pass(tuple(handed), sp2_sc)
    else:
        zero = jnp.zeros((SUBLANES, LANES), F32)
        fin = run_pass((zero,) * (2 * gpb * nq), sp_sc)

    if has_fin:
        for g in range(gpb):
            for qi in range(nq):
                sl = slice(qi * SUBLANES, (qi + 1) * SUBLANES)
                fin_ref[g, 0, sl, :] = fin[2 * (g * nq + qi)]
                fin_ref[g, 1, sl, :] = fin[2 * (g * nq + qi) + 1]

    if paired:
        shape = (nc * ns, 2 * LANES)
        row = lax.broadcasted_iota(jnp.int32, shape, 0)
        col = lax.broadcasted_iota(jnp.int32, shape, 1)
        first_pass = (jnp.bitwise_and(row, 1) == 0) == (jnp.bitwise_and(col, half) == 0)
    for g in range(gpb):
        states = sp_sc[g]
        if paired:
            states = jnp.where(first_pass, states, sp2_sc[g])
        yt_sc[g] = yt_sc[g] + _dot_nt(states.astype(BF16), w2_ref[g])

    d = jnp.broadcast_to(d_ref[...], (SUBLANES, LANES))

    def unfold(c, carry):
        for qi in range(nq):
            sl = slice(qi * SUBLANES, (qi + 1) * SUBLANES)
            r0 = pl.multiple_of(c * ns + qi * SUBLANES, SUBLANES)
            for hf in range(2):
                w = [yt_sc[g, pl.ds(r0, SUBLANES), hf * LANES:(hf + 1) * LANES] for g in range(gpb)]
                v = _block_transpose8(w)
                for t in range(8):
                    step = c * S5_CHUNK + hf * 8 + t
                    y_ref[step, sl, :] = v[t] + d * u_ref[step, sl, :]
        return carry

    lax.fori_loop(0, nc, unfold, 0, unroll=FOLD_UNROLL // nq)


def _s5_core(u, w1, w2, lam, d_skip, h0, want_final):
    seq, slots, _ = u.shape
    paired = h0 is not None
    nq = 1 if paired else slots // SUBLANES
    ns = nq * SUBLANES
    gpb = GROUPS_PER_BLOCK
    rows = (seq // S5_CHUNK) * ns
    u_spec = pl.BlockSpec((seq, ns, LANES), lambda s, g: (0, s, g))
    in_specs = [u_spec,
                pl.BlockSpec((gpb, CHUNK_LANES, CHUNK_LANES + 2 * LANES), lambda s, g: (g, 0, 0)),
                pl.BlockSpec((gpb, CHUNK_LANES, 2 * LANES), lambda s, g: (g, 0, 0)),
                pl.BlockSpec((gpb, 2, LANES), lambda s, g: (g, 0, 0)),
                pl.BlockSpec((1, LANES), lambda s, g: (0, g))]
    args = [u, w1, w2, lam, d_skip.reshape(1, D_MODEL)]
    if paired:
        in_specs.append(pl.BlockSpec((gpb, 2, SUBLANES, LANES), lambda s, g: (g, 0, s, 0)))
        args.append(h0)
    out_specs = [u_spec]
    out_shape = [jax.ShapeDtypeStruct(u.shape, F32)]
    if want_final:
        out_specs.append(pl.BlockSpec((gpb, 2, ns, LANES), lambda s, g: (g, 0, s, 0)))
        out_shape.append(jax.ShapeDtypeStruct((N_GROUPS, 2, slots, LANES), F32))
    res = pl.pallas_call(
        functools.partial(_s5_core_kernel, seq=seq, nq=nq, paired=paired, has_fin=want_final),
        grid=(slots // ns, N_GROUP_BLOCKS),
        in_specs=in_specs,
        out_specs=out_specs,
        out_shape=out_shape,
        scratch_shapes=[pltpu.VMEM((gpb, rows, CHUNK_LANES), F32) for _ in range(5 if paired else 4)],
        compiler_params=_params(2),
        name="s5_core",
    )(*args)
    return (res[0], res[1]) if want_final else (res[0], None)


def _s5_out_kernel(y_ref, x_ref, m_ref, wab_ref, wo_ref, lng_ref, lnb_ref, o_ref, *, tt, f):
    gate = m_ref[:, 2:3, :]
    perm = _row_perm(False)
    n = PERM_SEQS * PERM_STEPS
    lng = lng_ref[...].reshape(1, 1, D_MODEL)
    lnb = lnb_ref[...].reshape(1, 1, D_MODEL)
    for k in range(tt // (2 * PERM_STEPS)):
        t0 = 2 * k * PERM_STEPS
        hb = _gelu_tanh(y_ref[t0:t0 + 2 * PERM_STEPS].reshape(2 * n, D_MODEL)).astype(BF16)
        acc = None
        for c0 in range(0, f, MXU_TILE):
            val = _dot(hb, wab_ref[:, c0:c0 + MXU_TILE])
            gte = _dot(hb, wab_ref[:, f + c0:f + c0 + MXU_TILE])
            z = (val * _sigmoid(gte)).astype(BF16)
            zp = jnp.concatenate([_dot(perm, z[0:n]).astype(BF16), _dot(perm, z[n:2 * n]).astype(BF16)],
                                 axis=0)
            part = _dot(zp, wo_ref[c0:c0 + MXU_TILE, :])
            acc = part if acc is None else acc + part
        for hf in range(2):
            ts = slice(t0 + hf * PERM_STEPS, t0 + (hf + 1) * PERM_STEPS)
            mix = acc[hf * n:(hf + 1) * n].reshape(PERM_SEQS, PERM_STEPS, D_MODEL)
            r = DEEPNORM_ALPHA * x_ref[:, ts, :] + gate * mix
            o_ref[:, ts, :] = _layer_norm(r, lng, lnb)


def _s5_out(y, x, mods, w_glu, w_out, wl, ln_g, ln_b, tt):
    slots, seq, _ = x.shape
    f = w_out.shape[1]
    resident = pl.Buffered(1)
    tok = pl.BlockSpec((PERM_SEQS, tt, D_MODEL), lambda s, t: (s, t, 0))
    return pl.pallas_call(
        functools.partial(_s5_out_kernel, tt=tt, f=f),
        grid=(slots // PERM_SEQS, seq // tt),
        in_specs=[pl.BlockSpec((tt, PERM_SEQS, D_MODEL), lambda s, t: (t, s, 0)),
                  tok,
                  pl.BlockSpec((PERM_SEQS, 6, D_MODEL), lambda s, t: (s, 0, 0)),
                  pl.BlockSpec((None, D_MODEL, 2 * f), lambda s, t: (wl, 0, 0), pipeline_mode=resident),
                  pl.BlockSpec((None, f, D_MODEL), lambda s, t: (wl, 0, 0), pipeline_mode=resident),
                  pl.BlockSpec((1, D_MODEL), lambda s, t: (0, 0)),
                  pl.BlockSpec((1, D_MODEL), lambda s, t: (0, 0))],
        out_specs=tok,
        out_shape=jax.ShapeDtypeStruct(x.shape, F32),
        compiler_params=_params(2),
        name="s5_out",
    )(y, x, mods, w_glu, w_out, ln_g.reshape(1, D_MODEL), ln_b.reshape(1, D_MODEL))


def _ffn_kernel(x_ref, m_ref, wi_hbm, wo_hbm, lng_ref, lnb_ref, o_ref,
                h_sc, wab_sc, wo_sc, stage_a, stage_b, stage_o, sem, *, f, wl):
    n_slab = f // MXU_TILE
    m = m_ref[0]
    h = x_ref[...] * (1.0 + m[4:5]) + m[3:4]
    h_sc[...] = h.astype(BF16)

    def slab_copies(k, slot):
        c0 = k * MXU_TILE
        return (pltpu.make_async_copy(wi_hbm.at[wl, :, pl.ds(c0, MXU_TILE)], stage_a.at[slot], sem.at[0, slot]),
                pltpu.make_async_copy(wi_hbm.at[wl, :, pl.ds(f + c0, MXU_TILE)], stage_b.at[slot], sem.at[1, slot]),
                pltpu.make_async_copy(wo_hbm.at[wl, pl.ds(c0, MXU_TILE), :], stage_o.at[slot], sem.at[2, slot]))

    def run(load_weights):
        ahead = FFN_STAGE_SLOTS - 1
        if load_weights:
            for k in range(min(ahead, n_slab)):
                for cp in slab_copies(k, k % FFN_STAGE_SLOTS):
                    cp.start()
        acc = None
        for k in range(n_slab):
            c0 = k * MXU_TILE
            if load_weights:
                slot = k % FFN_STAGE_SLOTS
                if k + ahead < n_slab:
                    for cp in slab_copies(k + ahead, (k + ahead) % FFN_STAGE_SLOTS):
                        cp.start()
                for cp in slab_copies(k, slot):
                    cp.wait()
                wab_sc[:, c0:c0 + MXU_TILE] = stage_a[slot].astype(BF16)
                wab_sc[:, f + c0:f + c0 + MXU_TILE] = stage_b[slot].astype(BF16)
                wo_sc[c0:c0 + MXU_TILE, :] = stage_o[slot].astype(BF16)
            a = _dot(h_sc[...], wab_sc[:, c0:c0 + MXU_TILE])
            b = _dot(h_sc[...], wab_sc[:, f + c0:f + c0 + MXU_TILE])
            z = (a * _sigmoid(a)) * b
            part = _dot(z.astype(BF16), wo_sc[c0:c0 + MXU_TILE, :])
            acc = part if acc is None else acc + part
        r = DEEPNORM_ALPHA * x_ref[...] + m[5:6] * acc
        o_ref[...] = _layer_norm(r, lng_ref[...], lnb_ref[...])

    first = jnp.logical_and(pl.program_id(0) == 0, pl.program_id(1) == 0)

    @pl.when(first)
    def _():
        run(True)

    @pl.when(jnp.logical_not(first))
    def _():
        run(False)


def _ffn(x, mods, w_in, w_out, wl, ln_g, ln_b, nb, seq, tm):
    f = w_out.shape[1]
    nt = seq // tm
    per_batch = mods.shape[0] > 1
    tok = pl.BlockSpec((tm, D_MODEL), lambda b, t: (b * nt + t, 0))
    return pl.pallas_call(
        functools.partial(_ffn_kernel, f=f, wl=wl),
        grid=(nb, nt),
        in_specs=[tok,
                  pl.BlockSpec((1, 6, D_MODEL), lambda b, t: (b if per_batch else 0, 0, 0)),
                  pl.BlockSpec(memory_space=pl.ANY),
                  pl.BlockSpec(memory_space=pl.ANY),
                  pl.BlockSpec((1, D_MODEL), lambda b, t: (0, 0)),
                  pl.BlockSpec((1, D_MODEL), lambda b, t: (0, 0))],
        out_specs=tok,
        out_shape=jax.ShapeDtypeStruct((nb * seq, D_MODEL), F32),
        scratch_shapes=[pltpu.VMEM((tm, D_MODEL), BF16),
                        pltpu.VMEM((D_MODEL, 2 * f), BF16),
                        pltpu.VMEM((f, D_MODEL), BF16),
                        pltpu.VMEM((FFN_STAGE_SLOTS, D_MODEL, MXU_TILE), F32),
                        pltpu.VMEM((FFN_STAGE_SLOTS, D_MODEL, MXU_TILE), F32),
                        pltpu.VMEM((FFN_STAGE_SLOTS, MXU_TILE, D_MODEL), F32),
                        pltpu.SemaphoreType.DMA((3, FFN_STAGE_SLOTS))],
        compiler_params=_params(2),
        name="ffn",
    )(x, mods, w_in, w_out, ln_g.reshape(1, D_MODEL), ln_b.reshape(1, D_MODEL))


def _qkv_kernel(*refs, rope):
    if rope:
        x_ref, m_ref, w_ref, qg_ref, kg_ref, cos_ref, sin_ref = refs[:7]
    else:
        x_ref, m_ref, w_ref, qg_ref, kg_ref = refs[:5]
    q_ref, k_ref, v_ref = refs[-3:]
    rows = x_ref.shape[0]
    m = m_ref[0]
    h = x_ref[...] * (1.0 + m[1:2]) + m[0:1]
    qkv = _dot(h.astype(BF16), w_ref[...])
    if rope:
        cos = cos_ref[...]
        sin = sin_ref[...]
        lane = lax.broadcasted_iota(jnp.int32, cos.shape, 1)
        first = jnp.bitwise_and(lane, AXIS_PAIRS) == 0
    for hd in range(N_HEADS + N_KV_HEADS):
        xh = qkv[:, hd * HEAD_DIM:(hd + 1) * HEAD_DIM]
        gain = qg_ref[...] if hd < N_HEADS else kg_ref[...]
        n = xh * lax.rsqrt(jnp.mean(xh * xh, axis=-1, keepdims=True) + RMS_EPS) * gain
        if rope:
            up = pltpu.roll(n, HEAD_DIM - AXIS_PAIRS, 1)
            down = pltpu.roll(n, AXIS_PAIRS, 1)
            n = n * cos + jnp.where(first, up, down) * sin
        if hd < N_HEADS:
            q_ref[:, hd * HEAD_DIM:(hd + 1) * HEAD_DIM] = (n * ATTN_SCALE).astype(BF16)
        else:
            k_ref[pl.ds(hd - N_HEADS, rows, stride=N_KV_HEADS), :] = n
    for g in range(N_KV_HEADS):
        v_ref[pl.ds(g, rows, stride=N_KV_HEADS), :] = (
            qkv[:, D_Q + D_KV + g * HEAD_DIM:D_Q + D_KV + (g + 1) * HEAD_DIM])


def _rope_tables(seq):
    pos = jnp.arange(seq, dtype=jnp.int32)
    row = (pos // GRID_W).astype(F32)
    col = (pos % GRID_W).astype(F32)
    inv = ROPE_THETA ** (-jnp.arange(AXIS_PAIRS, dtype=F32) / AXIS_PAIRS)
    ar = row[:, None] * inv
    ac = col[:, None] * inv
    cos = jnp.concatenate([jnp.cos(ar), jnp.cos(ar), jnp.cos(ac), jnp.cos(ac)], axis=-1)
    sin = jnp.concatenate([-jnp.sin(ar), jnp.sin(ar), -jnp.sin(ac), jnp.sin(ac)], axis=-1)
    return cos, sin


def _qkv(x, mods, w_qkv, wl, q_gain, k_gain, nb, seq, tm, rope, kv_layers=1, kv_prev=None):
    nt = seq // tm
    per_batch = mods.shape[0] > 1
    in_specs = [pl.BlockSpec((tm, D_MODEL), lambda b, t: (b * nt + t, 0)),
                pl.BlockSpec((1, 6, D_MODEL), lambda b, t: (b if per_batch else 0, 0, 0)),
                pl.BlockSpec((None, D_MODEL, QKV_DIM), lambda b, t: (wl, 0, 0)),
                pl.BlockSpec((1, HEAD_DIM), lambda b, t: (0, 0)),
                pl.BlockSpec((1, HEAD_DIM), lambda b, t: (0, 0))]
    args = [x, mods, w_qkv, q_gain.reshape(1, HEAD_DIM), k_gain.reshape(1, HEAD_DIM)]
    if rope:
        cos, sin = _rope_tables(seq)
        in_specs += [pl.BlockSpec((tm, HEAD_DIM), lambda b, t: (t, 0))] * 2
        args += [cos, sin]
    n_tok = nb * seq
    kv_slot = wl if kv_layers > 1 else 0
    kv_spec = pl.BlockSpec((None, None, tm * N_KV_HEADS, HEAD_DIM), lambda b, t: (b, kv_slot, t, 0))
    kv_shape = jax.ShapeDtypeStruct((nb, kv_layers, seq * N_KV_HEADS, HEAD_DIM), F32)
    aliases = {}
    if kv_prev is not None:
        aliases = {len(args): 1, len(args) + 1: 2}
        in_specs += [pl.BlockSpec(memory_space=pl.ANY)] * 2
        args += list(kv_prev)
    return pl.pallas_call(
        functools.partial(_qkv_kernel, rope=rope),
        grid=(nb, nt),
        in_specs=in_specs,
        out_specs=[pl.BlockSpec((tm, D_Q), lambda b, t: (b * nt + t, 0)), kv_spec, kv_spec],
        out_shape=[jax.ShapeDtypeStruct((n_tok, D_Q), BF16), kv_shape, kv_shape],
        input_output_aliases=aliases,
        compiler_params=_params(2),
        name="qkv_rope" if rope else "qkv",
    )(*args)


def _attn_kernel(*refs, has_cache):
    if has_cache:
        (q_ref, k_ref, v_ref, ck_ref, cv_ref, x_ref, m_ref, wo_ref, lng_ref, lnb_ref,
         o_ref, oh_sc) = refs
    else:
        q_ref, k_ref, v_ref, x_ref, m_ref, wo_ref, lng_ref, lnb_ref, o_ref, oh_sc = refs
    for g in range(N_KV_HEADS):
        sl = slice(g * HEAD_DIM, (g + 1) * HEAD_DIM)
        head_rows = lambda ref: ref[pl.ds(g, ref.shape[0] // N_KV_HEADS, stride=N_KV_HEADS), :]
        kg = head_rows(k_ref).astype(BF16)
        vg = head_rows(v_ref).astype(BF16)
        if has_cache:
            ckg = head_rows(ck_ref).astype(BF16)
            cvg = head_rows(cv_ref).astype(BF16)
        for r in range(KV_REP):
            hsl = slice((g * KV_REP + r) * HEAD_DIM, (g * KV_REP + r + 1) * HEAD_DIM)
            qh = q_ref[:, hsl]
            s1 = _dot_nt(qh, kg)
            mx = jnp.max(s1, axis=-1, keepdims=True)
            if has_cache:
                s2 = _dot_nt(qh, ckg)
                mx = jnp.maximum(mx, jnp.max(s2, axis=-1, keepdims=True))
            p1 = jnp.exp(s1 - mx)
            den = jnp.sum(p1, axis=-1, keepdims=True)
            o = _dot(p1.astype(BF16), vg)
            if has_cache:
                p2 = jnp.exp(s2 - mx)
                den = den + jnp.sum(p2, axis=-1, keepdims=True)
                o = o + _dot(p2.astype(BF16), cvg)
            oh_sc[:, hsl] = (o / den).astype(BF16)
    mix = _dot(oh_sc[...], wo_ref[...])
    m = m_ref[0]
    r = DEEPNORM_ALPHA * x_ref[...] + m[2:3] * mix
    o_ref[...] = _layer_norm(r, lng_ref[...], lnb_ref[...])


def _attention(q, k, v, kv_slot, cache_k, cache_v, layer_j, x, mods, w_o, ln_g, ln_b, nb, seq, tq):
    nt = seq // tq
    per_batch = mods.shape[0] > 1
    has_cache = cache_k is not None
    tok = pl.BlockSpec((tq, D_MODEL), lambda b, t: (b * nt + t, 0))
    kv = pl.BlockSpec((None, None, seq * N_KV_HEADS, HEAD_DIM), lambda b, t: (b, kv_slot, 0, 0))
    in_specs, args = [tok, kv, kv], [q, k, v]
    if has_cache:
        past = cache_k.shape[2]
        cspec = pl.BlockSpec((None, None, past * N_KV_HEADS, HEAD_DIM), lambda b, t: (b, layer_j, 0, 0))
        in_specs += [cspec, cspec]
        flat = (cache_k.shape[0], cache_k.shape[1], past * N_KV_HEADS, HEAD_DIM)
        args += [cache_k.reshape(flat), cache_v.reshape(flat)]
    in_specs += [tok,
                 pl.BlockSpec((1, 6, D_MODEL), lambda b, t: (b if per_batch else 0, 0, 0)),
                 pl.BlockSpec((None, D_Q, D_MODEL), lambda b, t: (layer_j, 0, 0)),
                 pl.BlockSpec((1, D_MODEL), lambda b, t: (0, 0)),
                 pl.BlockSpec((1, D_MODEL), lambda b, t: (0, 0))]
    args += [x, mods, w_o, ln_g.reshape(1, D_MODEL), ln_b.reshape(1, D_MODEL)]
    return pl.pallas_call(
        functools.partial(_attn_kernel, has_cache=has_cache),
        grid=(nb, nt),
        in_specs=in_specs,
        out_specs=tok,
        out_shape=jax.ShapeDtypeStruct((nb * seq, D_MODEL), F32),
        scratch_shapes=[pltpu.VMEM((tq, D_Q), BF16)],
        compiler_params=_params(2),
        name="attn_cache" if has_cache else "attn",
    )(*args)


def _latent_h0(st):
    f = st[:, 0].transpose(2, 1, 0, 3)
    b = st[:, 1].transpose(2, 1, 0, 3)
    z = jnp.zeros_like(f)
    even = jnp.concatenate([f, z], axis=-1)
    odd = jnp.concatenate([z, b], axis=-1)
    h0 = jnp.stack([even, odd], axis=3)
    return h0.reshape(N_GROUPS, 2, 2 * st.shape[0], 2 * STATE_DIM)


def kernel(x_prompt, x_sample, c, cache_k, cache_v, state_s5, c_ctx, w_mod, b_mod, ln_g, ln_b, w_s5_in, s5_a_re, s5_a_im, s5_log_dt, s5_b_re, s5_b_im, s5_c_re, s5_c_im, s5_d, w_s5_glu, w_s5_out, w_qkv, q_norm_g, k_norm_g, w_o, w_ffn_in, w_ffn_out):
    nbp, seqp, _ = x_prompt.shape
    nbs, seqs, _ = x_sample.shape
    xp = x_prompt.reshape(nbp * seqp, D_MODEL)
    xs = x_sample.reshape(nbs * seqs, D_MODEL)
    s5_slots_s, s5_seq_s = 2 * nbs, seqs // 2

    cond = jnp.concatenate([c_ctx[None, :], c, jnp.zeros((8 - 1 - nbs, D_MODEL), F32)], axis=0)
    mods = _adaln(cond, w_mod, b_mod)
    mods_p = mods[:, 0:1].reshape(DEPTH, 1, 6, D_MODEL)
    mods_s = mods[:, 1:1 + nbs].reshape(DEPTH, nbs, 6, D_MODEL)

    w_in, w_glu, w_out = w_s5_in.astype(BF16), w_s5_glu.astype(BF16), w_s5_out.astype(BF16)
    wq, wo = w_qkv.astype(BF16), w_o.astype(BF16)

    kv_rows = (nbp, DEPTH // 2, seqp * N_KV_HEADS, HEAD_DIM)
    new_kv, new_s = (jnp.zeros(kv_rows, F32), jnp.zeros(kv_rows, F32)), []
    for layer in range(DEPTH):
        j = layer // 2
        mp, ms = mods_p[layer], mods_s[layer]
        lg0, lb0, lg1, lb1 = ln_g[layer, 0], ln_b[layer, 0], ln_g[layer, 1], ln_b[layer, 1]
        if layer % 2 == 0:
            w1, w2, lam = _s5_tables(s5_a_re[j], s5_a_im[j], s5_log_dt[j], s5_b_re[j], s5_b_im[j],
                                     s5_c_re[j], s5_c_im[j])
            mp_slots = jnp.broadcast_to(mp, (nbp, 6, D_MODEL))
            ms_slots = jnp.repeat(ms, 2, axis=0)
            xp3 = xp.reshape(nbp, seqp, D_MODEL)
            xs3 = xs.reshape(s5_slots_s, s5_seq_s, D_MODEL)
            up = _s5_in(xp3, mp_slots, w_in, j, 128)
            us = _s5_in(xs3, ms_slots, w_in, j, 128)
            yp, fin = _s5_core(up, w1, w2, lam, s5_d[j], None, True)
            ys, _ = _s5_core(us, w1, w2, lam, s5_d[j], _latent_h0(state_s5[:, j]), False)
            fin = fin.reshape(N_GROUPS, 2, nbp, 2, STATE_DIM)
            new_s.append(fin.transpose(2, 3, 1, 0, 4))
            xp = _s5_out(yp, xp3, mp_slots, w_glu, w_out, j, lg0, lb0, 128).reshape(nbp * seqp, D_MODEL)
            xs = _s5_out(ys, xs3, ms_slots, w_glu, w_out, j, lg0, lb0, 128).reshape(nbs * seqs, D_MODEL)
        else:
            qp, kp, vp = _qkv(xp, mp, wq, j, q_norm_g[j], k_norm_g[j], nbp, seqp, 256, False,
                              kv_layers=DEPTH // 2, kv_prev=new_kv)
            new_kv = (kp, vp)
            qs, ks, vs = _qkv(xs, ms, wq, j, q_norm_g[j], k_norm_g[j], nbs, seqs, 512, True)
            xp = _attention(qp, kp, vp, j, None, None, j, xp, mp, wo, lg0, lb0, nbp, seqp, 256)
            xs = _attention(qs, ks, vs, 0, cache_k, cache_v, j, xs, ms, wo, lg0, lb0, nbs, seqs, 256)
        xp = _ffn(xp, mp, w_ffn_in, w_ffn_out, layer, lg1, lb1, 1, nbp * seqp, 512)
        xs = _ffn(xs, ms, w_ffn_in, w_ffn_out, layer, lg1, lb1, nbs, seqs, 512)

    y_prompt = xp.reshape(nbp, seqp, D_MODEL)
    y_sample = xs.reshape(nbs, seqs, D_MODEL)
    kv_out = (nbp, DEPTH // 2, seqp, N_KV_HEADS, HEAD_DIM)
    return (y_prompt, y_sample, new_kv[0].reshape(kv_out), new_kv[1].reshape(kv_out),
            jnp.stack(new_s, axis=1))
```

```python
import functools
import math

import jax
import jax.numpy as jnp
from jax import lax
from jax.experimental import pallas as pl
from jax.experimental.pallas import tpu as pltpu

F32 = jnp.float32
BF16 = jnp.bfloat16

D_MODEL = 1024
DEPTH = 4
N_GROUPS = 64
GROUP_CH = 16
STATE_DIM = 64
HEAD_DIM = 128
N_HEADS = 8
N_KV_HEADS = 2
KV_REP = N_HEADS // N_KV_HEADS
D_Q = N_HEADS * HEAD_DIM
D_KV = N_KV_HEADS * HEAD_DIM
QKV_DIM = D_Q + 2 * D_KV
GRID_W = 64
ROPE_THETA = 10000.0
AXIS_PAIRS = HEAD_DIM // 4
ATTN_SCALE = HEAD_DIM ** -0.5
DEEPNORM_ALPHA = (2.0 * DEPTH) ** 0.25
LN_EPS = 1e-6
RMS_EPS = 1e-6

V7X_VMEM_LIMIT_BYTES = 56 * 1024 * 1024
LANES = 128
SUBLANES = 8
MXU_TILE = 256
FFN_STAGE_SLOTS = 2
GROUPS_PER_BLOCK = LANES // GROUP_CH
N_GROUP_BLOCKS = N_GROUPS // GROUPS_PER_BLOCK
S5_CHUNK = MXU_TILE // GROUP_CH
CHUNK_LANES = S5_CHUNK * GROUP_CH
FOLD_UNROLL = 4
PERM_SEQS = SUBLANES
PERM_STEPS = MXU_TILE // PERM_SEQS
NT_DIMS = (((1,), (1,)), ((), ()))


def _params(n_axes):
    return pltpu.CompilerParams(dimension_semantics=("arbitrary",) * n_axes,
                                vmem_limit_bytes=V7X_VMEM_LIMIT_BYTES)


def _sigmoid(x):
    return 1.0 / (1.0 + jnp.exp(-x))


def _gelu_tanh(x):
    cdf = 0.5 * (1.0 + jnp.tanh(math.sqrt(2.0 / math.pi) * (x + 0.044715 * (x * x * x))))
    return x * cdf


def _layer_norm(r, g, b):
    mu = jnp.mean(r, axis=-1, keepdims=True)
    d = r - mu
    var = jnp.mean(d * d, axis=-1, keepdims=True)
    return d * lax.rsqrt(var + LN_EPS) * g + b


def _dot(a, b):
    return jnp.dot(a, b, preferred_element_type=F32)


def _dot_nt(a, b):
    return lax.dot_general(a, b, NT_DIMS, preferred_element_type=F32)


def _adaln_kernel(c_ref, w_ref, b_ref, o_ref):
    c = c_ref[...]
    s = c * _sigmoid(c)
    o_ref[0] = _dot(s.astype(BF16), w_ref[0].astype(BF16)) + b_ref[0]


def _adaln(cond, w_mod, b_mod):
    tn = 1536
    n = 6 * D_MODEL
    return pl.pallas_call(
        _adaln_kernel,
        grid=(DEPTH, n // tn),
        in_specs=[pl.BlockSpec((8, D_MODEL), lambda l, j: (0, 0)),
                  pl.BlockSpec((1, D_MODEL, tn), lambda l, j: (l, 0, j)),
                  pl.BlockSpec((1, 1, tn), lambda l, j: (l, 0, j))],
        out_specs=pl.BlockSpec((1, 8, tn), lambda l, j: (l, 0, j)),
        out_shape=jax.ShapeDtypeStruct((DEPTH, 8, n), F32),
        compiler_params=_params(2),
        name="adaln",
    )(cond, w_mod, b_mod.reshape(DEPTH, 1, n))


def _s5_prep_kernel(a_ref, ldt_ref, bt_ref, c_ref, w1_ref, w2_ref, lam_ref):
    half = STATE_DIM
    lane1 = lax.broadcasted_iota(jnp.int32, (1, LANES), 1)
    sgn1 = jnp.where(lane1 < half, -1.0, 1.0)
    lane_h = lax.broadcasted_iota(jnp.int32, (GROUP_CH, LANES), 1)
    first_h = lane_h < half
    conj_h = jnp.where(first_h, 1.0, -1.0)
    lane_c = lax.broadcasted_iota(jnp.int32, (GROUP_CH, CHUNK_LANES), 1)
    t = S5_CHUNK

    def cmul(pr, pi, x):
        return pr * x + (pi * sgn1) * pltpu.roll(x, half, 1)

    def pack_states(f, b):
        return (jnp.where(first_h, f, pltpu.roll(b, half, 1)),
                jnp.where(first_h, pltpu.roll(f, half, 1), b))

    def group(g, carry):
        kt, qs, cks, lam_t = [], [], [], []
        for d in range(2):
            a_re = a_ref[0, d, g]
            a_im = a_ref[1, d, g]
            dt = jnp.exp(ldt_ref[d, g])
            mag = jnp.exp(dt * a_re)
            lr = mag * jnp.cos(dt * a_im)
            li = mag * jnp.sin(dt * a_im)
            den = a_re * a_re + a_im * a_im
            nr = lr - 1.0
            k_re = (nr * a_re + li * a_im) / den
            k_im = (li * a_re - nr * a_im) / den
            bb = cmul(k_re, k_im, bt_ref[d, g])
            cc = c_ref[d, g]
            pr = jnp.ones((1, LANES), F32)
            pi = jnp.zeros((1, LANES), F32)
            ck, q = [], []
            for k in range(t + 1):
                ck.append(cmul(pr, pi, cc))
                if k < t:
                    q.append(cmul(pr, pi, bb))
                    pr, pi = pr * lr - pi * li, pr * li + pi * lr
            lam_t.append((pr, pi))
            order = range(t) if d == 0 else range(t - 1, -1, -1)
            rhs = jnp.concatenate([ck[k] for k in order], axis=0).astype(BF16)
            kt.append(_dot_nt((bb * conj_h).astype(BF16), rhs))
            qs.append(q)
            cks.append(ck)
        for j in range(t):
            rows = slice(j * GROUP_CH, (j + 1) * GROUP_CH)
            tf = kt[0] if j == 0 else pltpu.roll(kt[0], GROUP_CH * j, 1)
            tf = jnp.where(lane_c >= GROUP_CH * j, tf, 0.0)
            back = GROUP_CH * (t - 1 - j)
            tb = kt[1] if back == 0 else pltpu.roll(kt[1], CHUNK_LANES - back, 1)
            tb = jnp.where(lane_c < GROUP_CH * (j + 1), tb, 0.0)
            w1_ref[g, rows, 0:CHUNK_LANES] = (tf + tb).astype(BF16)
            s_re, s_im = pack_states(qs[0][t - 1 - j], qs[1][j])
            w1_ref[g, rows, CHUNK_LANES:CHUNK_LANES + LANES] = s_re.astype(BF16)
            w1_ref[g, rows, CHUNK_LANES + LANES:CHUNK_LANES + 2 * LANES] = s_im.astype(BF16)
            c_re, c_im = pack_states(cks[0][j + 1] * conj_h, cks[1][t - j] * conj_h)
            w2_ref[g, rows, 0:LANES] = c_re.astype(BF16)
            w2_ref[g, rows, LANES:2 * LANES] = c_im.astype(BF16)
        fwd1 = lane1 < half
        lam_ref[g, 0:1, :] = jnp.where(fwd1, lam_t[0][0], lam_t[1][0])
        lam_ref[g, 1:2, :] = jnp.where(fwd1, lam_t[0][1], lam_t[1][1])
        return carry

    lax.fori_loop(0, GROUPS_PER_BLOCK, group, 0, unroll=2)


def _s5_tables(a_re, a_im, log_dt, b_re, b_im, c_re, c_im):
    g, p, h = N_GROUPS, STATE_DIM, GROUP_CH
    dup = lambda x: jnp.concatenate([x, x], axis=-1)
    a2 = jnp.stack([dup(a_re), dup(a_im)]).reshape(2, 2, g, 1, 2 * p)
    bt = jnp.concatenate([b_re.transpose(0, 1, 3, 2), b_im.transpose(0, 1, 3, 2)], axis=-1)
    cc = jnp.concatenate([c_re, c_im], axis=-1)
    gb = GROUPS_PER_BLOCK
    return pl.pallas_call(
        _s5_prep_kernel,
        grid=(N_GROUP_BLOCKS,),
        in_specs=[pl.BlockSpec((2, 2, gb, 1, 2 * p), lambda i: (0, 0, i, 0, 0)),
                  pl.BlockSpec((2, gb, 1, 1), lambda i: (0, i, 0, 0)),
                  pl.BlockSpec((2, gb, h, 2 * p), lambda i: (0, i, 0, 0)),
                  pl.BlockSpec((2, gb, h, 2 * p), lambda i: (0, i, 0, 0))],
        out_specs=[pl.BlockSpec((gb, CHUNK_LANES, CHUNK_LANES + 2 * LANES), lambda i: (i, 0, 0)),
                   pl.BlockSpec((gb, CHUNK_LANES, 2 * LANES), lambda i: (i, 0, 0)),
                   pl.BlockSpec((gb, 2, LANES), lambda i: (i, 0, 0))],
        out_shape=[jax.ShapeDtypeStruct((g, CHUNK_LANES, CHUNK_LANES + 2 * LANES), BF16),
                   jax.ShapeDtypeStruct((g, CHUNK_LANES, 2 * LANES), BF16),
                   jax.ShapeDtypeStruct((g, 2, LANES), F32)],
        compiler_params=_params(1),
        name="s5_prep",
    )(a2, log_dt.reshape(2, g, 1, 1), bt, cc)


def _row_perm(to_time_major):
    n = PERM_SEQS * PERM_STEPS
    r = lax.broadcasted_iota(jnp.int32, (n, n), 0)
    c = lax.broadcasted_iota(jnp.int32, (n, n), 1)
    if to_time_major:
        src = jnp.bitwise_and(r, PERM_SEQS - 1) * PERM_STEPS + lax.shift_right_logical(r, 3)
    else:
        src = jnp.bitwise_and(r, PERM_STEPS - 1) * PERM_SEQS + lax.shift_right_logical(r, 5)
    return jnp.where(c == src, 1.0, 0.0).astype(BF16)


def _s5_in_kernel(x_ref, m_ref, w_ref, o_ref, *, tt):
    scale = 1.0 + m_ref[:, 1:2, :]
    shift = m_ref[:, 0:1, :]
    perm = _row_perm(True)
    n = PERM_SEQS * PERM_STEPS
    for k in range(tt // (2 * PERM_STEPS)):
        pieces = []
        for hf in range(2):
            t0 = (2 * k + hf) * PERM_STEPS
            h = x_ref[:, t0:t0 + PERM_STEPS, :] * scale + shift
            hb = h.reshape(n, D_MODEL).astype(BF16)
            pieces.append(_dot(perm, hb).astype(BF16))
        u = _dot(jnp.concatenate(pieces, axis=0), w_ref[...])
        o_ref[2 * k * PERM_STEPS:(2 * k + 2) * PERM_STEPS] = u.reshape(2 * PERM_STEPS, PERM_SEQS, D_MODEL)


def _s5_in(x, mods, w_in, wl, tt):
    slots, seq, _ = x.shape
    return pl.pallas_call(
        functools.partial(_s5_in_kernel, tt=tt),
        grid=(slots // PERM_SEQS, seq // tt),
        in_specs=[pl.BlockSpec((PERM_SEQS, tt, D_MODEL), lambda s, t: (s, t, 0)),
                  pl.BlockSpec((PERM_SEQS, 6, D_MODEL), lambda s, t: (s, 0, 0)),
                  pl.BlockSpec((None, D_MODEL, D_MODEL), lambda s, t: (wl, 0, 0))],
        out_specs=pl.BlockSpec((tt, PERM_SEQS, D_MODEL), lambda s, t: (t, s, 0)),
        out_shape=jax.ShapeDtypeStruct((seq, slots, D_MODEL), F32),
        compiler_params=_params(2),
        name="s5_in",
    )(x, mods, w_in)


def _block_transpose8(v):
    lane = lax.broadcasted_iota(jnp.int32, (SUBLANES, LANES), 1)
    v = list(v)
    for d in (4, 2, 1):
        low = jnp.bitwise_and(lane, GROUP_CH * d) == 0
        nxt = list(v)
        for i in range(8):
            if i & d == 0:
                a, b = v[i], v[i + d]
                nxt[i] = jnp.where(low, a, pltpu.roll(b, GROUP_CH * d, 1))
                nxt[i + d] = jnp.where(low, pltpu.roll(a, LANES - GROUP_CH * d, 1), b)
        v = nxt
    return v


def _s5_core_kernel(*refs, seq, nq, paired, has_fin):
    refs = list(refs)
    u_ref, w1_ref, w2_ref, lam_ref, d_ref = refs[:5]
    pos = 5
    h0_ref = None
    if paired:
        h0_ref = refs[pos]
        pos += 1
    y_ref = refs[pos]
    pos += 1
    fin_ref = refs[pos] if has_fin else None
    z_sc, yt_sc, bs_sc, sp_sc = refs[-4 - int(paired):][:4]
    sp2_sc = refs[-1] if paired else None
    gpb = GROUPS_PER_BLOCK
    nc = seq // S5_CHUNK
    ns = nq * SUBLANES
    half = STATE_DIM

    def fold(c, carry):
        for qi in range(nq):
            sl = slice(qi * SUBLANES, (qi + 1) * SUBLANES)
            r0 = pl.multiple_of(c * ns + qi * SUBLANES, SUBLANES)
            for hf in range(2):
                v = [u_ref[c * S5_CHUNK + hf * 8 + t, sl, :] for t in range(8)]
                w = _block_transpose8(v)
                for g in range(gpb):
                    z_sc[g, pl.ds(r0, SUBLANES), hf * LANES:(hf + 1) * LANES] = w[g]
        return carry

    lax.fori_loop(0, nc, fold, 0, unroll=FOLD_UNROLL // nq)

    for g in range(gpb):
        m1 = _dot(z_sc[g].astype(BF16), w1_ref[g])
        yt_sc[g] = m1[:, 0:CHUNK_LANES]
        bs_sc[g] = m1[:, CHUNK_LANES:]

    lane = lax.broadcasted_iota(jnp.int32, (SUBLANES, LANES), 1)
    fwd = jnp.bitwise_and(lane, half) == 0
    lam = [(jnp.broadcast_to(lam_ref[g, 0:1, :], (SUBLANES, LANES)),
            jnp.broadcast_to(lam_ref[g, 1:2, :], (SUBLANES, LANES))) for g in range(gpb)]

    def run_pass(init, dst):
        def body(i, carry):
            out = []
            for g in range(gpb):
                l_re, l_im = lam[g]
                for qi in range(nq):
                    s_re, s_im = carry[2 * (g * nq + qi)], carry[2 * (g * nq + qi) + 1]
                    ri = pl.ds(pl.multiple_of(i * ns + qi * SUBLANES, SUBLANES), SUBLANES)
                    rr = pl.ds(pl.multiple_of((nc - 1 - i) * ns + qi * SUBLANES, SUBLANES), SUBLANES)
                    dst[g, ri, 0:half] = s_re[:, 0:half]
                    dst[g, ri, LANES:LANES + half] = s_im[:, 0:half]
                    dst[g, rr, half:LANES] = s_re[:, half:LANES]
                    dst[g, rr, LANES + half:2 * LANES] = s_im[:, half:LANES]
                    x_re = jnp.where(fwd, bs_sc[g, ri, 0:LANES], bs_sc[g, rr, 0:LANES])
                    x_im = jnp.where(fwd, bs_sc[g, ri, LANES:2 * LANES], bs_sc[g, rr, LANES:2 * LANES])
                    out.append(l_re * s_re - l_im * s_im + x_re)
                    out.append(l_re * s_im + l_im * s_re + x_im)
            return tuple(out)

        return lax.fori_loop(0, nc, body, init)

    if paired:
        init = []
        for g in range(gpb):
            init += [h0_ref[g, 0], h0_ref[g, 1]]
        mid = run_pass(tuple(init), sp_sc)
        handed = [jnp.where(fwd, pltpu.roll(s, 1, 0), pltpu.roll(s, SUBLANES - 1, 0)) for s in mid]
        fin = run_pass(tuple(handed), sp2_sc)
    else:
        zero = jnp.zeros((SUBLANES, LANES), F32)
        fin = run_pass((zero,) * (2 * gpb * nq), sp_sc)

    if has_fin:
        for g in range(gpb):
            for qi in range(nq):
                sl = slice(qi * SUBLANES, (qi + 1) * SUBLANES)
                fin_ref[g, 0, sl, :] = fin[2 * (g * nq + qi)]
                fin_ref[g, 1, sl, :] = fin[2 * (g * nq + qi) + 1]

    if paired:
        shape = (nc * ns, 2 * LANES)
        row = lax.broadcasted_iota(jnp.int32, shape, 0)
        col = lax.broadcasted_iota(jnp.int32, shape, 1)
        first_pass = (jnp.bitwise_and(row, 1) == 0) == (jnp.bitwise_and(col, half) == 0)
    for g in range(gpb):
        states = sp_sc[g]
        if paired:
            states = jnp.where(first_pass, states, sp2_sc[g])
        yt_sc[g] = yt_sc[g] + _dot_nt(states.astype(BF16), w2_ref[g])

    d = jnp.broadcast_to(d_ref[...], (SUBLANES, LANES))

    def unfold(c, carry):
        for qi in range(nq):
            sl = slice(qi * SUBLANES, (qi + 1) * SUBLANES)
            r0 = pl.multiple_of(c * ns + qi * SUBLANES, SUBLANES)
            for hf in range(2):
                w = [yt_sc[g, pl.ds(r0, SUBLANES), hf * LANES:(hf + 1) * LANES] for g in range(gpb)]
                v = _block_transpose8(w)
                for t in range(8):
                    step = c * S5_CHUNK + hf * 8 + t
                    y_ref[step, sl, :] = v[t] + d * u_ref[step, sl, :]
        return carry

    lax.fori_loop(0, nc, unfold, 0, unroll=FOLD_UNROLL // nq)


def _s5_core(u, w1, w2, lam, d_skip, h0, want_final):
    seq, slots, _ = u.shape
    paired = h0 is not None
    nq = 1 if paired else slots // SUBLANES
    ns = nq * SUBLANES
    gpb = GROUPS_PER_BLOCK
    rows = (seq // S5_CHUNK) * ns
    u_spec = pl.BlockSpec((seq, ns, LANES), lambda s, g: (0, s, g))
    in_specs = [u_spec,
                pl.BlockSpec((gpb, CHUNK_LANES, CHUNK_LANES + 2 * LANES), lambda s, g: (g, 0, 0)),
                pl.BlockSpec((gpb, CHUNK_LANES, 2 * LANES), lambda s, g: (g, 0, 0)),
                pl.BlockSpec((gpb, 2, LANES), lambda s, g: (g, 0, 0)),
                pl.BlockSpec((1, LANES), lambda s, g: (0, g))]
    args = [u, w1, w2, lam, d_skip.reshape(1, D_MODEL)]
    if paired:
        in_specs.append(pl.BlockSpec((gpb, 2, SUBLANES, LANES), lambda s, g: (g, 0, s, 0)))
        args.append(h0)
    out_specs = [u_spec]
    out_shape = [jax.ShapeDtypeStruct(u.shape, F32)]
    if want_final:
        out_specs.append(pl.BlockSpec((gpb, 2, ns, LANES), lambda s, g: (g, 0, s, 0)))
        out_shape.append(jax.ShapeDtypeStruct((N_GROUPS, 2, slots, LANES), F32))
    res = pl.pallas_call(
        functools.partial(_s5_core_kernel, seq=seq, nq=nq, paired=paired, has_fin=want_final),
        grid=(slots // ns, N_GROUP_BLOCKS),
        in_specs=in_specs,
        out_specs=out_specs,
        out_shape=out_shape,
        scratch_shapes=[pltpu.VMEM((gpb, rows, CHUNK_LANES), F32) for _ in range(5 if paired else 4)],
        compiler_params=_params(2),
        name="s5_core",
    )(*args)
    return (res[0], res[1]) if want_final else (res[0], None)


def _s5_out_kernel(y_ref, x_ref, m_ref, wab_ref, wo_ref, lng_ref, lnb_ref, o_ref, *, tt, f):
    gate = m_ref[:, 2:3, :]
    perm = _row_perm(False)
    n = PERM_SEQS * PERM_STEPS
    lng = lng_ref[...].reshape(1, 1, D_MODEL)
    lnb = lnb_ref[...].reshape(1, 1, D_MODEL)
    for k in range(tt // (2 * PERM_STEPS)):
        t0 = 2 * k * PERM_STEPS
        hb = _gelu_tanh(y_ref[t0:t0 + 2 * PERM_STEPS].reshape(2 * n, D_MODEL)).astype(BF16)
        acc = None
        for c0 in range(0, f, MXU_TILE):
            val = _dot(hb, wab_ref[:, c0:c0 + MXU_TILE])
            gte = _dot(hb, wab_ref[:, f + c0:f + c0 + MXU_TILE])
            z = (val * _sigmoid(gte)).astype(BF16)
            zp = jnp.concatenate([_dot(perm, z[0:n]).astype(BF16), _dot(perm, z[n:2 * n]).astype(BF16)],
                                 axis=0)
            part = _dot(zp, wo_ref[c0:c0 + MXU_TILE, :])
            acc = part if acc is None else acc + part
        for hf in range(2):
            ts = slice(t0 + hf * PERM_STEPS, t0 + (hf + 1) * PERM_STEPS)
            mix = acc[hf * n:(hf + 1) * n].reshape(PERM_SEQS, PERM_STEPS, D_MODEL)
            r = DEEPNORM_ALPHA * x_ref[:, ts, :] + gate * mix
            o_ref[:, ts, :] = _layer_norm(r, lng, lnb)


def _s5_out(y, x, mods, w_glu, w_out, wl, ln_g, ln_b, tt):
    slots, seq, _ = x.shape
    f = w_out.shape[1]
    resident = pl.Buffered(1)
    tok = pl.BlockSpec((PERM_SEQS, tt, D_MODEL), lambda s, t: (s, t, 0))
    return pl.pallas_call(
        functools.partial(_s5_out_kernel, tt=tt, f=f),
        grid=(slots // PERM_SEQS, seq // tt),
        in_specs=[pl.BlockSpec((tt, PERM_SEQS, D_MODEL), lambda s, t: (t, s, 0)),
                  tok,
                  pl.BlockSpec((PERM_SEQS, 6, D_MODEL), lambda s, t: (s, 0, 0)),
                  pl.BlockSpec((None, D_MODEL, 2 * f), lambda s, t: (wl, 0, 0), pipeline_mode=resident),
                  pl.BlockSpec((None, f, D_MODEL), lambda s, t: (wl, 0, 0), pipeline_mode=resident),
                  pl.BlockSpec((1, D_MODEL), lambda s, t: (0, 0)),
                  pl.BlockSpec((1, D_MODEL), lambda s, t: (0, 0))],
        out_specs=tok,
        out_shape=jax.ShapeDtypeStruct(x.shape, F32),
        compiler_params=_params(2),
        name="s5_out",
    )(y, x, mods, w_glu, w_out, ln_g.reshape(1, D_MODEL), ln_b.reshape(1, D_MODEL))


def _ffn_kernel(xp_ref, xs_ref, m_ref, wi_hbm, wo_hbm, lng_ref, lnb_ref, op_ref, os_ref,
                x_sc, h_sc, wab_sc, wo_sc, stage_a, stage_b, stage_o, sem, *, f, wl, n_first):
    n_slab = f // MXU_TILE
    step = pl.program_id(0)
    m = m_ref[0]

    @pl.when(step < n_first)
    def _():
        x_sc[...] = xp_ref[...]

    @pl.when(step >= n_first)
    def _():
        x_sc[...] = xs_ref[...]

    h = x_sc[...] * (1.0 + m[4:5]) + m[3:4]
    h_sc[...] = h.astype(BF16)

    def slab_copies(k, slot):
        c0 = k * MXU_TILE
        return (pltpu.make_async_copy(wi_hbm.at[wl, :, pl.ds(c0, MXU_TILE)], stage_a.at[slot], sem.at[0, slot]),
                pltpu.make_async_copy(wi_hbm.at[wl, :, pl.ds(f + c0, MXU_TILE)], stage_b.at[slot], sem.at[1, slot]),
                pltpu.make_async_copy(wo_hbm.at[wl, pl.ds(c0, MXU_TILE), :], stage_o.at[slot], sem.at[2, slot]))

    def run(load_weights):
        ahead = FFN_STAGE_SLOTS - 1
        if load_weights:
            for k in range(min(ahead, n_slab)):
                for cp in slab_copies(k, k % FFN_STAGE_SLOTS):
                    cp.start()
        acc = None
        for k in range(n_slab):
            c0 = k * MXU_TILE
            if load_weights:
                slot = k % FFN_STAGE_SLOTS
                if k + ahead < n_slab:
                    for cp in slab_copies(k + ahead, (k + ahead) % FFN_STAGE_SLOTS):
                        cp.start()
                for cp in slab_copies(k, slot):
                    cp.wait()
                wab_sc[:, c0:c0 + MXU_TILE] = stage_a[slot].astype(BF16)
                wab_sc[:, f + c0:f + c0 + MXU_TILE] = stage_b[slot].astype(BF16)
                wo_sc[c0:c0 + MXU_TILE, :] = stage_o[slot].astype(BF16)
            a = _dot(h_sc[...], wab_sc[:, c0:c0 + MXU_TILE])
            b = _dot(h_sc[...], wab_sc[:, f + c0:f + c0 + MXU_TILE])
            z = (a * _sigmoid(a)) * b
            part = _dot(z.astype(BF16), wo_sc[c0:c0 + MXU_TILE, :])
            acc = part if acc is None else acc + part
        r = DEEPNORM_ALPHA * x_sc[...] + m[5:6] * acc
        x_sc[...] = _layer_norm(r, lng_ref[...], lnb_ref[...])

    first = step == 0

    @pl.when(first)
    def _():
        run(True)

    @pl.when(jnp.logical_not(first))
    def _():
        run(False)

    @pl.when(step < n_first)
    def _():
        op_ref[...] = x_sc[...]

    @pl.when(step >= n_first)
    def _():
        os_ref[...] = x_sc[...]


def _ffn(xp, xs, mods, seq_s, w_in, w_out, wl, ln_g, ln_b, tm):
    f = w_out.shape[1]
    n_first = xp.shape[0] // tm
    n_second = xs.shape[0] // tm
    tiles_per_seq = seq_s // tm
    first_tile = lambda i: (jnp.minimum(i, n_first - 1), 0)
    second_tile = lambda i: (jnp.maximum(i - n_first, 0), 0)
    cond_row = lambda i: (jnp.where(i < n_first, 0, 1 + jnp.maximum(i - n_first, 0) // tiles_per_seq), 0, 0)
    return pl.pallas_call(
        functools.partial(_ffn_kernel, f=f, wl=wl, n_first=n_first),
        grid=(n_first + n_second,),
        in_specs=[pl.BlockSpec((tm, D_MODEL), first_tile),
                  pl.BlockSpec((tm, D_MODEL), second_tile),
                  pl.BlockSpec((1, 6, D_MODEL), cond_row),
                  pl.BlockSpec(memory_space=pl.ANY),
                  pl.BlockSpec(memory_space=pl.ANY),
                  pl.BlockSpec((1, D_MODEL), lambda i: (0, 0)),
                  pl.BlockSpec((1, D_MODEL), lambda i: (0, 0))],
        out_specs=[pl.BlockSpec((tm, D_MODEL), first_tile), pl.BlockSpec((tm, D_MODEL), second_tile)],
        out_shape=[jax.ShapeDtypeStruct(xp.shape, F32), jax.ShapeDtypeStruct(xs.shape, F32)],
        scratch_shapes=[pltpu.VMEM((tm, D_MODEL), F32),
                        pltpu.VMEM((tm, D_MODEL), BF16),
                        pltpu.VMEM((D_MODEL, 2 * f), BF16),
                        pltpu.VMEM((f, D_MODEL), BF16),
                        pltpu.VMEM((FFN_STAGE_SLOTS, D_MODEL, MXU_TILE), F32),
                        pltpu.VMEM((FFN_STAGE_SLOTS, D_MODEL, MXU_TILE), F32),
                        pltpu.VMEM((FFN_STAGE_SLOTS, MXU_TILE, D_MODEL), F32),
                        pltpu.SemaphoreType.DMA((3, FFN_STAGE_SLOTS))],
        compiler_params=_params(1),
        name="ffn",
    )(xp, xs, mods, w_in, w_out, ln_g.reshape(1, D_MODEL), ln_b.reshape(1, D_MODEL))


def _qkv_kernel(*refs, rope):
    if rope:
        x_ref, m_ref, w_ref, qg_ref, kg_ref, cos_ref, sin_ref = refs[:7]
    else:
        x_ref, m_ref, w_ref, qg_ref, kg_ref = refs[:5]
    q_ref, k_ref, v_ref = refs[-3:]
    rows = x_ref.shape[0]
    m = m_ref[0]
    h = x_ref[...] * (1.0 + m[1:2]) + m[0:1]
    qkv = _dot(h.astype(BF16), w_ref[...])
    if rope:
        cos = cos_ref[...]
        sin = sin_ref[...]
        lane = lax.broadcasted_iota(jnp.int32, cos.shape, 1)
        first = jnp.bitwise_and(lane, AXIS_PAIRS) == 0
    for hd in range(N_HEADS + N_KV_HEADS):
        xh = qkv[:, hd * HEAD_DIM:(hd + 1) * HEAD_DIM]
        gain = qg_ref[...] if hd < N_HEADS else kg_ref[...]
        n = xh * lax.rsqrt(jnp.mean(xh * xh, axis=-1, keepdims=True) + RMS_EPS) * gain
        if rope:
            up = pltpu.roll(n, HEAD_DIM - AXIS_PAIRS, 1)
            down = pltpu.roll(n, AXIS_PAIRS, 1)
            n = n * cos + jnp.where(first, up, down) * sin
        if hd < N_HEADS:
            q_ref[:, hd * HEAD_DIM:(hd + 1) * HEAD_DIM] = (n * ATTN_SCALE).astype(BF16)
        else:
            k_ref[pl.ds(hd - N_HEADS, rows, stride=N_KV_HEADS), :] = n
    for g in range(N_KV_HEADS):
        v_ref[pl.ds(g, rows, stride=N_KV_HEADS), :] = (
            qkv[:, D_Q + D_KV + g * HEAD_DIM:D_Q + D_KV + (g + 1) * HEAD_DIM])


def _rope_tables(seq):
    pos = jnp.arange(seq, dtype=jnp.int32)
    row = (pos // GRID_W).astype(F32)
    col = (pos % GRID_W).astype(F32)
    inv = ROPE_THETA ** (-jnp.arange(AXIS_PAIRS, dtype=F32) / AXIS_PAIRS)
    ar = row[:, None] * inv
    ac = col[:, None] * inv
    cos = jnp.concatenate([jnp.cos(ar), jnp.cos(ar), jnp.cos(ac), jnp.cos(ac)], axis=-1)
    sin = jnp.concatenate([-jnp.sin(ar), jnp.sin(ar), -jnp.sin(ac), jnp.sin(ac)], axis=-1)
    return cos, sin


def _qkv(x, mods, w_qkv, wl, q_gain, k_gain, nb, seq, tm, rope, kv_layers=1, kv_prev=None):
    nt = seq // tm
    per_batch = mods.shape[0] > 1
    in_specs = [pl.BlockSpec((tm, D_MODEL), lambda b, t: (b * nt + t, 0)),
                pl.BlockSpec((1, 6, D_MODEL), lambda b, t: (b if per_batch else 0, 0, 0)),
                pl.BlockSpec((None, D_MODEL, QKV_DIM), lambda b, t: (wl, 0, 0)),
                pl.BlockSpec((1, HEAD_DIM), lambda b, t: (0, 0)),
                pl.BlockSpec((1, HEAD_DIM), lambda b, t: (0, 0))]
    args = [x, mods, w_qkv, q_gain.reshape(1, HEAD_DIM), k_gain.reshape(1, HEAD_DIM)]
    if rope:
        cos, sin = _rope_tables(seq)
        in_specs += [pl.BlockSpec((tm, HEAD_DIM), lambda b, t: (t, 0))] * 2
        args += [cos, sin]
    n_tok = nb * seq
    kv_slot = wl if kv_layers > 1 else 0
    kv_spec = pl.BlockSpec((None, None, tm * N_KV_HEADS, HEAD_DIM), lambda b, t: (b, kv_slot, t, 0))
    kv_shape = jax.ShapeDtypeStruct((nb, kv_layers, seq * N_KV_HEADS, HEAD_DIM), F32)
    aliases = {}
    if kv_prev is not None:
        aliases = {len(args): 1, len(args) + 1: 2}
        in_specs += [pl.BlockSpec(memory_space=pl.ANY)] * 2
        args += list(kv_prev)
    return pl.pallas_call(
        functools.partial(_qkv_kernel, rope=rope),
        grid=(nb, nt),
        in_specs=in_specs,
        out_specs=[pl.BlockSpec((tm, D_Q), lambda b, t: (b * nt + t, 0)), kv_spec, kv_spec],
        out_shape=[jax.ShapeDtypeStruct((n_tok, D_Q), BF16), kv_shape, kv_shape],
        input_output_aliases=aliases,
        compiler_params=_params(2),
        name="qkv_rope" if rope else "qkv",
    )(*args)


def _attn_kernel(*refs, has_cache):
    if has_cache:
        (q_ref, k_ref, v_ref, ck_ref, cv_ref, x_ref, m_ref, wo_ref, lng_ref, lnb_ref,
         o_ref, oh_sc) = refs
    else:
        q_ref, k_ref, v_ref, x_ref, m_ref, wo_ref, lng_ref, lnb_ref, o_ref, oh_sc = refs
    for g in range(N_KV_HEADS):
        sl = slice(g * HEAD_DIM, (g + 1) * HEAD_DIM)
        head_rows = lambda ref: ref[pl.ds(g, ref.shape[0] // N_KV_HEADS, stride=N_KV_HEADS), :]
        kg = head_rows(k_ref).astype(BF16)
        vg = head_rows(v_ref).astype(BF16)
        if has_cache:
            ckg = head_rows(ck_ref).astype(BF16)
            cvg = head_rows(cv_ref).astype(BF16)
        for r in range(KV_REP):
            hsl = slice((g * KV_REP + r) * HEAD_DIM, (g * KV_REP + r + 1) * HEAD_DIM)
            qh = q_ref[:, hsl]
            s1 = _dot_nt(qh, kg)
            mx = jnp.max(s1, axis=-1, keepdims=True)
            if has_cache:
                s2 = _dot_nt(qh, ckg)
                mx = jnp.maximum(mx, jnp.max(s2, axis=-1, keepdims=True))
            p1 = jnp.exp(s1 - mx)
            den = jnp.sum(p1, axis=-1, keepdims=True)
            o = _dot(p1.astype(BF16), vg)
            if has_cache:
                p2 = jnp.exp(s2 - mx)
                den = den + jnp.sum(p2, axis=-1, keepdims=True)
                o = o + _dot(p2.astype(BF16), cvg)
            oh_sc[:, hsl] = (o / den).astype(BF16)
    mix = _dot(oh_sc[...], wo_ref[...])
    m = m_ref[0]
    r = DEEPNORM_ALPHA * x_ref[...] + m[2:3] * mix
    o_ref[...] = _layer_norm(r, lng_ref[...], lnb_ref[...])


def _attention(q, k, v, kv_slot, cache_k, cache_v, layer_j, x, mods, w_o, ln_g, ln_b, nb, seq, tq):
    nt = seq // tq
    per_batch = mods.shape[0] > 1
    has_cache = cache_k is not None
    tok = pl.BlockSpec((tq, D_MODEL), lambda b, t: (b * nt + t, 0))
    kv = pl.BlockSpec((None, None, seq * N_KV_HEADS, HEAD_DIM), lambda b, t: (b, kv_slot, 0, 0))
    in_specs, args = [tok, kv, kv], [q, k, v]
    if has_cache:
        past = cache_k.shape[2]
        cspec = pl.BlockSpec((None, None, past * N_KV_HEADS, HEAD_DIM), lambda b, t: (b, layer_j, 0, 0))
        in_specs += [cspec, cspec]
        flat = (cache_k.shape[0], cache_k.shape[1], past * N_KV_HEADS, HEAD_DIM)
        args += [cache_k.reshape(flat), cache_v.reshape(flat)]
    in_specs += [tok,
                 pl.BlockSpec((1, 6, D_MODEL), lambda b, t: (b if per_batch else 0, 0, 0)),
                 pl.BlockSpec((None, D_Q, D_MODEL), lambda b, t: (layer_j, 0, 0)),
                 pl.BlockSpec((1, D_MODEL), lambda b, t: (0, 0)),
                 pl.BlockSpec((1, D_MODEL), lambda b, t: (0, 0))]
    args += [x, mods, w_o, ln_g.reshape(1, D_MODEL), ln_b.reshape(1, D_MODEL)]
    return pl.pallas_call(
        functools.partial(_attn_kernel, has_cache=has_cache),
        grid=(nb, nt),
        in_specs=in_specs,
        out_specs=tok,
        out_shape=jax.ShapeDtypeStruct((nb * seq, D_MODEL), F32),
        scratch_shapes=[pltpu.VMEM((tq, D_Q), BF16)],
        compiler_params=_params(2),
        name="attn_cache" if has_cache else "attn",
    )(*args)


def _latent_h0(st):
    f = st[:, 0].transpose(2, 1, 0, 3)
    b = st[:, 1].transpose(2, 1, 0, 3)
    z = jnp.zeros_like(f)
    even = jnp.concatenate([f, z], axis=-1)
    odd = jnp.concatenate([z, b], axis=-1)
    h0 = jnp.stack([even, odd], axis=3)
    return h0.reshape(N_GROUPS, 2, 2 * st.shape[0], 2 * STATE_DIM)


def kernel(x_prompt, x_sample, c, cache_k, cache_v, state_s5, c_ctx, w_mod, b_mod, ln_g, ln_b, w_s5_in, s5_a_re, s5_a_im, s5_log_dt, s5_b_re, s5_b_im, s5_c_re, s5_c_im, s5_d, w_s5_glu, w_s5_out, w_qkv, q_norm_g, k_norm_g, w_o, w_ffn_in, w_ffn_out):
    nbp, seqp, _ = x_prompt.shape
    nbs, seqs, _ = x_sample.shape
    xp = x_prompt.reshape(nbp * seqp, D_MODEL)
    xs = x_sample.reshape(nbs * seqs, D_MODEL)
    s5_slots_s, s5_seq_s = 2 * nbs, seqs // 2

    cond = jnp.concatenate([c_ctx[None, :], c, jnp.zeros((8 - 1 - nbs, D_MODEL), F32)], axis=0)
    mods = _adaln(cond, w_mod, b_mod)
    mods_all = mods.reshape(DEPTH, 8, 6, D_MODEL)
    mods_p = mods[:, 0:1].reshape(DEPTH, 1, 6, D_MODEL)
    mods_s = mods[:, 1:1 + nbs].reshape(DEPTH, nbs, 6, D_MODEL)

    w_in, w_glu, w_out = w_s5_in.astype(BF16), w_s5_glu.astype(BF16), w_s5_out.astype(BF16)
    wq, wo = w_qkv.astype(BF16), w_o.astype(BF16)

    kv_rows = (nbp, DEPTH // 2, seqp * N_KV_HEADS, HEAD_DIM)
    new_kv, new_s = (jnp.zeros(kv_rows, F32), jnp.zeros(kv_rows, F32)), []
    for layer in range(DEPTH):
        j = layer // 2
        mp, ms = mods_p[layer], mods_s[layer]
        lg0, lb0, lg1, lb1 = ln_g[layer, 0], ln_b[layer, 0], ln_g[layer, 1], ln_b[layer, 1]
        if layer % 2 == 0:
            w1, w2, lam = _s5_tables(s5_a_re[j], s5_a_im[j], s5_log_dt[j], s5_b_re[j], s5_b_im[j],
                                     s5_c_re[j], s5_c_im[j])
            mp_slots = jnp.broadcast_to(mp, (nbp, 6, D_MODEL))
            ms_slots = jnp.repeat(ms, 2, axis=0)
            xp3 = xp.reshape(nbp, seqp, D_MODEL)
            xs3 = xs.reshape(s5_slots_s, s5_seq_s, D_MODEL)
            up = _s5_in(xp3, mp_slots, w_in, j, 128)
            us = _s5_in(xs3, ms_slots, w_in, j, 128)
            yp, fin = _s5_core(up, w1, w2, lam, s5_d[j], None, True)
            ys, _ = _s5_core(us, w1, w2, lam, s5_d[j], _latent_h0(state_s5[:, j]), False)
            fin = fin.reshape(N_GROUPS, 2, nbp, 2, STATE_DIM)
            new_s.append(fin.transpose(2, 3, 1, 0, 4))
            xp = _s5_out(yp, xp3, mp_slots, w_glu, w_out, j, lg0, lb0, 128).reshape(nbp * seqp, D_MODEL)
            xs = _s5_out(ys, xs3, ms_slots, w_glu, w_out, j, lg0, lb0, 128).reshape(nbs * seqs, D_MODEL)
        else:
            qp, kp, vp = _qkv(xp, mp, wq, j, q_norm_g[j], k_norm_g[j], nbp, seqp, 256, False,
                              kv_layers=DEPTH // 2, kv_prev=new_kv)
            new_kv = (kp, vp)
            qs, ks, vs = _qkv(xs, ms, wq, j, q_norm_g[j], k_norm_g[j], nbs, seqs, 512, True)
            xp = _attention(qp, kp, vp, j, None, None, j, xp, mp, wo, lg0, lb0, nbp, seqp, 256)
            xs = _attention(qs, ks, vs, 0, cache_k, cache_v, j, xs, ms, wo, lg0, lb0, nbs, seqs, 256)
        xp, xs = _ffn(xp, xs, mods_all[layer], seqs, w_ffn_in, w_ffn_out, layer, lg1, lb1, 512)

    y_prompt = xp.reshape(nbp, seqp, D_MODEL)
    y_sample = xs.reshape(nbs, seqs, D_MODEL)
    kv_out = (nbp, DEPTH // 2, seqp, N_KV_HEADS, HEAD_DIM)
    return (y_prompt, y_sample, new_kv[0].reshape(kv_out), new_kv[1].reshape(kv_out),
            jnp.stack(new_s, axis=1))
```

```python
import functools
import math

import jax
import jax.numpy as jnp
from jax import lax
from jax.experimental import pallas as pl
from jax.experimental.pallas import tpu as pltpu

F32 = jnp.float32
BF16 = jnp.bfloat16

D_MODEL = 1024
DEPTH = 4
N_GROUPS = 64
GROUP_CH = 16
STATE_DIM = 64
HEAD_DIM = 128
N_HEADS = 8
N_KV_HEADS = 2
KV_REP = N_HEADS // N_KV_HEADS
D_Q = N_HEADS * HEAD_DIM
D_KV = N_KV_HEADS * HEAD_DIM
QKV_DIM = D_Q + 2 * D_KV
GRID_W = 64
ROPE_THETA = 10000.0
AXIS_PAIRS = HEAD_DIM // 4
ATTN_SCALE = HEAD_DIM ** -0.5
DEEPNORM_ALPHA = (2.0 * DEPTH) ** 0.25
LN_EPS = 1e-6
RMS_EPS = 1e-6

V7X_VMEM_LIMIT_BYTES = 56 * 1024 * 1024
LANES = 128
SUBLANES = 8
MXU_TILE = 256
FFN_STAGE_SLOTS = 2
GROUPS_PER_BLOCK = LANES // GROUP_CH
N_GROUP_BLOCKS = N_GROUPS // GROUPS_PER_BLOCK
S5_CHUNK = MXU_TILE // GROUP_CH
CHUNK_LANES = S5_CHUNK * GROUP_CH
FOLD_UNROLL = 4
PERM_SEQS = SUBLANES
PERM_STEPS = MXU_TILE // PERM_SEQS
NT_DIMS = (((1,), (1,)), ((), ()))


def _params(n_axes):
    return pltpu.CompilerParams(dimension_semantics=("arbitrary",) * n_axes,
                                vmem_limit_bytes=V7X_VMEM_LIMIT_BYTES)


def _sigmoid(x):
    return 1.0 / (1.0 + jnp.exp(-x))


def _gelu_tanh(x):
    cdf = 0.5 * (1.0 + jnp.tanh(math.sqrt(2.0 / math.pi) * (x + 0.044715 * (x * x * x))))
    return x * cdf


def _layer_norm(r, g, b):
    mu = jnp.mean(r, axis=-1, keepdims=True)
    d = r - mu
    var = jnp.mean(d * d, axis=-1, keepdims=True)
    return d * lax.rsqrt(var + LN_EPS) * g + b


def _dot(a, b):
    return jnp.dot(a, b, preferred_element_type=F32)


def _dot_nt(a, b):
    return lax.dot_general(a, b, NT_DIMS, preferred_element_type=F32)


def _adaln_kernel(c_ref, w_ref, b_ref, o_ref):
    c = c_ref[...]
    s = c * _sigmoid(c)
    o_ref[0] = _dot(s.astype(BF16), w_ref[0].astype(BF16)) + b_ref[0]


def _adaln(cond, w_mod, b_mod):
    tn = 1536
    n = 6 * D_MODEL
    return pl.pallas_call(
        _adaln_kernel,
        grid=(DEPTH, n // tn),
        in_specs=[pl.BlockSpec((8, D_MODEL), lambda l, j: (0, 0)),
                  pl.BlockSpec((1, D_MODEL, tn), lambda l, j: (l, 0, j)),
                  pl.BlockSpec((1, 1, tn), lambda l, j: (l, 0, j))],
        out_specs=pl.BlockSpec((1, 8, tn), lambda l, j: (l, 0, j)),
        out_shape=jax.ShapeDtypeStruct((DEPTH, 8, n), F32),
        compiler_params=_params(2),
        name="adaln",
    )(cond, w_mod, b_mod.reshape(DEPTH, 1, n))


def _s5_prep_kernel(a_ref, ldt_ref, bt_ref, c_ref, w1_ref, w2_ref, lam_ref):
    half = STATE_DIM
    lane1 = lax.broadcasted_iota(jnp.int32, (1, LANES), 1)
    sgn1 = jnp.where(lane1 < half, -1.0, 1.0)
    lane_h = lax.broadcasted_iota(jnp.int32, (GROUP_CH, LANES), 1)
    first_h = lane_h < half
    conj_h = jnp.where(first_h, 1.0, -1.0)
    lane_c = lax.broadcasted_iota(jnp.int32, (GROUP_CH, CHUNK_LANES), 1)
    t = S5_CHUNK

    def cmul(pr, pi, x):
        return pr * x + (pi * sgn1) * pltpu.roll(x, half, 1)

    def pack_states(f, b):
        return (jnp.where(first_h, f, pltpu.roll(b, half, 1)),
                jnp.where(first_h, pltpu.roll(f, half, 1), b))

    def group(g, carry):
        kt, qs, cks, lam_t = [], [], [], []
        for d in range(2):
            a_re = a_ref[0, d, g]
            a_im = a_ref[1, d, g]
            dt = jnp.exp(ldt_ref[d, g])
            mag = jnp.exp(dt * a_re)
            lr = mag * jnp.cos(dt * a_im)
            li = mag * jnp.sin(dt * a_im)
            den = a_re * a_re + a_im * a_im
            nr = lr - 1.0
            k_re = (nr * a_re + li * a_im) / den
            k_im = (li * a_re - nr * a_im) / den
            bb = cmul(k_re, k_im, bt_ref[d, g])
            cc = c_ref[d, g]
            pr = jnp.ones((1, LANES), F32)
            pi = jnp.zeros((1, LANES), F32)
            ck, q = [], []
            for k in range(t + 1):
                ck.append(cmul(pr, pi, cc))
                if k < t:
                    q.append(cmul(pr, pi, bb))
                    pr, pi = pr * lr - pi * li, pr * li + pi * lr
            lam_t.append((pr, pi))
            order = range(t) if d == 0 else range(t - 1, -1, -1)
            rhs = jnp.concatenate([ck[k] for k in order], axis=0).astype(BF16)
            kt.append(_dot_nt((bb * conj_h).astype(BF16), rhs))
            qs.append(q)
            cks.append(ck)
        for j in range(t):
            rows = slice(j * GROUP_CH, (j + 1) * GROUP_CH)
            tf = kt[0] if j == 0 else pltpu.roll(kt[0], GROUP_CH * j, 1)
            tf = jnp.where(lane_c >= GROUP_CH * j, tf, 0.0)
            back = GROUP_CH * (t - 1 - j)
            tb = kt[1] if back == 0 else pltpu.roll(kt[1], CHUNK_LANES - back, 1)
            tb = jnp.where(lane_c < GROUP_CH * (j + 1), tb, 0.0)
            w1_ref[g, rows, 0:CHUNK_LANES] = (tf + tb).astype(BF16)
            s_re, s_im = pack_states(qs[0][t - 1 - j], qs[1][j])
            w1_ref[g, rows, CHUNK_LANES:CHUNK_LANES + LANES] = s_re.astype(BF16)
            w1_ref[g, rows, CHUNK_LANES + LANES:CHUNK_LANES + 2 * LANES] = s_im.astype(BF16)
            c_re, c_im = pack_states(cks[0][j + 1] * conj_h, cks[1][t - j] * conj_h)
            w2_ref[g, rows, 0:LANES] = c_re.astype(BF16)
            w2_ref[g, rows, LANES:2 * LANES] = c_im.astype(BF16)
        fwd1 = lane1 < half
        lam_ref[g, 0:1, :] = jnp.where(fwd1, lam_t[0][0], lam_t[1][0])
        lam_ref[g, 1:2, :] = jnp.where(fwd1, lam_t[0][1], lam_t[1][1])
        return carry

    lax.fori_loop(0, GROUPS_PER_BLOCK, group, 0, unroll=2)


def _s5_tables(a_re, a_im, log_dt, b_re, b_im, c_re, c_im):
    g, p, h = N_GROUPS, STATE_DIM, GROUP_CH
    dup = lambda x: jnp.concatenate([x, x], axis=-1)
    a2 = jnp.stack([dup(a_re), dup(a_im)]).reshape(2, 2, g, 1, 2 * p)
    bt = jnp.concatenate([b_re.transpose(0, 1, 3, 2), b_im.transpose(0, 1, 3, 2)], axis=-1)
    cc = jnp.concatenate([c_re, c_im], axis=-1)
    gb = GROUPS_PER_BLOCK
    return pl.pallas_call(
        _s5_prep_kernel,
        grid=(N_GROUP_BLOCKS,),
        in_specs=[pl.BlockSpec((2, 2, gb, 1, 2 * p), lambda i: (0, 0, i, 0, 0)),
                  pl.BlockSpec((2, gb, 1, 1), lambda i: (0, i, 0, 0)),
                  pl.BlockSpec((2, gb, h, 2 * p), lambda i: (0, i, 0, 0)),
                  pl.BlockSpec((2, gb, h, 2 * p), lambda i: (0, i, 0, 0))],
        out_specs=[pl.BlockSpec((gb, CHUNK_LANES, CHUNK_LANES + 2 * LANES), lambda i: (i, 0, 0)),
                   pl.BlockSpec((gb, CHUNK_LANES, 2 * LANES), lambda i: (i, 0, 0)),
                   pl.BlockSpec((gb, 2, LANES), lambda i: (i, 0, 0))],
        out_shape=[jax.ShapeDtypeStruct((g, CHUNK_LANES, CHUNK_LANES + 2 * LANES), BF16),
                   jax.ShapeDtypeStruct((g, CHUNK_LANES, 2 * LANES), BF16),
                   jax.ShapeDtypeStruct((g, 2, LANES), F32)],
        compiler_params=_params(1),
        name="s5_prep",
    )(a2, log_dt.reshape(2, g, 1, 1), bt, cc)


def _row_perm(to_time_major):
    n = PERM_SEQS * PERM_STEPS
    r = lax.broadcasted_iota(jnp.int32, (n, n), 0)
    c = lax.broadcasted_iota(jnp.int32, (n, n), 1)
    if to_time_major:
        src = jnp.bitwise_and(r, PERM_SEQS - 1) * PERM_STEPS + lax.shift_right_logical(r, 3)
    else:
        src = jnp.bitwise_and(r, PERM_STEPS - 1) * PERM_SEQS + lax.shift_right_logical(r, 5)
    return jnp.where(c == src, 1.0, 0.0).astype(BF16)


def _s5_in_kernel(x_ref, m_ref, w_ref, o_ref, *, tt):
    scale = 1.0 + m_ref[:, 1:2, :]
    shift = m_ref[:, 0:1, :]
    perm = _row_perm(True)
    n = PERM_SEQS * PERM_STEPS
    for k in range(tt // (2 * PERM_STEPS)):
        pieces = []
        for hf in range(2):
            t0 = (2 * k + hf) * PERM_STEPS
            h = x_ref[:, t0:t0 + PERM_STEPS, :] * scale + shift
            hb = h.reshape(n, D_MODEL).astype(BF16)
            pieces.append(_dot(perm, hb).astype(BF16))
        u = _dot(jnp.concatenate(pieces, axis=0), w_ref[...])
        o_ref[2 * k * PERM_STEPS:(2 * k + 2) * PERM_STEPS] = u.reshape(2 * PERM_STEPS, PERM_SEQS, D_MODEL)


def _s5_in(x, mods, w_in, wl, tt):
    slots, seq, _ = x.shape
    return pl.pallas_call(
        functools.partial(_s5_in_kernel, tt=tt),
        grid=(slots // PERM_SEQS, seq // tt),
        in_specs=[pl.BlockSpec((PERM_SEQS, tt, D_MODEL), lambda s, t: (s, t, 0)),
                  pl.BlockSpec((PERM_SEQS, 6, D_MODEL), lambda s, t: (s, 0, 0)),
                  pl.BlockSpec((None, D_MODEL, D_MODEL), lambda s, t: (wl, 0, 0))],
        out_specs=pl.BlockSpec((tt, PERM_SEQS, D_MODEL), lambda s, t: (t, s, 0)),
        out_shape=jax.ShapeDtypeStruct((seq, slots, D_MODEL), F32),
        compiler_params=_params(2),
        name="s5_in",
    )(x, mods, w_in)


def _block_transpose8(v):
    lane = lax.broadcasted_iota(jnp.int32, (SUBLANES, LANES), 1)
    v = list(v)
    for d in (4, 2, 1):
        low = jnp.bitwise_and(lane, GROUP_CH * d) == 0
        nxt = list(v)
        for i in range(8):
            if i & d == 0:
                a, b = v[i], v[i + d]
                nxt[i] = jnp.where(low, a, pltpu.roll(b, GROUP_CH * d, 1))
                nxt[i + d] = jnp.where(low, pltpu.roll(a, LANES - GROUP_CH * d, 1), b)
        v = nxt
    return v


def _s5_core_kernel(*refs, seq, nq, paired, has_fin):
    refs = list(refs)
    u_ref, w1_ref, w2_ref, lam_ref, d_ref = refs[:5]
    pos = 5
    h0_ref = None
    if paired:
        h0_ref = refs[pos]
        pos += 1
    y_ref = refs[pos]
    pos += 1
    fin_ref = refs[pos] if has_fin else None
    z_sc, yt_sc, bs_sc, sp_sc = refs[-4 - int(paired):][:4]
    sp2_sc = refs[-1] if paired else None
    gpb = GROUPS_PER_BLOCK
    nc = seq // S5_CHUNK
    ns = nq * SUBLANES
    half = STATE_DIM

    def fold(c, carry):
        for qi in range(nq):
            sl = slice(qi * SUBLANES, (qi + 1) * SUBLANES)
            r0 = pl.multiple_of(c * ns + qi * SUBLANES, SUBLANES)
            for hf in range(2):
                v = [u_ref[c * S5_CHUNK + hf * 8 + t, sl, :] for t in range(8)]
                w = _block_transpose8(v)
                for g in range(gpb):
                    z_sc[g, pl.ds(r0, SUBLANES), hf * LANES:(hf + 1) * LANES] = w[g]
        return carry

    lax.fori_loop(0, nc, fold, 0, unroll=FOLD_UNROLL // nq)

    for g in range(gpb):
        m1 = _dot(z_sc[g].astype(BF16), w1_ref[g])
        yt_sc[g] = m1[:, 0:CHUNK_LANES]
        bs_sc[g] = m1[:, CHUNK_LANES:]

    lane = lax.broadcasted_iota(jnp.int32, (SUBLANES, LANES), 1)
    fwd = jnp.bitwise_and(lane, half) == 0
    lam = [(jnp.broadcast_to(lam_ref[g, 0:1, :], (SUBLANES, LANES)),
            jnp.broadcast_to(lam_ref[g, 1:2, :], (SUBLANES, LANES))) for g in range(gpb)]

    def run_pass(init, dst):
        def body(i, carry):
            out = []
            for g in range(gpb):
                l_re, l_im = lam[g]
                for qi in range(nq):
                    s_re, s_im = carry[2 * (g * nq + qi)], carry[2 * (g * nq + qi) + 1]
                    ri = pl.ds(pl.multiple_of(i * ns + qi * SUBLANES, SUBLANES), SUBLANES)
                    rr = pl.ds(pl.multiple_of((nc - 1 - i) * ns + qi * SUBLANES, SUBLANES), SUBLANES)
                    dst[g, ri, 0:half] = s_re[:, 0:half]
                    dst[g, ri, LANES:LANES + half] = s_im[:, 0:half]
                    dst[g, rr, half:LANES] = s_re[:, half:LANES]
                    dst[g, rr, LANES + half:2 * LANES] = s_im[:, half:LANES]
                    x_re = jnp.where(fwd, bs_sc[g, ri, 0:LANES], bs_sc[g, rr, 0:LANES])
                    x_im = jnp.where(fwd, bs_sc[g, ri, LANES:2 * LANES], bs_sc[g, rr, LANES:2 * LANES])
                    out.append(l_re * s_re - l_im * s_im + x_re)
                    out.append(l_re * s_im + l_im * s_re + x_im)
            return tuple(out)

        return lax.fori_loop(0, nc, body, init)

    if paired:
        init = []
        for g in range(gpb):
            init += [h0_ref[g, 0], h0_ref[g, 1]]
        mid = run_pass(tuple(init), sp_sc)
        handed = [jnp.where(fwd, pltpu.roll(s, 1, 0), pltpu.roll(s, SUBLANES - 1, 0)) for s in mid]
        fin = run_pass(tuple(handed), sp2_sc)
    else:
        zero = jnp.zeros((SUBLANES, LANES), F32)
        fin = run_pass((zero,) * (2 * gpb * nq), sp_sc)

    if has_fin:
        for g in range(gpb):
            for qi in range(nq):
                sl = slice(qi * SUBLANES, (qi + 1) * SUBLANES)
                fin_ref[g, 0, sl, :] = fin[2 * (g * nq + qi)]
                fin_ref[g, 1, sl, :] = fin[2 * (g * nq + qi) + 1]

    if paired:
        shape = (nc * ns, 2 * LANES)
        row = lax.broadcasted_iota(jnp.int32, shape, 0)
        col = lax.broadcasted_iota(jnp.int32, shape, 1)
        first_pass = (jnp.bitwise_and(row, 1) == 0) == (jnp.bitwise_and(col, half) == 0)
    for g in range(gpb):
        states = sp_sc[g]
        if paired:
            states = jnp.where(first_pass, states, sp2_sc[g])
        yt_sc[g] = yt_sc[g] + _dot_nt(states.astype(BF16), w2_ref[g])

    d = jnp.broadcast_to(d_ref[...], (SUBLANES, LANES))

    def unfold(c, carry):
        for qi in range(nq):
            sl = slice(qi * SUBLANES, (qi + 1) * SUBLANES)
            r0 = pl.multiple_of(c * ns + qi * SUBLANES, SUBLANES)
            for hf in range(2):
                w = [yt_sc[g, pl.ds(r0, SUBLANES), hf * LANES:(hf + 1) * LANES] for g in range(gpb)]
                v = _block_transpose8(w)
                for t in range(8):
                    step = c * S5_CHUNK + hf * 8 + t
                    y_ref[step, sl, :] = v[t] + d * u_ref[step, sl, :]
        return carry

    lax.fori_loop(0, nc, unfold, 0, unroll=FOLD_UNROLL // nq)


def _s5_core(u, w1, w2, lam, d_skip, h0, want_final):
    seq, slots, _ = u.shape
    paired = h0 is not None
    nq = 1 if paired else slots // SUBLANES
    ns = nq * SUBLANES
    gpb = GROUPS_PER_BLOCK
    rows = (seq // S5_CHUNK) * ns
    u_spec = pl.BlockSpec((seq, ns, LANES), lambda s, g: (0, s, g))
    in_specs = [u_spec,
                pl.BlockSpec((gpb, CHUNK_LANES, CHUNK_LANES + 2 * LANES), lambda s, g: (g, 0, 0)),
                pl.BlockSpec((gpb, CHUNK_LANES, 2 * LANES), lambda s, g: (g, 0, 0)),
                pl.BlockSpec((gpb, 2, LANES), lambda s, g: (g, 0, 0)),
                pl.BlockSpec((1, LANES), lambda s, g: (0, g))]
    args = [u, w1, w2, lam, d_skip.reshape(1, D_MODEL)]
    if paired:
        in_specs.append(pl.BlockSpec((gpb, 2, SUBLANES, LANES), lambda s, g: (g, 0, s, 0)))
        args.append(h0)
    out_specs = [u_spec]
    out_shape = [jax.ShapeDtypeStruct(u.shape, F32)]
    if want_final:
        out_specs.append(pl.BlockSpec((gpb, 2, ns, LANES), lambda s, g: (g, 0, s, 0)))
        out_shape.append(jax.ShapeDtypeStruct((N_GROUPS, 2, slots, LANES), F32))
    res = pl.pallas_call(
        functools.partial(_s5_core_kernel, seq=seq, nq=nq, paired=paired, has_fin=want_final),
        grid=(slots // ns, N_GROUP_BLOCKS),
        in_specs=in_specs,
        out_specs=out_specs,
        out_shape=out_shape,
        scratch_shapes=[pltpu.VMEM((gpb, rows, CHUNK_LANES), F32) for _ in range(5 if paired else 4)],
        compiler_params=_params(2),
        name="s5_core",
    )(*args)
    return (res[0], res[1]) if want_final else (res[0], None)


def _s5_out_kernel(y_ref, x_ref, m_ref, wab_ref, wo_ref, lng_ref, lnb_ref, o_ref, *, tt, f):
    gate = m_ref[:, 2:3, :]
    perm = _row_perm(False)
    n = PERM_SEQS * PERM_STEPS
    lng = lng_ref[...].reshape(1, 1, D_MODEL)
    lnb = lnb_ref[...].reshape(1, 1, D_MODEL)
    for k in range(tt // (2 * PERM_STEPS)):
        t0 = 2 * k * PERM_STEPS
        hb = _gelu_tanh(y_ref[t0:t0 + 2 * PERM_STEPS].reshape(2 * n, D_MODEL)).astype(BF16)
        acc = None
        for c0 in range(0, f, MXU_TILE):
            val = _dot(hb, wab_ref[:, c0:c0 + MXU_TILE])
            gte = _dot(hb, wab_ref[:, f + c0:f + c0 + MXU_TILE])
            z = (val * _sigmoid(gte)).astype(BF16)
            zp = jnp.concatenate([_dot(perm, z[0:n]).astype(BF16), _dot(perm, z[n:2 * n]).astype(BF16)],
                                 axis=0)
            part = _dot(zp, wo_ref[c0:c0 + MXU_TILE, :])
            acc = part if acc is None else acc + part
        for hf in range(2):
            ts = slice(t0 + hf * PERM_STEPS, t0 + (hf + 1) * PERM_STEPS)
            mix = acc[hf * n:(hf + 1) * n].reshape(PERM_SEQS, PERM_STEPS, D_MODEL)
            r = DEEPNORM_ALPHA * x_ref[:, ts, :] + gate * mix
            o_ref[:, ts, :] = _layer_norm(r, lng, lnb)


def _s5_out(y, x, mods, w_glu, w_out, wl, ln_g, ln_b, tt):
    slots, seq, _ = x.shape
    f = w_out.shape[1]
    resident = pl.Buffered(1)
    tok = pl.BlockSpec((PERM_SEQS, tt, D_MODEL), lambda s, t: (s, t, 0))
    return pl.pallas_call(
        functools.partial(_s5_out_kernel, tt=tt, f=f),
        grid=(slots // PERM_SEQS, seq // tt),
        in_specs=[pl.BlockSpec((tt, PERM_SEQS, D_MODEL), lambda s, t: (t, s, 0)),
                  tok,
                  pl.BlockSpec((PERM_SEQS, 6, D_MODEL), lambda s, t: (s, 0, 0)),
                  pl.BlockSpec((None, D_MODEL, 2 * f), lambda s, t: (wl, 0, 0), pipeline_mode=resident),
                  pl.BlockSpec((None, f, D_MODEL), lambda s, t: (wl, 0, 0), pipeline_mode=resident),
                  pl.BlockSpec((1, D_MODEL), lambda s, t: (0, 0)),
                  pl.BlockSpec((1, D_MODEL), lambda s, t: (0, 0))],
        out_specs=tok,
        out_shape=jax.ShapeDtypeStruct(x.shape, F32),
        compiler_params=_params(2),
        name="s5_out",
    )(y, x, mods, w_glu, w_out, ln_g.reshape(1, D_MODEL), ln_b.reshape(1, D_MODEL))


def _ffn_kernel(xp_ref, xs_ref, m_ref, wi_hbm, wo_hbm, lng_ref, lnb_ref, op_ref, os_ref,
                h_sc, wab_sc, wo_sc, stage_a, stage_b, stage_o, sem, *, f, wl, n_first):
    n_slab = f // MXU_TILE
    step = pl.program_id(0)
    m = m_ref[0]

    def slab_copies(k, slot):
        c0 = k * MXU_TILE
        return (pltpu.make_async_copy(wi_hbm.at[wl, :, pl.ds(c0, MXU_TILE)], stage_a.at[slot], sem.at[0, slot]),
                pltpu.make_async_copy(wi_hbm.at[wl, :, pl.ds(f + c0, MXU_TILE)], stage_b.at[slot], sem.at[1, slot]),
                pltpu.make_async_copy(wo_hbm.at[wl, pl.ds(c0, MXU_TILE), :], stage_o.at[slot], sem.at[2, slot]))

    def run(load_weights, x_ref, o_ref):
        h = x_ref[...] * (1.0 + m[4:5]) + m[3:4]
        h_sc[...] = h.astype(BF16)
        ahead = FFN_STAGE_SLOTS - 1
        if load_weights:
            for k in range(min(ahead, n_slab)):
                for cp in slab_copies(k, k % FFN_STAGE_SLOTS):
                    cp.start()
        acc = None
        for k in range(n_slab):
            c0 = k * MXU_TILE
            if load_weights:
                slot = k % FFN_STAGE_SLOTS
                if k + ahead < n_slab:
                    for cp in slab_copies(k + ahead, (k + ahead) % FFN_STAGE_SLOTS):
                        cp.start()
                for cp in slab_copies(k, slot):
                    cp.wait()
                wab_sc[:, c0:c0 + MXU_TILE] = stage_a[slot].astype(BF16)
                wab_sc[:, f + c0:f + c0 + MXU_TILE] = stage_b[slot].astype(BF16)
                wo_sc[c0:c0 + MXU_TILE, :] = stage_o[slot].astype(BF16)
            a = _dot(h_sc[...], wab_sc[:, c0:c0 + MXU_TILE])
            b = _dot(h_sc[...], wab_sc[:, f + c0:f + c0 + MXU_TILE])
            z = (a * _sigmoid(a)) * b
            part = _dot(z.astype(BF16), wo_sc[c0:c0 + MXU_TILE, :])
            acc = part if acc is None else acc + part
        r = DEEPNORM_ALPHA * x_ref[...] + m[5:6] * acc
        o_ref[...] = _layer_norm(r, lng_ref[...], lnb_ref[...])

    @pl.when(step == 0)
    def _():
        run(True, xp_ref, op_ref)

    @pl.when(jnp.logical_and(step > 0, step < n_first))
    def _():
        run(False, xp_ref, op_ref)

    @pl.when(step >= n_first)
    def _():
        run(False, xs_ref, os_ref)


def _ffn(xp, xs, mods, seq_s, w_in, w_out, wl, ln_g, ln_b, tm):
    f = w_out.shape[1]
    n_first = xp.shape[0] // tm
    n_second = xs.shape[0] // tm
    tiles_per_seq = seq_s // tm
    first_tile = lambda i: (jnp.minimum(i, n_first - 1), 0)
    second_tile = lambda i: (jnp.maximum(i - n_first, 0), 0)
    cond_row = lambda i: (jnp.where(i < n_first, 0, 1 + jnp.maximum(i - n_first, 0) // tiles_per_seq), 0, 0)
    return pl.pallas_call(
        functools.partial(_ffn_kernel, f=f, wl=wl, n_first=n_first),
        grid=(n_first + n_second,),
        in_specs=[pl.BlockSpec((tm, D_MODEL), first_tile),
                  pl.BlockSpec((tm, D_MODEL), second_tile),
                  pl.BlockSpec((1, 6, D_MODEL), cond_row),
                  pl.BlockSpec(memory_space=pl.ANY),
                  pl.BlockSpec(memory_space=pl.ANY),
                  pl.BlockSpec((1, D_MODEL), lambda i: (0, 0)),
                  pl.BlockSpec((1, D_MODEL), lambda i: (0, 0))],
        out_specs=[pl.BlockSpec((tm, D_MODEL), first_tile), pl.BlockSpec((tm, D_MODEL), second_tile)],
        out_shape=[jax.ShapeDtypeStruct(xp.shape, F32), jax.ShapeDtypeStruct(xs.shape, F32)],
        scratch_shapes=[pltpu.VMEM((tm, D_MODEL), BF16),
                        pltpu.VMEM((D_MODEL, 2 * f), BF16),
                        pltpu.VMEM((f, D_MODEL), BF16),
                        pltpu.VMEM((FFN_STAGE_SLOTS, D_MODEL, MXU_TILE), F32),
                        pltpu.VMEM((FFN_STAGE_SLOTS, D_MODEL, MXU_TILE), F32),
                        pltpu.VMEM((FFN_STAGE_SLOTS, MXU_TILE, D_MODEL), F32),
                        pltpu.SemaphoreType.DMA((3, FFN_STAGE_SLOTS))],
        compiler_params=_params(1),
        name="ffn",
    )(xp, xs, mods, w_in, w_out, ln_g.reshape(1, D_MODEL), ln_b.reshape(1, D_MODEL))


def _qkv_kernel(*refs, rope):
    if rope:
        x_ref, m_ref, w_ref, qg_ref, kg_ref, cos_ref, sin_ref = refs[:7]
    else:
        x_ref, m_ref, w_ref, qg_ref, kg_ref = refs[:5]
    q_ref, k_ref, v_ref = refs[-3:]
    rows = x_ref.shape[0]
    m = m_ref[0]
    h = x_ref[...] * (1.0 + m[1:2]) + m[0:1]
    qkv = _dot(h.astype(BF16), w_ref[...])
    if rope:
        cos = cos_ref[...]
        sin = sin_ref[...]
        lane = lax.broadcasted_iota(jnp.int32, cos.shape, 1)
        first = jnp.bitwise_and(lane, AXIS_PAIRS) == 0
    for hd in range(N_HEADS + N_KV_HEADS):
        xh = qkv[:, hd * HEAD_DIM:(hd + 1) * HEAD_DIM]
        gain = qg_ref[...] if hd < N_HEADS else kg_ref[...]
        n = xh * lax.rsqrt(jnp.mean(xh * xh, axis=-1, keepdims=True) + RMS_EPS) * gain
        if rope:
            up = pltpu.roll(n, HEAD_DIM - AXIS_PAIRS, 1)
            down = pltpu.roll(n, AXIS_PAIRS, 1)
            n = n * cos + jnp.where(first, up, down) * sin
        if hd < N_HEADS:
            q_ref[:, hd * HEAD_DIM:(hd + 1) * HEAD_DIM] = (n * ATTN_SCALE).astype(BF16)
        else:
            k_ref[pl.ds(hd - N_HEADS, rows, stride=N_KV_HEADS), :] = n
    for g in range(N_KV_HEADS):
        v_ref[pl.ds(g, rows, stride=N_KV_HEADS), :] = (
            qkv[:, D_Q + D_KV + g * HEAD_DIM:D_Q + D_KV + (g + 1) * HEAD_DIM])


def _rope_tables(seq):
    pos = jnp.arange(seq, dtype=jnp.int32)
    row = (pos // GRID_W).astype(F32)
    col = (pos % GRID_W).astype(F32)
    inv = ROPE_THETA ** (-jnp.arange(AXIS_PAIRS, dtype=F32) / AXIS_PAIRS)
    ar = row[:, None] * inv
    ac = col[:, None] * inv
    cos = jnp.concatenate([jnp.cos(ar), jnp.cos(ar), jnp.cos(ac), jnp.cos(ac)], axis=-1)
    sin = jnp.concatenate([-jnp.sin(ar), jnp.sin(ar), -jnp.sin(ac), jnp.sin(ac)], axis=-1)
    return cos, sin


def _qkv(x, mods, w_qkv, wl, q_gain, k_gain, nb, seq, tm, rope, kv_layers=1, kv_prev=None):
    nt = seq // tm
    per_batch = mods.shape[0] > 1
    in_specs = [pl.BlockSpec((tm, D_MODEL), lambda b, t: (b * nt + t, 0)),
                pl.BlockSpec((1, 6, D_MODEL), lambda b, t: (b if per_batch else 0, 0, 0)),
                pl.BlockSpec((None, D_MODEL, QKV_DIM), lambda b, t: (wl, 0, 0)),
                pl.BlockSpec((1, HEAD_DIM), lambda b, t: (0, 0)),
                pl.BlockSpec((1, HEAD_DIM), lambda b, t: (0, 0))]
    args = [x, mods, w_qkv, q_gain.reshape(1, HEAD_DIM), k_gain.reshape(1, HEAD_DIM)]
    if rope:
        cos, sin = _rope_tables(seq)
        in_specs += [pl.BlockSpec((tm, HEAD_DIM), lambda b, t: (t, 0))] * 2
        args += [cos, sin]
    n_tok = nb * seq
    kv_slot = wl if kv_layers > 1 else 0
    kv_spec = pl.BlockSpec((None, None, tm * N_KV_HEADS, HEAD_DIM), lambda b, t: (b, kv_slot, t, 0))
    kv_shape = jax.ShapeDtypeStruct((nb, kv_layers, seq * N_KV_HEADS, HEAD_DIM), F32)
    aliases = {}
    if kv_prev is not None:
        aliases = {len(args): 1, len(args) + 1: 2}
        in_specs += [pl.BlockSpec(memory_space=pl.ANY)] * 2
        args += list(kv_prev)
    return pl.pallas_call(
        functools.partial(_qkv_kernel, rope=rope),
        grid=(nb, nt),
        in_specs=in_specs,
        out_specs=[pl.BlockSpec((tm, D_Q), lambda b, t: (b * nt + t, 0)), kv_spec, kv_spec],
        out_shape=[jax.ShapeDtypeStruct((n_tok, D_Q), BF16), kv_shape, kv_shape],
        input_output_aliases=aliases,
        compiler_params=_params(2),
        name="qkv_rope" if rope else "qkv",
    )(*args)


def _attn_kernel(*refs, has_cache):
    if has_cache:
        (q_ref, k_ref, v_ref, ck_ref, cv_ref, x_ref, m_ref, wo_ref, lng_ref, lnb_ref,
         o_ref, oh_sc) = refs
    else:
        q_ref, k_ref, v_ref, x_ref, m_ref, wo_ref, lng_ref, lnb_ref, o_ref, oh_sc = refs
    for g in range(N_KV_HEADS):
        sl = slice(g * HEAD_DIM, (g + 1) * HEAD_DIM)
        head_rows = lambda ref: ref[pl.ds(g, ref.shape[0] // N_KV_HEADS, stride=N_KV_HEADS), :]
        kg = head_rows(k_ref).astype(BF16)
        vg = head_rows(v_ref).astype(BF16)
        if has_cache:
            ckg = head_rows(ck_ref).astype(BF16)
            cvg = head_rows(cv_ref).astype(BF16)
        for r in range(KV_REP):
            hsl = slice((g * KV_REP + r) * HEAD_DIM, (g * KV_REP + r + 1) * HEAD_DIM)
            qh = q_ref[:, hsl]
            s1 = _dot_nt(qh, kg)
            mx = jnp.max(s1, axis=-1, keepdims=True)
            if has_cache:
                s2 = _dot_nt(qh, ckg)
                mx = jnp.maximum(mx, jnp.max(s2, axis=-1, keepdims=True))
            p1 = jnp.exp(s1 - mx)
            den = jnp.sum(p1, axis=-1, keepdims=True)
            o = _dot(p1.astype(BF16), vg)
            if has_cache:
                p2 = jnp.exp(s2 - mx)
                den = den + jnp.sum(p2, axis=-1, keepdims=True)
                o = o + _dot(p2.astype(BF16), cvg)
            oh_sc[:, hsl] = (o / den).astype(BF16)
    mix = _dot(oh_sc[...], wo_ref[...])
    m = m_ref[0]
    r = DEEPNORM_ALPHA * x_ref[...] + m[2:3] * mix
    o_ref[...] = _layer_norm(r, lng_ref[...], lnb_ref[...])


def _attention(q, k, v, kv_slot, cache_k, cache_v, layer_j, x, mods, w_o, ln_g, ln_b, nb, seq, tq):
    nt = seq // tq
    per_batch = mods.shape[0] > 1
    has_cache = cache_k is not None
    tok = pl.BlockSpec((tq, D_MODEL), lambda b, t: (b * nt + t, 0))
    kv = pl.BlockSpec((None, None, seq * N_KV_HEADS, HEAD_DIM), lambda b, t: (b, kv_slot, 0, 0))
    in_specs, args = [tok, kv, kv], [q, k, v]
    if has_cache:
        past = cache_k.shape[2]
        cspec = pl.BlockSpec((None, None, past * N_KV_HEADS, HEAD_DIM), lambda b, t: (b, layer_j, 0, 0))
        in_specs += [cspec, cspec]
        flat = (cache_k.shape[0], cache_k.shape[1], past * N_KV_HEADS, HEAD_DIM)
        args += [cache_k.reshape(flat), cache_v.reshape(flat)]
    in_specs += [tok,
                 pl.BlockSpec((1, 6, D_MODEL), lambda b, t: (b if per_batch else 0, 0, 0)),
                 pl.BlockSpec((None, D_Q, D_MODEL), lambda b, t: (layer_j, 0, 0)),
                 pl.BlockSpec((1, D_MODEL), lambda b, t: (0, 0)),
                 pl.BlockSpec((1, D_MODEL), lambda b, t: (0, 0))]
    args += [x, mods, w_o, ln_g.reshape(1, D_MODEL), ln_b.reshape(1, D_MODEL)]
    return pl.pallas_call(
        functools.partial(_attn_kernel, has_cache=has_cache),
        grid=(nb, nt),
        in_specs=in_specs,
        out_specs=tok,
        out_shape=jax.ShapeDtypeStruct((nb * seq, D_MODEL), F32),
        scratch_shapes=[pltpu.VMEM((tq, D_Q), BF16)],
        compiler_params=_params(2),
        name="attn_cache" if has_cache else "attn",
    )(*args)


def _latent_h0(st):
    f = st[:, 0].transpose(2, 1, 0, 3)
    b = st[:, 1].transpose(2, 1, 0, 3)
    z = jnp.zeros_like(f)
    even = jnp.concatenate([f, z], axis=-1)
    odd = jnp.concatenate([z, b], axis=-1)
    h0 = jnp.stack([even, odd], axis=3)
    return h0.reshape(N_GROUPS, 2, 2 * st.shape[0], 2 * STATE_DIM)


def kernel(x_prompt, x_sample, c, cache_k, cache_v, state_s5, c_ctx, w_mod, b_mod, ln_g, ln_b, w_s5_in, s5_a_re, s5_a_im, s5_log_dt, s5_b_re, s5_b_im, s5_c_re, s5_c_im, s5_d, w_s5_glu, w_s5_out, w_qkv, q_norm_g, k_norm_g, w_o, w_ffn_in, w_ffn_out):
    nbp, seqp, _ = x_prompt.shape
    nbs, seqs, _ = x_sample.shape
    xp = x_prompt.reshape(nbp * seqp, D_MODEL)
    xs = x_sample.reshape(nbs * seqs, D_MODEL)
    s5_slots_s, s5_seq_s = 2 * nbs, seqs // 2

    cond = jnp.concatenate([c_ctx[None, :], c, jnp.zeros((8 - 1 - nbs, D_MODEL), F32)], axis=0)
    mods = _adaln(cond, w_mod, b_mod)
    mods_all = mods.reshape(DEPTH, 8, 6, D_MODEL)
    mods_p = mods[:, 0:1].reshape(DEPTH, 1, 6, D_MODEL)
    mods_s = mods[:, 1:1 + nbs].reshape(DEPTH, nbs, 6, D_MODEL)

    w_in, w_glu, w_out = w_s5_in.astype(BF16), w_s5_glu.astype(BF16), w_s5_out.astype(BF16)
    wq, wo = w_qkv.astype(BF16), w_o.astype(BF16)

    kv_rows = (nbp, DEPTH // 2, seqp * N_KV_HEADS, HEAD_DIM)
    new_kv, new_s = (jnp.zeros(kv_rows, F32), jnp.zeros(kv_rows, F32)), []
    for layer in range(DEPTH):
        j = layer // 2
        mp, ms = mods_p[layer], mods_s[layer]
        lg0, lb0, lg1, lb1 = ln_g[layer, 0], ln_b[layer, 0], ln_g[layer, 1], ln_b[layer, 1]
        if layer % 2 == 0:
            w1, w2, lam = _s5_tables(s5_a_re[j], s5_a_im[j], s5_log_dt[j], s5_b_re[j], s5_b_im[j],
                                     s5_c_re[j], s5_c_im[j])
            mp_slots = jnp.broadcast_to(mp, (nbp, 6, D_MODEL))
            ms_slots = jnp.repeat(ms, 2, axis=0)
            xp3 = xp.reshape(nbp, seqp, D_MODEL)
            xs3 = xs.reshape(s5_slots_s, s5_seq_s, D_MODEL)
            up = _s5_in(xp3, mp_slots, w_in, j, 128)
            us = _s5_in(xs3, ms_slots, w_in, j, 128)
            yp, fin = _s5_core(up, w1, w2, lam, s5_d[j], None, True)
            ys, _ = _s5_core(us, w1, w2, lam, s5_d[j], _latent_h0(state_s5[:, j]), False)
            fin = fin.reshape(N_GROUPS, 2, nbp, 2, STATE_DIM)
            new_s.append(fin.transpose(2, 3, 1, 0, 4))
            xp = _s5_out(yp, xp3, mp_slots, w_glu, w_out, j, lg0, lb0, 128).reshape(nbp * seqp, D_MODEL)
            xs = _s5_out(ys, xs3, ms_slots, w_glu, w_out, j, lg0, lb0, 128).reshape(nbs * seqs, D_MODEL)
        else:
            qp, kp, vp = _qkv(xp, mp, wq, j, q_norm_g[j], k_norm_g[j], nbp, seqp, 256, False,
                              kv_layers=DEPTH // 2, kv_prev=new_kv)
            new_kv = (kp, vp)
            qs, ks, vs = _qkv(xs, ms, wq, j, q_norm_g[j], k_norm_g[j], nbs, seqs, 512, True)
            xp = _attention(qp, kp, vp, j, None, None, j, xp, mp, wo, lg0, lb0, nbp, seqp, 256)
            xs = _attention(qs, ks, vs, 0, cache_k, cache_v, j, xs, ms, wo, lg0, lb0, nbs, seqs, 256)
        xp, xs = _ffn(xp, xs, mods_all[layer], seqs, w_ffn_in, w_ffn_out, layer, lg1, lb1, 512)

    y_prompt = xp.reshape(nbp, seqp, D_MODEL)
    y_sample = xs.reshape(nbs, seqs, D_MODEL)
    kv_out = (nbp, DEPTH // 2, seqp, N_KV_HEADS, HEAD_DIM)
    return (y_prompt, y_sample, new_kv[0].reshape(kv_out), new_kv[1].reshape(kv_out),
            jnp.stack(new_s, axis=1))
```

```python
import functools
import math

import jax
import jax.numpy as jnp
from jax import lax
from jax.experimental import pallas as pl
from jax.experimental.pallas import tpu as pltpu

F32 = jnp.float32
BF16 = jnp.bfloat16

D_MODEL = 1024
DEPTH = 4
N_GROUPS = 64
GROUP_CH = 16
STATE_DIM = 64
HEAD_DIM = 128
N_HEADS = 8
N_KV_HEADS = 2
KV_REP = N_HEADS // N_KV_HEADS
D_Q = N_HEADS * HEAD_DIM
D_KV = N_KV_HEADS * HEAD_DIM
QKV_DIM = D_Q + 2 * D_KV
GRID_W = 64
ROPE_THETA = 10000.0
AXIS_PAIRS = HEAD_DIM // 4
ATTN_SCALE = HEAD_DIM ** -0.5
Q_SCALE_LOG2 = ATTN_SCALE * math.log2(math.e)
DEEPNORM_ALPHA = (2.0 * DEPTH) ** 0.25
LN_EPS = 1e-6
RMS_EPS = 1e-6

V7X_VMEM_LIMIT_BYTES = 56 * 1024 * 1024
LANES = 128
SUBLANES = 8
MXU_TILE = 256
FFN_STAGE_SLOTS = 2
GROUPS_PER_BLOCK = LANES // GROUP_CH
N_GROUP_BLOCKS = N_GROUPS // GROUPS_PER_BLOCK
S5_CHUNK = MXU_TILE // GROUP_CH
CHUNK_LANES = S5_CHUNK * GROUP_CH
FOLD_UNROLL = 4
PERM_SEQS = SUBLANES
PERM_STEPS = MXU_TILE // PERM_SEQS
NT_DIMS = (((1,), (1,)), ((), ()))


def _params(n_axes):
    return pltpu.CompilerParams(dimension_semantics=("arbitrary",) * n_axes,
                                vmem_limit_bytes=V7X_VMEM_LIMIT_BYTES)


def _sigmoid(x):
    return 1.0 / (1.0 + jnp.exp(-x))


def _gelu_tanh(x):
    cdf = 0.5 * (1.0 + jnp.tanh(math.sqrt(2.0 / math.pi) * (x + 0.044715 * (x * x * x))))
    return x * cdf


def _layer_norm(r, g, b):
    mu = jnp.mean(r, axis=-1, keepdims=True)
    d = r - mu
    var = jnp.mean(d * d, axis=-1, keepdims=True)
    return d * lax.rsqrt(var + LN_EPS) * g + b


def _dot(a, b):
    return jnp.dot(a, b, preferred_element_type=F32)


def _dot_nt(a, b):
    return lax.dot_general(a, b, NT_DIMS, preferred_element_type=F32)


def _adaln_kernel(c_ref, w_ref, b_ref, o_ref):
    c = c_ref[...]
    s = c * _sigmoid(c)
    o_ref[0] = _dot(s.astype(BF16), w_ref[0].astype(BF16)) + b_ref[0]


def _adaln(cond, w_mod, b_mod):
    tn = 1536
    n = 6 * D_MODEL
    return pl.pallas_call(
        _adaln_kernel,
        grid=(DEPTH, n // tn),
        in_specs=[pl.BlockSpec((8, D_MODEL), lambda l, j: (0, 0)),
                  pl.BlockSpec((1, D_MODEL, tn), lambda l, j: (l, 0, j)),
                  pl.BlockSpec((1, 1, tn), lambda l, j: (l, 0, j))],
        out_specs=pl.BlockSpec((1, 8, tn), lambda l, j: (l, 0, j)),
        out_shape=jax.ShapeDtypeStruct((DEPTH, 8, n), F32),
        compiler_params=_params(2),
        name="adaln",
    )(cond, w_mod, b_mod.reshape(DEPTH, 1, n))


def _s5_prep_kernel(a_ref, ldt_ref, bt_ref, c_ref, w1_ref, w2_ref, lam_ref):
    half = STATE_DIM
    lane1 = lax.broadcasted_iota(jnp.int32, (1, LANES), 1)
    sgn1 = jnp.where(lane1 < half, -1.0, 1.0)
    lane_h = lax.broadcasted_iota(jnp.int32, (GROUP_CH, LANES), 1)
    first_h = lane_h < half
    conj_h = jnp.where(first_h, 1.0, -1.0)
    lane_c = lax.broadcasted_iota(jnp.int32, (GROUP_CH, CHUNK_LANES), 1)
    t = S5_CHUNK

    def cmul(pr, pi, x):
        return pr * x + (pi * sgn1) * pltpu.roll(x, half, 1)

    def pack_states(f, b):
        return (jnp.where(first_h, f, pltpu.roll(b, half, 1)),
                jnp.where(first_h, pltpu.roll(f, half, 1), b))

    def group(g, carry):
        kt, qs, cks, lam_t = [], [], [], []
        for d in range(2):
            a_re = a_ref[0, d, g]
            a_im = a_ref[1, d, g]
            dt = jnp.exp(ldt_ref[d, g])
            mag = jnp.exp(dt * a_re)
            lr = mag * jnp.cos(dt * a_im)
            li = mag * jnp.sin(dt * a_im)
            den = a_re * a_re + a_im * a_im
            nr = lr - 1.0
            k_re = (nr * a_re + li * a_im) / den
            k_im = (li * a_re - nr * a_im) / den
            bb = cmul(k_re, k_im, bt_ref[d, g])
            cc = c_ref[d, g]
            pr = jnp.ones((1, LANES), F32)
            pi = jnp.zeros((1, LANES), F32)
            ck, q = [], []
            for k in range(t + 1):
                ck.append(cmul(pr, pi, cc))
                if k < t:
                    q.append(cmul(pr, pi, bb))
                    pr, pi = pr * lr - pi * li, pr * li + pi * lr
            lam_t.append((pr, pi))
            order = range(t) if d == 0 else range(t - 1, -1, -1)
            rhs = jnp.concatenate([ck[k] for k in order], axis=0).astype(BF16)
            kt.append(_dot_nt((bb * conj_h).astype(BF16), rhs))
            qs.append(q)
            cks.append(ck)
        for j in range(t):
            rows = slice(j * GROUP_CH, (j + 1) * GROUP_CH)
            tf = kt[0] if j == 0 else pltpu.roll(kt[0], GROUP_CH * j, 1)
            tf = jnp.where(lane_c >= GROUP_CH * j, tf, 0.0)
            back = GROUP_CH * (t - 1 - j)
            tb = kt[1] if back == 0 else pltpu.roll(kt[1], CHUNK_LANES - back, 1)
            tb = jnp.where(lane_c < GROUP_CH * (j + 1), tb, 0.0)
            w1_ref[g, rows, 0:CHUNK_LANES] = (tf + tb).astype(BF16)
            s_re, s_im = pack_states(qs[0][t - 1 - j], qs[1][j])
            w1_ref[g, rows, CHUNK_LANES:CHUNK_LANES + LANES] = s_re.astype(BF16)
            w1_ref[g, rows, CHUNK_LANES + LANES:CHUNK_LANES + 2 * LANES] = s_im.astype(BF16)
            c_re, c_im = pack_states(cks[0][j + 1] * conj_h, cks[1][t - j] * conj_h)
            w2_ref[g, rows, 0:LANES] = c_re.astype(BF16)
            w2_ref[g, rows, LANES:2 * LANES] = c_im.astype(BF16)
        fwd1 = lane1 < half
        lam_ref[g, 0:1, :] = jnp.where(fwd1, lam_t[0][0], lam_t[1][0])
        lam_ref[g, 1:2, :] = jnp.where(fwd1, lam_t[0][1], lam_t[1][1])
        return carry

    lax.fori_loop(0, GROUPS_PER_BLOCK, group, 0, unroll=2)


def _s5_tables(a_re, a_im, log_dt, b_re, b_im, c_re, c_im):
    g, p, h = N_GROUPS, STATE_DIM, GROUP_CH
    dup = lambda x: jnp.concatenate([x, x], axis=-1)
    a2 = jnp.stack([dup(a_re), dup(a_im)]).reshape(2, 2, g, 1, 2 * p)
    bt = jnp.concatenate([b_re.transpose(0, 1, 3, 2), b_im.transpose(0, 1, 3, 2)], axis=-1)
    cc = jnp.concatenate([c_re, c_im], axis=-1)
    gb = GROUPS_PER_BLOCK
    return pl.pallas_call(
        _s5_prep_kernel,
        grid=(N_GROUP_BLOCKS,),
        in_specs=[pl.BlockSpec((2, 2, gb, 1, 2 * p), lambda i: (0, 0, i, 0, 0)),
                  pl.BlockSpec((2, gb, 1, 1), lambda i: (0, i, 0, 0)),
                  pl.BlockSpec((2, gb, h, 2 * p), lambda i: (0, i, 0, 0)),
                  pl.BlockSpec((2, gb, h, 2 * p), lambda i: (0, i, 0, 0))],
        out_specs=[pl.BlockSpec((gb, CHUNK_LANES, CHUNK_LANES + 2 * LANES), lambda i: (i, 0, 0)),
                   pl.BlockSpec((gb, CHUNK_LANES, 2 * LANES), lambda i: (i, 0, 0)),
                   pl.BlockSpec((gb, 2, LANES), lambda i: (i, 0, 0))],
        out_shape=[jax.ShapeDtypeStruct((g, CHUNK_LANES, CHUNK_LANES + 2 * LANES), BF16),
                   jax.ShapeDtypeStruct((g, CHUNK_LANES, 2 * LANES), BF16),
                   jax.ShapeDtypeStruct((g, 2, LANES), F32)],
        compiler_params=_params(1),
        name="s5_prep",
    )(a2, log_dt.reshape(2, g, 1, 1), bt, cc)


def _row_perm(to_time_major):
    n = PERM_SEQS * PERM_STEPS
    r = lax.broadcasted_iota(jnp.int32, (n, n), 0)
    c = lax.broadcasted_iota(jnp.int32, (n, n), 1)
    if to_time_major:
        src = jnp.bitwise_and(r, PERM_SEQS - 1) * PERM_STEPS + lax.shift_right_logical(r, 3)
    else:
        src = jnp.bitwise_and(r, PERM_STEPS - 1) * PERM_SEQS + lax.shift_right_logical(r, 5)
    return jnp.where(c == src, 1.0, 0.0).astype(BF16)


def _s5_in_kernel(x_ref, m_ref, w_ref, o_ref, *, tt):
    scale = 1.0 + m_ref[:, 1:2, :]
    shift = m_ref[:, 0:1, :]
    perm = _row_perm(True)
    n = PERM_SEQS * PERM_STEPS
    for k in range(tt // (2 * PERM_STEPS)):
        pieces = []
        for hf in range(2):
            t0 = (2 * k + hf) * PERM_STEPS
            h = x_ref[:, t0:t0 + PERM_STEPS, :] * scale + shift
            hb = h.reshape(n, D_MODEL).astype(BF16)
            pieces.append(_dot(perm, hb).astype(BF16))
        u = _dot(jnp.concatenate(pieces, axis=0), w_ref[...])
        o_ref[2 * k * PERM_STEPS:(2 * k + 2) * PERM_STEPS] = u.reshape(2 * PERM_STEPS, PERM_SEQS, D_MODEL)


def _s5_in(x, mods, w_in, wl, tt):
    slots, seq, _ = x.shape
    return pl.pallas_call(
        functools.partial(_s5_in_kernel, tt=tt),
        grid=(slots // PERM_SEQS, seq // tt),
        in_specs=[pl.BlockSpec((PERM_SEQS, tt, D_MODEL), lambda s, t: (s, t, 0)),
                  pl.BlockSpec((PERM_SEQS, 6, D_MODEL), lambda s, t: (s, 0, 0)),
                  pl.BlockSpec((None, D_MODEL, D_MODEL), lambda s, t: (wl, 0, 0))],
        out_specs=pl.BlockSpec((tt, PERM_SEQS, D_MODEL), lambda s, t: (t, s, 0)),
        out_shape=jax.ShapeDtypeStruct((seq, slots, D_MODEL), F32),
        compiler_params=_params(2),
        name="s5_in",
    )(x, mods, w_in)


def _block_transpose8(v):
    lane = lax.broadcasted_iota(jnp.int32, (SUBLANES, LANES), 1)
    v = list(v)
    for d in (4, 2, 1):
        low = jnp.bitwise_and(lane, GROUP_CH * d) == 0
        nxt = list(v)
        for i in range(8):
            if i & d == 0:
                a, b = v[i], v[i + d]
                if 2 * GROUP_CH * d == LANES:
                    r = pltpu.roll(jnp.where(low, b, a), LANES // 2, 1)
                    nxt[i] = jnp.where(low, a, r)
                    nxt[i + d] = jnp.where(low, r, b)
                else:
                    nxt[i] = jnp.where(low, a, pltpu.roll(b, GROUP_CH * d, 1))
                    nxt[i + d] = jnp.where(low, pltpu.roll(a, LANES - GROUP_CH * d, 1), b)
        v = nxt
    return v


def _s5_core_kernel(*refs, seq, nq, paired, has_fin):
    refs = list(refs)
    u_ref, w1_ref, w2_ref, lam_ref, d_ref = refs[:5]
    pos = 5
    h0_ref = None
    if paired:
        h0_ref = refs[pos]
        pos += 1
    y_ref = refs[pos]
    pos += 1
    fin_ref = refs[pos] if has_fin else None
    z_sc, yt_sc, bs_sc, sp_sc = refs[-4 - int(paired):][:4]
    sp2_sc = refs[-1] if paired else None
    gpb = GROUPS_PER_BLOCK
    nc = seq // S5_CHUNK
    ns = nq * SUBLANES
    half = STATE_DIM

    def fold(c, carry):
        for qi in range(nq):
            sl = slice(qi * SUBLANES, (qi + 1) * SUBLANES)
            r0 = pl.multiple_of(c * ns + qi * SUBLANES, SUBLANES)
            for hf in range(2):
                v = [u_ref[c * S5_CHUNK + hf * 8 + t, sl, :] for t in range(8)]
                w = _block_transpose8(v)
                for g in range(gpb):
                    z_sc[g, pl.ds(r0, SUBLANES), hf * LANES:(hf + 1) * LANES] = w[g]
        return carry

    lax.fori_loop(0, nc, fold, 0, unroll=FOLD_UNROLL // nq)

    for g in range(gpb):
        m1 = _dot(z_sc[g].astype(BF16), w1_ref[g])
        yt_sc[g] = m1[:, 0:CHUNK_LANES]
        bs_sc[g] = m1[:, CHUNK_LANES:]

    lane = lax.broadcasted_iota(jnp.int32, (SUBLANES, LANES), 1)
    fwd = jnp.bitwise_and(lane, half) == 0
    lam = [(jnp.broadcast_to(lam_ref[g, 0:1, :], (SUBLANES, LANES)),
            jnp.broadcast_to(lam_ref[g, 1:2, :], (SUBLANES, LANES))) for g in range(gpb)]

    def run_pass(init, dst):
        def body(i, carry):
            out = []
            for g in range(gpb):
                l_re, l_im = lam[g]
                for qi in range(nq):
                    s_re, s_im = carry[2 * (g * nq + qi)], carry[2 * (g * nq + qi) + 1]
                    ri = pl.ds(pl.multiple_of(i * ns + qi * SUBLANES, SUBLANES), SUBLANES)
                    rr = pl.ds(pl.multiple_of((nc - 1 - i) * ns + qi * SUBLANES, SUBLANES), SUBLANES)
                    dst[g, ri, 0:half] = s_re[:, 0:half]
                    dst[g, ri, LANES:LANES + half] = s_im[:, 0:half]
                    dst[g, rr, half:LANES] = s_re[:, half:LANES]
                    dst[g, rr, LANES + half:2 * LANES] = s_im[:, half:LANES]
                    x_re = jnp.where(fwd, bs_sc[g, ri, 0:LANES], bs_sc[g, rr, 0:LANES])
                    x_im = jnp.where(fwd, bs_sc[g, ri, LANES:2 * LANES], bs_sc[g, rr, LANES:2 * LANES])
                    out.append(l_re * s_re - l_im * s_im + x_re)
                    out.append(l_re * s_im + l_im * s_re + x_im)
            return tuple(out)

        return lax.fori_loop(0, nc, body, init)

    if paired:
        init = []
        for g in range(gpb):
            init += [h0_ref[g, 0], h0_ref[g, 1]]
        mid = run_pass(tuple(init), sp_sc)
        handed = [jnp.where(fwd, pltpu.roll(s, 1, 0), pltpu.roll(s, SUBLANES - 1, 0)) for s in mid]
        fin = run_pass(tuple(handed), sp2_sc)
    else:
        zero = jnp.zeros((SUBLANES, LANES), F32)
        fin = run_pass((zero,) * (2 * gpb * nq), sp_sc)

    if has_fin:
        for g in range(gpb):
            for qi in range(nq):
                sl = slice(qi * SUBLANES, (qi + 1) * SUBLANES)
                fin_ref[g, 0, sl, :] = fin[2 * (g * nq + qi)]
                fin_ref[g, 1, sl, :] = fin[2 * (g * nq + qi) + 1]

    if paired:
        shape = (nc * ns, 2 * LANES)
        row = lax.broadcasted_iota(jnp.int32, shape, 0)
        col = lax.broadcasted_iota(jnp.int32, shape, 1)
        first_pass = (jnp.bitwise_and(row, 1) == 0) == (jnp.bitwise_and(col, half) == 0)
    for g in range(gpb):
        states = sp_sc[g]
        if paired:
            states = jnp.where(first_pass, states, sp2_sc[g])
        yt_sc[g] = yt_sc[g] + _dot_nt(states.astype(BF16), w2_ref[g])

    d = jnp.broadcast_to(d_ref[...], (SUBLANES, LANES))

    def unfold(c, carry):
        for qi in range(nq):
            sl = slice(qi * SUBLANES, (qi + 1) * SUBLANES)
            r0 = pl.multiple_of(c * ns + qi * SUBLANES, SUBLANES)
            for hf in range(2):
                w = [yt_sc[g, pl.ds(r0, SUBLANES), hf * LANES:(hf + 1) * LANES] for g in range(gpb)]
                v = _block_transpose8(w)
                for t in range(8):
                    step = c * S5_CHUNK + hf * 8 + t
                    y_ref[step, sl, :] = v[t] + d * u_ref[step, sl, :]
        return carry

    lax.fori_loop(0, nc, unfold, 0, unroll=FOLD_UNROLL // nq)


def _s5_core(u, w1, w2, lam, d_skip, h0, want_final):
    seq, slots, _ = u.shape
    paired = h0 is not None
    nq = 1 if paired else slots // SUBLANES
    ns = nq * SUBLANES
    gpb = GROUPS_PER_BLOCK
    rows = (seq // S5_CHUNK) * ns
    u_spec = pl.BlockSpec((seq, ns, LANES), lambda s, g: (0, s, g))
    in_specs = [u_spec,
                pl.BlockSpec((gpb, CHUNK_LANES, CHUNK_LANES + 2 * LANES), lambda s, g: (g, 0, 0)),
                pl.BlockSpec((gpb, CHUNK_LANES, 2 * LANES), lambda s, g: (g, 0, 0)),
                pl.BlockSpec((gpb, 2, LANES), lambda s, g: (g, 0, 0)),
                pl.BlockSpec((1, LANES), lambda s, g: (0, g))]
    args = [u, w1, w2, lam, d_skip.reshape(1, D_MODEL)]
    if paired:
        in_specs.append(pl.BlockSpec((gpb, 2, SUBLANES, LANES), lambda s, g: (g, 0, s, 0)))
        args.append(h0)
    out_specs = [u_spec]
    out_shape = [jax.ShapeDtypeStruct(u.shape, F32)]
    if want_final:
        out_specs.append(pl.BlockSpec((gpb, 2, ns, LANES), lambda s, g: (g, 0, s, 0)))
        out_shape.append(jax.ShapeDtypeStruct((N_GROUPS, 2, slots, LANES), F32))
    res = pl.pallas_call(
        functools.partial(_s5_core_kernel, seq=seq, nq=nq, paired=paired, has_fin=want_final),
        grid=(slots // ns, N_GROUP_BLOCKS),
        in_specs=in_specs,
        out_specs=out_specs,
        out_shape=out_shape,
        scratch_shapes=[pltpu.VMEM((gpb, rows, CHUNK_LANES), F32) for _ in range(5 if paired else 4)],
        compiler_params=_params(2),
        name="s5_core",
    )(*args)
    return (res[0], res[1]) if want_final else (res[0], None)


def _s5_out_kernel(y_ref, x_ref, m_ref, wab_ref, wo_ref, lng_ref, lnb_ref, o_ref, *, tt, f):
    gate = m_ref[:, 2:3, :]
    perm = _row_perm(False)
    n = PERM_SEQS * PERM_STEPS
    lng = lng_ref[...].reshape(1, 1, D_MODEL)
    lnb = lnb_ref[...].reshape(1, 1, D_MODEL)
    for k in range(tt // (2 * PERM_STEPS)):
        t0 = 2 * k * PERM_STEPS
        hb = _gelu_tanh(y_ref[t0:t0 + 2 * PERM_STEPS].reshape(2 * n, D_MODEL)).astype(BF16)
        acc = None
        for c0 in range(0, f, MXU_TILE):
            val = _dot(hb, wab_ref[:, c0:c0 + MXU_TILE])
            gte = _dot(hb, wab_ref[:, f + c0:f + c0 + MXU_TILE])
            z = (val * _sigmoid(gte)).astype(BF16)
            zp = jnp.concatenate([_dot(perm, z[0:n]).astype(BF16), _dot(perm, z[n:2 * n]).astype(BF16)],
                                 axis=0)
            part = _dot(zp, wo_ref[c0:c0 + MXU_TILE, :])
            acc = part if acc is None else acc + part
        for hf in range(2):
            ts = slice(t0 + hf * PERM_STEPS, t0 + (hf + 1) * PERM_STEPS)
            mix = acc[hf * n:(hf + 1) * n].reshape(PERM_SEQS, PERM_STEPS, D_MODEL)
            r = DEEPNORM_ALPHA * x_ref[:, ts, :] + gate * mix
            o_ref[:, ts, :] = _layer_norm(r, lng, lnb)


def _s5_out(y, x, mods, w_glu, w_out, wl, ln_g, ln_b, tt):
    slots, seq, _ = x.shape
    f = w_out.shape[1]
    resident = pl.Buffered(1)
    tok = pl.BlockSpec((PERM_SEQS, tt, D_MODEL), lambda s, t: (s, t, 0))
    return pl.pallas_call(
        functools.partial(_s5_out_kernel, tt=tt, f=f),
        grid=(slots // PERM_SEQS, seq // tt),
        in_specs=[pl.BlockSpec((tt, PERM_SEQS, D_MODEL), lambda s, t: (t, s, 0)),
                  tok,
                  pl.BlockSpec((PERM_SEQS, 6, D_MODEL), lambda s, t: (s, 0, 0)),
                  pl.BlockSpec((None, D_MODEL, 2 * f), lambda s, t: (wl, 0, 0), pipeline_mode=resident),
                  pl.BlockSpec((None, f, D_MODEL), lambda s, t: (wl, 0, 0), pipeline_mode=resident),
                  pl.BlockSpec((1, D_MODEL), lambda s, t: (0, 0)),
                  pl.BlockSpec((1, D_MODEL), lambda s, t: (0, 0))],
        out_specs=tok,
        out_shape=jax.ShapeDtypeStruct(x.shape, F32),
        compiler_params=_params(2),
        name="s5_out",
    )(y, x, mods, w_glu, w_out, ln_g.reshape(1, D_MODEL), ln_b.reshape(1, D_MODEL))


def _ffn_kernel(xp_ref, xs_ref, m_ref, wi_hbm, wo_hbm, lng_ref, lnb_ref, op_ref, os_ref,
                h_sc, wab_sc, wo_sc, stage_a, stage_b, stage_o, sem, *, f, wl, n_first):
    n_slab = f // MXU_TILE
    step = pl.program_id(0)
    m = m_ref[0]

    def slab_copies(k, slot):
        c0 = k * MXU_TILE
        return (pltpu.make_async_copy(wi_hbm.at[wl, :, pl.ds(c0, MXU_TILE)], stage_a.at[slot], sem.at[0, slot]),
                pltpu.make_async_copy(wi_hbm.at[wl, :, pl.ds(f + c0, MXU_TILE)], stage_b.at[slot], sem.at[1, slot]),
                pltpu.make_async_copy(wo_hbm.at[wl, pl.ds(c0, MXU_TILE), :], stage_o.at[slot], sem.at[2, slot]))

    def run(load_weights, x_ref, o_ref):
        h = x_ref[...] * (1.0 + m[4:5]) + m[3:4]
        h_sc[...] = h.astype(BF16)
        ahead = FFN_STAGE_SLOTS - 1
        if load_weights:
            for k in range(min(ahead, n_slab)):
                for cp in slab_copies(k, k % FFN_STAGE_SLOTS):
                    cp.start()
        acc = None
        for k in range(n_slab):
            c0 = k * MXU_TILE
            if load_weights:
                slot = k % FFN_STAGE_SLOTS
                if k + ahead < n_slab:
                    for cp in slab_copies(k + ahead, (k + ahead) % FFN_STAGE_SLOTS):
                        cp.start()
                for cp in slab_copies(k, slot):
                    cp.wait()
                wab_sc[:, c0:c0 + MXU_TILE] = stage_a[slot].astype(BF16)
                wab_sc[:, f + c0:f + c0 + MXU_TILE] = stage_b[slot].astype(BF16)
                wo_sc[c0:c0 + MXU_TILE, :] = stage_o[slot].astype(BF16)
            a = _dot(h_sc[...], wab_sc[:, c0:c0 + MXU_TILE])
            b = _dot(h_sc[...], wab_sc[:, f + c0:f + c0 + MXU_TILE])
            z = (a * _sigmoid(a)) * b
            part = _dot(z.astype(BF16), wo_sc[c0:c0 + MXU_TILE, :])
            acc = part if acc is None else acc + part
        r = DEEPNORM_ALPHA * x_ref[...] + m[5:6] * acc
        o_ref[...] = _layer_norm(r, lng_ref[...], lnb_ref[...])

    @pl.when(step == 0)
    def _():
        run(True, xp_ref, op_ref)

    @pl.when(jnp.logical_and(step > 0, step < n_first))
    def _():
        run(False, xp_ref, op_ref)

    @pl.when(step >= n_first)
    def _():
        run(False, xs_ref, os_ref)


def _ffn(xp, xs, mods, seq_s, w_in, w_out, wl, ln_g, ln_b, tm):
    f = w_out.shape[1]
    n_first = xp.shape[0] // tm
    n_second = xs.shape[0] // tm
    tiles_per_seq = seq_s // tm
    first_tile = lambda i: (jnp.minimum(i, n_first - 1), 0)
    second_tile = lambda i: (jnp.maximum(i - n_first, 0), 0)
    cond_row = lambda i: (jnp.where(i < n_first, 0, 1 + jnp.maximum(i - n_first, 0) // tiles_per_seq), 0, 0)
    return pl.pallas_call(
        functools.partial(_ffn_kernel, f=f, wl=wl, n_first=n_first),
        grid=(n_first + n_second,),
        in_specs=[pl.BlockSpec((tm, D_MODEL), first_tile),
                  pl.BlockSpec((tm, D_MODEL), second_tile),
                  pl.BlockSpec((1, 6, D_MODEL), cond_row),
                  pl.BlockSpec(memory_space=pl.ANY),
                  pl.BlockSpec(memory_space=pl.ANY),
                  pl.BlockSpec((1, D_MODEL), lambda i: (0, 0)),
                  pl.BlockSpec((1, D_MODEL), lambda i: (0, 0))],
        out_specs=[pl.BlockSpec((tm, D_MODEL), first_tile), pl.BlockSpec((tm, D_MODEL), second_tile)],
        out_shape=[jax.ShapeDtypeStruct(xp.shape, F32), jax.ShapeDtypeStruct(xs.shape, F32)],
        scratch_shapes=[pltpu.VMEM((tm, D_MODEL), BF16),
                        pltpu.VMEM((D_MODEL, 2 * f), BF16),
                        pltpu.VMEM((f, D_MODEL), BF16),
                        pltpu.VMEM((FFN_STAGE_SLOTS, D_MODEL, MXU_TILE), F32),
                        pltpu.VMEM((FFN_STAGE_SLOTS, D_MODEL, MXU_TILE), F32),
                        pltpu.VMEM((FFN_STAGE_SLOTS, MXU_TILE, D_MODEL), F32),
                        pltpu.SemaphoreType.DMA((3, FFN_STAGE_SLOTS))],
        compiler_params=_params(1),
        name="ffn",
    )(xp, xs, mods, w_in, w_out, ln_g.reshape(1, D_MODEL), ln_b.reshape(1, D_MODEL))


def _qkv_kernel(*refs, rope):
    if rope:
        x_ref, m_ref, w_ref, qg_ref, kg_ref, cos_ref, sin_ref = refs[:7]
    else:
        x_ref, m_ref, w_ref, qg_ref, kg_ref = refs[:5]
    q_ref, k_ref, v_ref = refs[-3:]
    rows = x_ref.shape[0]
    m = m_ref[0]
    h = x_ref[...] * (1.0 + m[1:2]) + m[0:1]
    qkv = _dot(h.astype(BF16), w_ref[...])
    if rope:
        cos = cos_ref[...]
        sin = sin_ref[...]
        lane = lax.broadcasted_iota(jnp.int32, cos.shape, 1)
        first = jnp.bitwise_and(lane, AXIS_PAIRS) == 0
    for hd in range(N_HEADS + N_KV_HEADS):
        xh = qkv[:, hd * HEAD_DIM:(hd + 1) * HEAD_DIM]
        gain = qg_ref[...] if hd < N_HEADS else kg_ref[...]
        n = xh * lax.rsqrt(jnp.mean(xh * xh, axis=-1, keepdims=True) + RMS_EPS) * gain
        if rope:
            up = pltpu.roll(n, HEAD_DIM - AXIS_PAIRS, 1)
            down = pltpu.roll(n, AXIS_PAIRS, 1)
            n = n * cos + jnp.where(first, up, down) * sin
        if hd < N_HEADS:
            q_ref[:, hd * HEAD_DIM:(hd + 1) * HEAD_DIM] = (n * Q_SCALE_LOG2).astype(BF16)
        else:
            k_ref[pl.ds(hd - N_HEADS, rows, stride=N_KV_HEADS), :] = n
    for g in range(N_KV_HEADS):
        v_ref[pl.ds(g, rows, stride=N_KV_HEADS), :] = (
            qkv[:, D_Q + D_KV + g * HEAD_DIM:D_Q + D_KV + (g + 1) * HEAD_DIM])


def _rope_tables(seq):
    pos = jnp.arange(seq, dtype=jnp.int32)
    row = (pos // GRID_W).astype(F32)
    col = (pos % GRID_W).astype(F32)
    inv = ROPE_THETA ** (-jnp.arange(AXIS_PAIRS, dtype=F32) / AXIS_PAIRS)
    ar = row[:, None] * inv
    ac = col[:, None] * inv
    cos = jnp.concatenate([jnp.cos(ar), jnp.cos(ar), jnp.cos(ac), jnp.cos(ac)], axis=-1)
    sin = jnp.concatenate([-jnp.sin(ar), jnp.sin(ar), -jnp.sin(ac), jnp.sin(ac)], axis=-1)
    return cos, sin


def _qkv(x, mods, w_qkv, wl, q_gain, k_gain, nb, seq, tm, rope, kv_layers=1, kv_prev=None):
    nt = seq // tm
    per_batch = mods.shape[0] > 1
    in_specs = [pl.BlockSpec((tm, D_MODEL), lambda b, t: (b * nt + t, 0)),
                pl.BlockSpec((1, 6, D_MODEL), lambda b, t: (b if per_batch else 0, 0, 0)),
                pl.BlockSpec((None, D_MODEL, QKV_DIM), lambda b, t: (wl, 0, 0)),
                pl.BlockSpec((1, HEAD_DIM), lambda b, t: (0, 0)),
                pl.BlockSpec((1, HEAD_DIM), lambda b, t: (0, 0))]
    args = [x, mods, w_qkv, q_gain.reshape(1, HEAD_DIM), k_gain.reshape(1, HEAD_DIM)]
    if rope:
        cos, sin = _rope_tables(seq)
        in_specs += [pl.BlockSpec((tm, HEAD_DIM), lambda b, t: (t, 0))] * 2
        args += [cos, sin]
    n_tok = nb * seq
    kv_slot = wl if kv_layers > 1 else 0
    kv_spec = pl.BlockSpec((None, None, tm * N_KV_HEADS, HEAD_DIM), lambda b, t: (b, kv_slot, t, 0))
    kv_shape = jax.ShapeDtypeStruct((nb, kv_layers, seq * N_KV_HEADS, HEAD_DIM), F32)
    aliases = {}
    if kv_prev is not None:
        aliases = {len(args): 1, len(args) + 1: 2}
        in_specs += [pl.BlockSpec(memory_space=pl.ANY)] * 2
        args += list(kv_prev)
    return pl.pallas_call(
        functools.partial(_qkv_kernel, rope=rope),
        grid=(nb, nt),
        in_specs=in_specs,
        out_specs=[pl.BlockSpec((tm, D_Q), lambda b, t: (b * nt + t, 0)), kv_spec, kv_spec],
        out_shape=[jax.ShapeDtypeStruct((n_tok, D_Q), BF16), kv_shape, kv_shape],
        input_output_aliases=aliases,
        compiler_params=_params(2),
        name="qkv_rope" if rope else "qkv",
    )(*args)


def _attn_kernel(*refs, has_cache, stack):
    if has_cache:
        (q_ref, k_ref, v_ref, ck_ref, cv_ref, x_ref, m_ref, wo_ref, lng_ref, lnb_ref,
         o_ref, oh_sc) = refs
    else:
        q_ref, k_ref, v_ref, x_ref, m_ref, wo_ref, lng_ref, lnb_ref, o_ref, oh_sc = refs
    for g in range(N_KV_HEADS):
        sl = slice(g * HEAD_DIM, (g + 1) * HEAD_DIM)
        head_rows = lambda ref: ref[pl.ds(g, ref.shape[0] // N_KV_HEADS, stride=N_KV_HEADS), :]
        kg = head_rows(k_ref).astype(BF16)
        vg = head_rows(v_ref).astype(BF16)
        if has_cache:
            ckg = head_rows(ck_ref).astype(BF16)
            cvg = head_rows(cv_ref).astype(BF16)
        tq = q_ref.shape[0]
        for r0 in range(0, KV_REP, stack):
            heads = [g * KV_REP + r0 + r for r in range(stack)]
            qh = jnp.concatenate([q_ref[:, h * HEAD_DIM:(h + 1) * HEAD_DIM] for h in heads], axis=0)
            s1 = _dot_nt(qh, kg)
            mx = jnp.max(s1, axis=-1, keepdims=True)
            if has_cache:
                s2 = _dot_nt(qh, ckg)
                mx = jnp.maximum(mx, jnp.max(s2, axis=-1, keepdims=True))
            p1 = jnp.exp2(s1 - mx)
            den = jnp.sum(p1, axis=-1, keepdims=True)
            o = _dot(p1.astype(BF16), vg)
            if has_cache:
                p2 = jnp.exp2(s2 - mx)
                den = den + jnp.sum(p2, axis=-1, keepdims=True)
                o = o + _dot(p2.astype(BF16), cvg)
            o = (o / den).astype(BF16)
            for r, h in enumerate(heads):
                oh_sc[:, h * HEAD_DIM:(h + 1) * HEAD_DIM] = o[r * tq:(r + 1) * tq]
    mix = _dot(oh_sc[...], wo_ref[...])
    m = m_ref[0]
    r = DEEPNORM_ALPHA * x_ref[...] + m[2:3] * mix
    o_ref[...] = _layer_norm(r, lng_ref[...], lnb_ref[...])


def _attention(q, k, v, kv_slot, cache_k, cache_v, layer_j, x, mods, w_o, ln_g, ln_b, nb, seq, tq):
    nt = seq // tq
    per_batch = mods.shape[0] > 1
    has_cache = cache_k is not None
    tok = pl.BlockSpec((tq, D_MODEL), lambda b, t: (b * nt + t, 0))
    kv = pl.BlockSpec((None, None, seq * N_KV_HEADS, HEAD_DIM), lambda b, t: (b, kv_slot, 0, 0))
    in_specs, args = [tok, kv, kv], [q, k, v]
    if has_cache:
        past = cache_k.shape[2]
        cspec = pl.BlockSpec((None, None, past * N_KV_HEADS, HEAD_DIM), lambda b, t: (b, layer_j, 0, 0))
        in_specs += [cspec, cspec]
        flat = (cache_k.shape[0], cache_k.shape[1], past * N_KV_HEADS, HEAD_DIM)
        args += [cache_k.reshape(flat), cache_v.reshape(flat)]
    in_specs += [tok,
                 pl.BlockSpec((1, 6, D_MODEL), lambda b, t: (b if per_batch else 0, 0, 0)),
                 pl.BlockSpec((None, D_Q, D_MODEL), lambda b, t: (layer_j, 0, 0)),
                 pl.BlockSpec((1, D_MODEL), lambda b, t: (0, 0)),
                 pl.BlockSpec((1, D_MODEL), lambda b, t: (0, 0))]
    args += [x, mods, w_o, ln_g.reshape(1, D_MODEL), ln_b.reshape(1, D_MODEL)]
    return pl.pallas_call(
        functools.partial(_attn_kernel, has_cache=has_cache, stack=2 if has_cache else KV_REP),
        grid=(nb, nt),
        in_specs=in_specs,
        out_specs=tok,
        out_shape=jax.ShapeDtypeStruct((nb * seq, D_MODEL), F32),
        scratch_shapes=[pltpu.VMEM((tq, D_Q), BF16)],
        compiler_params=_params(2),
        name="attn_cache" if has_cache else "attn",
    )(*args)


def _latent_h0(st):
    f = st[:, 0].transpose(2, 1, 0, 3)
    b = st[:, 1].transpose(2, 1, 0, 3)
    z = jnp.zeros_like(f)
    even = jnp.concatenate([f, z], axis=-1)
    odd = jnp.concatenate([z, b], axis=-1)
    h0 = jnp.stack([even, odd], axis=3)
    return h0.reshape(N_GROUPS, 2, 2 * st.shape[0], 2 * STATE_DIM)


def kernel(x_prompt, x_sample, c, cache_k, cache_v, state_s5, c_ctx, w_mod, b_mod, ln_g, ln_b, w_s5_in, s5_a_re, s5_a_im, s5_log_dt, s5_b_re, s5_b_im, s5_c_re, s5_c_im, s5_d, w_s5_glu, w_s5_out, w_qkv, q_norm_g, k_norm_g, w_o, w_ffn_in, w_ffn_out):
    nbp, seqp, _ = x_prompt.shape
    nbs, seqs, _ = x_sample.shape
    xp = x_prompt.reshape(nbp * seqp, D_MODEL)
    xs = x_sample.reshape(nbs * seqs, D_MODEL)
    s5_slots_s, s5_seq_s = 2 * nbs, seqs // 2

    cond = jnp.concatenate([c_ctx[None, :], c, jnp.zeros((8 - 1 - nbs, D_MODEL), F32)], axis=0)
    mods = _adaln(cond, w_mod, b_mod)
    mods_all = mods.reshape(DEPTH, 8, 6, D_MODEL)
    mods_p = mods[:, 0:1].reshape(DEPTH, 1, 6, D_MODEL)
    mods_s = mods[:, 1:1 + nbs].reshape(DEPTH, nbs, 6, D_MODEL)

    w_in, w_glu, w_out = w_s5_in.astype(BF16), w_s5_glu.astype(BF16), w_s5_out.astype(BF16)
    wq, wo = w_qkv.astype(BF16), w_o.astype(BF16)

    kv_rows = (nbp, DEPTH // 2, seqp * N_KV_HEADS, HEAD_DIM)
    new_kv, new_s = (jnp.zeros(kv_rows, F32), jnp.zeros(kv_rows, F32)), []
    for layer in range(DEPTH):
        j = layer // 2
        mp, ms = mods_p[layer], mods_s[layer]
        lg0, lb0, lg1, lb1 = ln_g[layer, 0], ln_b[layer, 0], ln_g[layer, 1], ln_b[layer, 1]
        if layer % 2 == 0:
            w1, w2, lam = _s5_tables(s5_a_re[j], s5_a_im[j], s5_log_dt[j], s5_b_re[j], s5_b_im[j],
                                     s5_c_re[j], s5_c_im[j])
            mp_slots = jnp.broadcast_to(mp, (nbp, 6, D_MODEL))
            ms_slots = jnp.repeat(ms, 2, axis=0)
            xp3 = xp.reshape(nbp, seqp, D_MODEL)
            xs3 = xs.reshape(s5_slots_s, s5_seq_s, D_MODEL)
            up = _s5_in(xp3, mp_slots, w_in, j, 128)
            us = _s5_in(xs3, ms_slots, w_in, j, 128)
            yp, fin = _s5_core(up, w1, w2, lam, s5_d[j], None, True)
            ys, _ = _s5_core(us, w1, w2, lam, s5_d[j], _latent_h0(state_s5[:, j]), False)
            fin = fin.reshape(N_GROUPS, 2, nbp, 2, STATE_DIM)
            new_s.append(fin.transpose(2, 3, 1, 0, 4))
            xp = _s5_out(yp, xp3, mp_slots, w_glu, w_out, j, lg0, lb0, 128).reshape(nbp * seqp, D_MODEL)
            xs = _s5_out(ys, xs3, ms_slots, w_glu, w_out, j, lg0, lb0, 128).reshape(nbs * seqs, D_MODEL)
        else:
            qp, kp, vp = _qkv(xp, mp, wq, j, q_norm_g[j], k_norm_g[j], nbp, seqp, 256, False,
                              kv_layers=DEPTH // 2, kv_prev=new_kv)
            new_kv = (kp, vp)
            qs, ks, vs = _qkv(xs, ms, wq, j, q_norm_g[j], k_norm_g[j], nbs, seqs, 512, True)
            xp = _attention(qp, kp, vp, j, None, None, j, xp, mp, wo, lg0, lb0, nbp, seqp, 256)
            xs = _attention(qs, ks, vs, 0, cache_k, cache_v, j, xs, ms, wo, lg0, lb0, nbs, seqs, 256)
        xp, xs = _ffn(xp, xs, mods_all[layer], seqs, w_ffn_in, w_ffn_out, layer, lg1, lb1, 512)

    y_prompt = xp.reshape(nbp, seqp, D_MODEL)
    y_sample = xs.reshape(nbs, seqs, D_MODEL)
    kv_out = (nbp, DEPTH // 2, seqp, N_KV_HEADS, HEAD_DIM)
    return (y_prompt, y_sample, new_kv[0].reshape(kv_out), new_kv[1].reshape(kv_out),
            jnp.stack(new_s, axis=1))
```

```python
import functools
import math

import jax
import jax.numpy as jnp
from jax import lax
from jax.experimental import pallas as pl
from jax.experimental.pallas import tpu as pltpu

F32 = jnp.float32
BF16 = jnp.bfloat16

D_MODEL = 1024
DEPTH = 4
N_GROUPS = 64
GROUP_CH = 16
STATE_DIM = 64
HEAD_DIM = 128
N_HEADS = 8
N_KV_HEADS = 2
KV_REP = N_HEADS // N_KV_HEADS
D_Q = N_HEADS * HEAD_DIM
D_KV = N_KV_HEADS * HEAD_DIM
QKV_DIM = D_Q + 2 * D_KV
GRID_W = 64
ROPE_THETA = 10000.0
AXIS_PAIRS = HEAD_DIM // 4
ATTN_SCALE = HEAD_DIM ** -0.5
Q_SCALE_LOG2 = ATTN_SCALE * math.log2(math.e)
DEEPNORM_ALPHA = (2.0 * DEPTH) ** 0.25
LN_EPS = 1e-6
RMS_EPS = 1e-6

V7X_VMEM_LIMIT_BYTES = 56 * 1024 * 1024
LANES = 128
SUBLANES = 8
MXU_TILE = 256
FFN_STAGE_SLOTS = 2
GROUPS_PER_BLOCK = LANES // GROUP_CH
N_GROUP_BLOCKS = N_GROUPS // GROUPS_PER_BLOCK
S5_CHUNK = MXU_TILE // GROUP_CH
CHUNK_LANES = S5_CHUNK * GROUP_CH
FOLD_UNROLL = 4
PERM_SEQS = SUBLANES
PERM_STEPS = MXU_TILE // PERM_SEQS
NT_DIMS = (((1,), (1,)), ((), ()))


def _params(n_axes):
    return pltpu.CompilerParams(dimension_semantics=("arbitrary",) * n_axes,
                                vmem_limit_bytes=V7X_VMEM_LIMIT_BYTES)


def _sigmoid(x):
    return 1.0 / (1.0 + jnp.exp(-x))


def _gelu_tanh(x):
    cdf = 0.5 * (1.0 + jnp.tanh(math.sqrt(2.0 / math.pi) * (x + 0.044715 * (x * x * x))))
    return x * cdf


def _layer_norm(r, g, b):
    mu = jnp.mean(r, axis=-1, keepdims=True)
    d = r - mu
    var = jnp.mean(d * d, axis=-1, keepdims=True)
    return d * lax.rsqrt(var + LN_EPS) * g + b


def _dot(a, b):
    return jnp.dot(a, b, preferred_element_type=F32)


def _dot_nt(a, b):
    return lax.dot_general(a, b, NT_DIMS, preferred_element_type=F32)


def _adaln_kernel(c_ref, w_ref, b_ref, o_ref):
    c = c_ref[...]
    s = c * _sigmoid(c)
    o_ref[0] = _dot(s.astype(BF16), w_ref[0].astype(BF16)) + b_ref[0]


def _adaln(cond, w_mod, b_mod):
    tn = 1536
    n = 6 * D_MODEL
    return pl.pallas_call(
        _adaln_kernel,
        grid=(DEPTH, n // tn),
        in_specs=[pl.BlockSpec((8, D_MODEL), lambda l, j: (0, 0)),
                  pl.BlockSpec((1, D_MODEL, tn), lambda l, j: (l, 0, j)),
                  pl.BlockSpec((1, 1, tn), lambda l, j: (l, 0, j))],
        out_specs=pl.BlockSpec((1, 8, tn), lambda l, j: (l, 0, j)),
        out_shape=jax.ShapeDtypeStruct((DEPTH, 8, n), F32),
        compiler_params=_params(2),
        name="adaln",
    )(cond, w_mod, b_mod.reshape(DEPTH, 1, n))


def _s5_prep_kernel(a_ref, ldt_ref, bt_ref, c_ref, w1_ref, w2_ref, lam_ref):
    half = STATE_DIM
    lane1 = lax.broadcasted_iota(jnp.int32, (1, LANES), 1)
    sgn1 = jnp.where(lane1 < half, -1.0, 1.0)
    lane_h = lax.broadcasted_iota(jnp.int32, (GROUP_CH, LANES), 1)
    first_h = lane_h < half
    conj_h = jnp.where(first_h, 1.0, -1.0)
    lane_c = lax.broadcasted_iota(jnp.int32, (GROUP_CH, CHUNK_LANES), 1)
    t = S5_CHUNK

    def cmul(pr, pi, x):
        return pr * x + (pi * sgn1) * pltpu.roll(x, half, 1)

    def pack_states(f, b):
        return (jnp.where(first_h, f, pltpu.roll(b, half, 1)),
                jnp.where(first_h, pltpu.roll(f, half, 1), b))

    def group(g, carry):
        kt, qs, cks, lam_t = [], [], [], []
        for d in range(2):
            a_re = a_ref[0, d, g]
            a_im = a_ref[1, d, g]
            dt = jnp.exp(ldt_ref[d, g])
            mag = jnp.exp(dt * a_re)
            lr = mag * jnp.cos(dt * a_im)
            li = mag * jnp.sin(dt * a_im)
            den = a_re * a_re + a_im * a_im
            nr = lr - 1.0
            k_re = (nr * a_re + li * a_im) / den
            k_im = (li * a_re - nr * a_im) / den
            bb = cmul(k_re, k_im, bt_ref[d, g])
            cc = c_ref[d, g]
            pr = jnp.ones((1, LANES), F32)
            pi = jnp.zeros((1, LANES), F32)
            ck, q = [], []
            for k in range(t + 1):
                ck.append(cmul(pr, pi, cc))
                if k < t:
                    q.append(cmul(pr, pi, bb))
                    pr, pi = pr * lr - pi * li, pr * li + pi * lr
            lam_t.append((pr, pi))
            order = range(t) if d == 0 else range(t - 1, -1, -1)
            rhs = jnp.concatenate([ck[k] for k in order], axis=0).astype(BF16)
            kt.append(_dot_nt((bb * conj_h).astype(BF16), rhs))
            qs.append(q)
            cks.append(ck)
        for j in range(t):
            rows = slice(j * GROUP_CH, (j + 1) * GROUP_CH)
            tf = kt[0] if j == 0 else pltpu.roll(kt[0], GROUP_CH * j, 1)
            tf = jnp.where(lane_c >= GROUP_CH * j, tf, 0.0)
            back = GROUP_CH * (t - 1 - j)
            tb = kt[1] if back == 0 else pltpu.roll(kt[1], CHUNK_LANES - back, 1)
            tb = jnp.where(lane_c < GROUP_CH * (j + 1), tb, 0.0)
            w1_ref[g, rows, 0:CHUNK_LANES] = (tf + tb).astype(BF16)
            s_re, s_im = pack_states(qs[0][t - 1 - j], qs[1][j])
            w1_ref[g, rows, CHUNK_LANES:CHUNK_LANES + LANES] = s_re.astype(BF16)
            w1_ref[g, rows, CHUNK_LANES + LANES:CHUNK_LANES + 2 * LANES] = s_im.astype(BF16)
            c_re, c_im = pack_states(cks[0][j + 1] * conj_h, cks[1][t - j] * conj_h)
            w2_ref[g, rows, 0:LANES] = c_re.astype(BF16)
            w2_ref[g, rows, LANES:2 * LANES] = c_im.astype(BF16)
        fwd1 = lane1 < half
        lam_ref[g, 0:1, :] = jnp.where(fwd1, lam_t[0][0], lam_t[1][0])
        lam_ref[g, 1:2, :] = jnp.where(fwd1, lam_t[0][1], lam_t[1][1])
        return carry

    lax.fori_loop(0, GROUPS_PER_BLOCK, group, 0, unroll=2)


def _s5_tables(a_re, a_im, log_dt, b_re, b_im, c_re, c_im):
    g, p, h = N_GROUPS, STATE_DIM, GROUP_CH
    dup = lambda x: jnp.concatenate([x, x], axis=-1)
    a2 = jnp.stack([dup(a_re), dup(a_im)]).reshape(2, 2, g, 1, 2 * p)
    bt = jnp.concatenate([b_re.transpose(0, 1, 3, 2), b_im.transpose(0, 1, 3, 2)], axis=-1)
    cc = jnp.concatenate([c_re, c_im], axis=-1)
    gb = GROUPS_PER_BLOCK
    return pl.pallas_call(
        _s5_prep_kernel,
        grid=(N_GROUP_BLOCKS,),
        in_specs=[pl.BlockSpec((2, 2, gb, 1, 2 * p), lambda i: (0, 0, i, 0, 0)),
                  pl.BlockSpec((2, gb, 1, 1), lambda i: (0, i, 0, 0)),
                  pl.BlockSpec((2, gb, h, 2 * p), lambda i: (0, i, 0, 0)),
                  pl.BlockSpec((2, gb, h, 2 * p), lambda i: (0, i, 0, 0))],
        out_specs=[pl.BlockSpec((gb, CHUNK_LANES, CHUNK_LANES + 2 * LANES), lambda i: (i, 0, 0)),
                   pl.BlockSpec((gb, CHUNK_LANES, 2 * LANES), lambda i: (i, 0, 0)),
                   pl.BlockSpec((gb, 2, LANES), lambda i: (i, 0, 0))],
        out_shape=[jax.ShapeDtypeStruct((g, CHUNK_LANES, CHUNK_LANES + 2 * LANES), BF16),
                   jax.ShapeDtypeStruct((g, CHUNK_LANES, 2 * LANES), BF16),
                   jax.ShapeDtypeStruct((g, 2, LANES), F32)],
        compiler_params=_params(1),
        name="s5_prep",
    )(a2, log_dt.reshape(2, g, 1, 1), bt, cc)


def _row_perm(to_time_major):
    n = PERM_SEQS * PERM_STEPS
    r = lax.broadcasted_iota(jnp.int32, (n, n), 0)
    c = lax.broadcasted_iota(jnp.int32, (n, n), 1)
    if to_time_major:
        src = jnp.bitwise_and(r, PERM_SEQS - 1) * PERM_STEPS + lax.shift_right_logical(r, 3)
    else:
        src = jnp.bitwise_and(r, PERM_STEPS - 1) * PERM_SEQS + lax.shift_right_logical(r, 5)
    return jnp.where(c == src, 1.0, 0.0).astype(BF16)


def _s5_in_kernel(x_ref, m_ref, w_ref, o_ref, *, tt):
    scale = 1.0 + m_ref[:, 1:2, :]
    shift = m_ref[:, 0:1, :]
    perm = _row_perm(True)
    n = PERM_SEQS * PERM_STEPS
    for k in range(tt // (2 * PERM_STEPS)):
        pieces = []
        for hf in range(2):
            t0 = (2 * k + hf) * PERM_STEPS
            h = x_ref[:, t0:t0 + PERM_STEPS, :] * scale + shift
            hb = h.reshape(n, D_MODEL).astype(BF16)
            pieces.append(_dot(perm, hb).astype(BF16))
        u = _dot(jnp.concatenate(pieces, axis=0), w_ref[...])
        o_ref[2 * k * PERM_STEPS:(2 * k + 2) * PERM_STEPS] = u.reshape(2 * PERM_STEPS, PERM_SEQS, D_MODEL)


def _two_group_steps(xa, xb, tt):
    ta, tb = xa.shape[1] // tt, xb.shape[1] // tt
    na = (xa.shape[0] // PERM_SEQS) * ta
    nb = (xb.shape[0] // PERM_SEQS) * tb

    def tile_a(i):
        i = jnp.minimum(i, na - 1)
        return i // ta, i % ta

    def tile_b(i):
        i = jnp.maximum(i - na, 0)
        return i // tb, i % tb

    return na, nb, tile_a, tile_b


def _s5_in_pair_kernel(xa_ref, ma_ref, xb_ref, mb_ref, w_ref, oa_ref, ob_ref, *, tt, n_first):
    @pl.when(pl.program_id(0) < n_first)
    def _():
        _s5_in_kernel(xa_ref, ma_ref, w_ref, oa_ref, tt=tt)

    @pl.when(pl.program_id(0) >= n_first)
    def _():
        _s5_in_kernel(xb_ref, mb_ref, w_ref, ob_ref, tt=tt)


def _s5_in(xa, ma, xb, mb, w_in, wl, tt):
    na, nb, tile_a, tile_b = _two_group_steps(xa, xb, tt)
    seq_major = lambda tile: (lambda i: (*tile(i), 0))
    time_major = lambda tile: (lambda i: (*tile(i)[::-1], 0))
    slot_only = lambda tile: (lambda i: (tile(i)[0], 0, 0))
    return pl.pallas_call(
        functools.partial(_s5_in_pair_kernel, tt=tt, n_first=na),
        grid=(na + nb,),
        in_specs=[pl.BlockSpec((PERM_SEQS, tt, D_MODEL), seq_major(tile_a)),
                  pl.BlockSpec((PERM_SEQS, 6, D_MODEL), slot_only(tile_a)),
                  pl.BlockSpec((PERM_SEQS, tt, D_MODEL), seq_major(tile_b)),
                  pl.BlockSpec((PERM_SEQS, 6, D_MODEL), slot_only(tile_b)),
                  pl.BlockSpec((None, D_MODEL, D_MODEL), lambda i: (wl, 0, 0))],
        out_specs=[pl.BlockSpec((tt, PERM_SEQS, D_MODEL), time_major(tile_a)),
                   pl.BlockSpec((tt, PERM_SEQS, D_MODEL), time_major(tile_b))],
        out_shape=[jax.ShapeDtypeStruct((x.shape[1], x.shape[0], D_MODEL), F32) for x in (xa, xb)],
        compiler_params=_params(1),
        name="s5_in",
    )(xa, ma, xb, mb, w_in)


def _block_transpose8(v):
    lane = lax.broadcasted_iota(jnp.int32, (SUBLANES, LANES), 1)
    v = list(v)
    for d in (4, 2, 1):
        low = jnp.bitwise_and(lane, GROUP_CH * d) == 0
        nxt = list(v)
        for i in range(8):
            if i & d == 0:
                a, b = v[i], v[i + d]
                if 2 * GROUP_CH * d == LANES:
                    r = pltpu.roll(jnp.where(low, b, a), LANES // 2, 1)
                    nxt[i] = jnp.where(low, a, r)
                    nxt[i + d] = jnp.where(low, r, b)
                else:
                    nxt[i] = jnp.where(low, a, pltpu.roll(b, GROUP_CH * d, 1))
                    nxt[i + d] = jnp.where(low, pltpu.roll(a, LANES - GROUP_CH * d, 1), b)
        v = nxt
    return v


def _s5_core_kernel(*refs, seq, nq, paired, has_fin):
    refs = list(refs)
    u_ref, w1_ref, w2_ref, lam_ref, d_ref = refs[:5]
    pos = 5
    h0_ref = None
    if paired:
        h0_ref = refs[pos]
        pos += 1
    y_ref = refs[pos]
    pos += 1
    fin_ref = refs[pos] if has_fin else None
    z_sc, yt_sc, bs_sc, sp_sc = refs[-4 - int(paired):][:4]
    sp2_sc = refs[-1] if paired else None
    gpb = GROUPS_PER_BLOCK
    nc = seq // S5_CHUNK
    ns = nq * SUBLANES
    half = STATE_DIM

    def fold(c, carry):
        for qi in range(nq):
            sl = slice(qi * SUBLANES, (qi + 1) * SUBLANES)
            r0 = pl.multiple_of(c * ns + qi * SUBLANES, SUBLANES)
            for hf in range(2):
                v = [u_ref[c * S5_CHUNK + hf * 8 + t, sl, :] for t in range(8)]
                w = _block_transpose8(v)
                for g in range(gpb):
                    z_sc[g, pl.ds(r0, SUBLANES), hf * LANES:(hf + 1) * LANES] = w[g]
        return carry

    lax.fori_loop(0, nc, fold, 0, unroll=FOLD_UNROLL // nq)

    for g in range(gpb):
        m1 = _dot(z_sc[g].astype(BF16), w1_ref[g])
        yt_sc[g] = m1[:, 0:CHUNK_LANES]
        bs_sc[g] = m1[:, CHUNK_LANES:]

    lane = lax.broadcasted_iota(jnp.int32, (SUBLANES, LANES), 1)
    fwd = jnp.bitwise_and(lane, half) == 0
    lam = [(jnp.broadcast_to(lam_ref[g, 0:1, :], (SUBLANES, LANES)),
            jnp.broadcast_to(lam_ref[g, 1:2, :], (SUBLANES, LANES))) for g in range(gpb)]

    def run_pass(init, dst):
        def body(i, carry):
            out = []
            for g in range(gpb):
                l_re, l_im = lam[g]
                for qi in range(nq):
                    s_re, s_im = carry[2 * (g * nq + qi)], carry[2 * (g * nq + qi) + 1]
                    ri = pl.ds(pl.multiple_of(i * ns + qi * SUBLANES, SUBLANES), SUBLANES)
                    rr = pl.ds(pl.multiple_of((nc - 1 - i) * ns + qi * SUBLANES, SUBLANES), SUBLANES)
                    dst[g, ri, 0:half] = s_re[:, 0:half]
                    dst[g, ri, LANES:LANES + half] = s_im[:, 0:half]
                    dst[g, rr, half:LANES] = s_re[:, half:LANES]
                    dst[g, rr, LANES + half:2 * LANES] = s_im[:, half:LANES]
                    x_re = jnp.where(fwd, bs_sc[g, ri, 0:LANES], bs_sc[g, rr, 0:LANES])
                    x_im = jnp.where(fwd, bs_sc[g, ri, LANES:2 * LANES], bs_sc[g, rr, LANES:2 * LANES])
                    out.append(l_re * s_re - l_im * s_im + x_re)
                    out.append(l_re * s_im + l_im * s_re + x_im)
            return tuple(out)

        return lax.fori_loop(0, nc, body, init)

    if paired:
        init = []
        for g in range(gpb):
            init += [h0_ref[g, 0], h0_ref[g, 1]]
        mid = run_pass(tuple(init), sp_sc)
        handed = [jnp.where(fwd, pltpu.roll(s, 1, 0), pltpu.roll(s, SUBLANES - 1, 0)) for s in mid]
        fin = run_pass(tuple(handed), sp2_sc)
    else:
        zero = jnp.zeros((SUBLANES, LANES), F32)
        fin = run_pass((zero,) * (2 * gpb * nq), sp_sc)

    if has_fin:
        for g in range(gpb):
            for qi in range(nq):
                sl = slice(qi * SUBLANES, (qi + 1) * SUBLANES)
                fin_ref[g, 0, sl, :] = fin[2 * (g * nq + qi)]
                fin_ref[g, 1, sl, :] = fin[2 * (g * nq + qi) + 1]

    if paired:
        shape = (nc * ns, 2 * LANES)
        row = lax.broadcasted_iota(jnp.int32, shape, 0)
        col = lax.broadcasted_iota(jnp.int32, shape, 1)
        first_pass = (jnp.bitwise_and(row, 1) == 0) == (jnp.bitwise_and(col, half) == 0)
    for g in range(gpb):
        states = sp_sc[g]
        if paired:
            states = jnp.where(first_pass, states, sp2_sc[g])
        yt_sc[g] = yt_sc[g] + _dot_nt(states.astype(BF16), w2_ref[g])

    d = jnp.broadcast_to(d_ref[...], (SUBLANES, LANES))

    def unfold(c, carry):
        for qi in range(nq):
            sl = slice(qi * SUBLANES, (qi + 1) * SUBLANES)
            r0 = pl.multiple_of(c * ns + qi * SUBLANES, SUBLANES)
            for hf in range(2):
                w = [yt_sc[g, pl.ds(r0, SUBLANES), hf * LANES:(hf + 1) * LANES] for g in range(gpb)]
                v = _block_transpose8(w)
                for t in range(8):
                    step = c * S5_CHUNK + hf * 8 + t
                    y_ref[step, sl, :] = v[t] + d * u_ref[step, sl, :]
        return carry

    lax.fori_loop(0, nc, unfold, 0, unroll=FOLD_UNROLL // nq)


def _s5_core(u, w1, w2, lam, d_skip, h0, want_final):
    seq, slots, _ = u.shape
    paired = h0 is not None
    nq = 1 if paired else slots // SUBLANES
    ns = nq * SUBLANES
    gpb = GROUPS_PER_BLOCK
    rows = (seq // S5_CHUNK) * ns
    u_spec = pl.BlockSpec((seq, ns, LANES), lambda s, g: (0, s, g))
    in_specs = [u_spec,
                pl.BlockSpec((gpb, CHUNK_LANES, CHUNK_LANES + 2 * LANES), lambda s, g: (g, 0, 0)),
                pl.BlockSpec((gpb, CHUNK_LANES, 2 * LANES), lambda s, g: (g, 0, 0)),
                pl.BlockSpec((gpb, 2, LANES), lambda s, g: (g, 0, 0)),
                pl.BlockSpec((1, LANES), lambda s, g: (0, g))]
    args = [u, w1, w2, lam, d_skip.reshape(1, D_MODEL)]
    if paired:
        in_specs.append(pl.BlockSpec((gpb, 2, SUBLANES, LANES), lambda s, g: (g, 0, s, 0)))
        args.append(h0)
    out_specs = [u_spec]
    out_shape = [jax.ShapeDtypeStruct(u.shape, F32)]
    if want_final:
        out_specs.append(pl.BlockSpec((gpb, 2, ns, LANES), lambda s, g: (g, 0, s, 0)))
        out_shape.append(jax.ShapeDtypeStruct((N_GROUPS, 2, slots, LANES), F32))
    res = pl.pallas_call(
        functools.partial(_s5_core_kernel, seq=seq, nq=nq, paired=paired, has_fin=want_final),
        grid=(slots // ns, N_GROUP_BLOCKS),
        in_specs=in_specs,
        out_specs=out_specs,
        out_shape=out_shape,
        scratch_shapes=[pltpu.VMEM((gpb, rows, CHUNK_LANES), F32) for _ in range(5 if paired else 4)],
        compiler_params=_params(2),
        name="s5_core",
    )(*args)
    return (res[0], res[1]) if want_final else (res[0], None)


def _s5_out_kernel(y_ref, x_ref, m_ref, wab_ref, wo_ref, lng_ref, lnb_ref, o_ref, *, tt, f):
    gate = m_ref[:, 2:3, :]
    perm = _row_perm(False)
    n = PERM_SEQS * PERM_STEPS
    lng = lng_ref[...].reshape(1, 1, D_MODEL)
    lnb = lnb_ref[...].reshape(1, 1, D_MODEL)
    for k in range(tt // (2 * PERM_STEPS)):
        t0 = 2 * k * PERM_STEPS
        hb = _gelu_tanh(y_ref[t0:t0 + 2 * PERM_STEPS].reshape(2 * n, D_MODEL)).astype(BF16)
        acc = None
        for c0 in range(0, f, MXU_TILE):
            val = _dot(hb, wab_ref[:, c0:c0 + MXU_TILE])
            gte = _dot(hb, wab_ref[:, f + c0:f + c0 + MXU_TILE])
            z = (val * _sigmoid(gte)).astype(BF16)
            zp = jnp.concatenate([_dot(perm, z[0:n]).astype(BF16), _dot(perm, z[n:2 * n]).astype(BF16)],
                                 axis=0)
            part = _dot(zp, wo_ref[c0:c0 + MXU_TILE, :])
            acc = part if acc is None else acc + part
        for hf in range(2):
            ts = slice(t0 + hf * PERM_STEPS, t0 + (hf + 1) * PERM_STEPS)
            mix = acc[hf * n:(hf + 1) * n].reshape(PERM_SEQS, PERM_STEPS, D_MODEL)
            r = DEEPNORM_ALPHA * x_ref[:, ts, :] + gate * mix
            o_ref[:, ts, :] = _layer_norm(r, lng, lnb)


def _s5_out_pair_kernel(ya_ref, xa_ref, ma_ref, yb_ref, xb_ref, mb_ref, wab_ref, wo_ref, lng_ref, lnb_ref,
                        oa_ref, ob_ref, *, tt, f, n_first):
    @pl.when(pl.program_id(0) < n_first)
    def _():
        _s5_out_kernel(ya_ref, xa_ref, ma_ref, wab_ref, wo_ref, lng_ref, lnb_ref, oa_ref, tt=tt, f=f)

    @pl.when(pl.program_id(0) >= n_first)
    def _():
        _s5_out_kernel(yb_ref, xb_ref, mb_ref, wab_ref, wo_ref, lng_ref, lnb_ref, ob_ref, tt=tt, f=f)


def _s5_out(ya, xa, ma, yb, xb, mb, w_glu, w_out, wl, ln_g, ln_b, tt):
    f = w_out.shape[1]
    na, nb, tile_a, tile_b = _two_group_steps(xa, xb, tt)
    seq_major = lambda tile: pl.BlockSpec((PERM_SEQS, tt, D_MODEL), lambda i: (*tile(i), 0))
    time_major = lambda tile: pl.BlockSpec((tt, PERM_SEQS, D_MODEL), lambda i: (*tile(i)[::-1], 0))
    slot_only = lambda tile: pl.BlockSpec((PERM_SEQS, 6, D_MODEL), lambda i: (tile(i)[0], 0, 0))
    resident = pl.Buffered(1)
    return pl.pallas_call(
        functools.partial(_s5_out_pair_kernel, tt=tt, f=f, n_first=na),
        grid=(na + nb,),
        in_specs=[time_major(tile_a), seq_major(tile_a), slot_only(tile_a),
                  time_major(tile_b), seq_major(tile_b), slot_only(tile_b),
                  pl.BlockSpec((None, D_MODEL, 2 * f), lambda i: (wl, 0, 0), pipeline_mode=resident),
                  pl.BlockSpec((None, f, D_MODEL), lambda i: (wl, 0, 0), pipeline_mode=resident),
                  pl.BlockSpec((1, D_MODEL), lambda i: (0, 0)),
                  pl.BlockSpec((1, D_MODEL), lambda i: (0, 0))],
        out_specs=[seq_major(tile_a), seq_major(tile_b)],
        out_shape=[jax.ShapeDtypeStruct(xa.shape, F32), jax.ShapeDtypeStruct(xb.shape, F32)],
        compiler_params=_params(1),
        name="s5_out",
    )(ya, xa, ma, yb, xb, mb, w_glu, w_out, ln_g.reshape(1, D_MODEL), ln_b.reshape(1, D_MODEL))


def _ffn_kernel(xp_ref, xs_ref, m_ref, wi_hbm, wo_hbm, lng_ref, lnb_ref, op_ref, os_ref,
                h_sc, wab_sc, wo_sc, stage_a, stage_b, stage_o, sem, *, f, wl, n_first):
    n_slab = f // MXU_TILE
    step = pl.program_id(0)
    m = m_ref[0]

    def slab_copies(k, slot):
        c0 = k * MXU_TILE
        return (pltpu.make_async_copy(wi_hbm.at[wl, :, pl.ds(c0, MXU_TILE)], stage_a.at[slot], sem.at[0, slot]),
                pltpu.make_async_copy(wi_hbm.at[wl, :, pl.ds(f + c0, MXU_TILE)], stage_b.at[slot], sem.at[1, slot]),
                pltpu.make_async_copy(wo_hbm.at[wl, pl.ds(c0, MXU_TILE), :], stage_o.at[slot], sem.at[2, slot]))

    def run(load_weights, x_ref, o_ref):
        h = x_ref[...] * (1.0 + m[4:5]) + m[3:4]
        h_sc[...] = h.astype(BF16)
        ahead = FFN_STAGE_SLOTS - 1
        if load_weights:
            for k in range(min(ahead, n_slab)):
                for cp in slab_copies(k, k % FFN_STAGE_SLOTS):
                    cp.start()
        acc = None
        for k in range(n_slab):
            c0 = k * MXU_TILE
            if load_weights:
                slot = k % FFN_STAGE_SLOTS
                if k + ahead < n_slab:
                    for cp in slab_copies(k + ahead, (k + ahead) % FFN_STAGE_SLOTS):
                        cp.start()
                for cp in slab_copies(k, slot):
                    cp.wait()
                wab_sc[:, c0:c0 + MXU_TILE] = stage_a[slot].astype(BF16)
                wab_sc[:, f + c0:f + c0 + MXU_TILE] = stage_b[slot].astype(BF16)
                wo_sc[c0:c0 + MXU_TILE, :] = stage_o[slot].astype(BF16)
            a = _dot(h_sc[...], wab_sc[:, c0:c0 + MXU_TILE])
            b = _dot(h_sc[...], wab_sc[:, f + c0:f + c0 + MXU_TILE])
            z = (a * _sigmoid(a)) * b
            part = _dot(z.astype(BF16), wo_sc[c0:c0 + MXU_TILE, :])
            acc = part if acc is None else acc + part
        r = DEEPNORM_ALPHA * x_ref[...] + m[5:6] * acc
        o_ref[...] = _layer_norm(r, lng_ref[...], lnb_ref[...])

    @pl.when(step == 0)
    def _():
        run(True, xp_ref, op_ref)

    @pl.when(jnp.logical_and(step > 0, step < n_first))
    def _():
        run(False, xp_ref, op_ref)

    @pl.when(step >= n_first)
    def _():
        run(False, xs_ref, os_ref)


def _ffn(xp, xs, mods, seq_s, w_in, w_out, wl, ln_g, ln_b, tm):
    f = w_out.shape[1]
    n_first = xp.shape[0] // tm
    n_second = xs.shape[0] // tm
    tiles_per_seq = seq_s // tm
    first_tile = lambda i: (jnp.minimum(i, n_first - 1), 0)
    second_tile = lambda i: (jnp.maximum(i - n_first, 0), 0)
    cond_row = lambda i: (jnp.where(i < n_first, 0, 1 + jnp.maximum(i - n_first, 0) // tiles_per_seq), 0, 0)
    return pl.pallas_call(
        functools.partial(_ffn_kernel, f=f, wl=wl, n_first=n_first),
        grid=(n_first + n_second,),
        in_specs=[pl.BlockSpec((tm, D_MODEL), first_tile),
                  pl.BlockSpec((tm, D_MODEL), second_tile),
                  pl.BlockSpec((1, 6, D_MODEL), cond_row),
                  pl.BlockSpec(memory_space=pl.ANY),
                  pl.BlockSpec(memory_space=pl.ANY),
                  pl.BlockSpec((1, D_MODEL), lambda i: (0, 0)),
                  pl.BlockSpec((1, D_MODEL), lambda i: (0, 0))],
        out_specs=[pl.BlockSpec((tm, D_MODEL), first_tile), pl.BlockSpec((tm, D_MODEL), second_tile)],
        out_shape=[jax.ShapeDtypeStruct(xp.shape, F32), jax.ShapeDtypeStruct(xs.shape, F32)],
        scratch_shapes=[pltpu.VMEM((tm, D_MODEL), BF16),
                        pltpu.VMEM((D_MODEL, 2 * f), BF16),
                        pltpu.VMEM((f, D_MODEL), BF16),
                        pltpu.VMEM((FFN_STAGE_SLOTS, D_MODEL, MXU_TILE), F32),
                        pltpu.VMEM((FFN_STAGE_SLOTS, D_MODEL, MXU_TILE), F32),
                        pltpu.VMEM((FFN_STAGE_SLOTS, MXU_TILE, D_MODEL), F32),
                        pltpu.SemaphoreType.DMA((3, FFN_STAGE_SLOTS))],
        compiler_params=_params(1),
        name="ffn",
    )(xp, xs, mods, w_in, w_out, ln_g.reshape(1, D_MODEL), ln_b.reshape(1, D_MODEL))


def _qkv_kernel(*refs, rope):
    if rope:
        x_ref, m_ref, w_ref, qg_ref, kg_ref, cos_ref, sin_ref = refs[:7]
    else:
        x_ref, m_ref, w_ref, qg_ref, kg_ref = refs[:5]
    q_ref, k_ref, v_ref = refs[-3:]
    rows = x_ref.shape[0]
    m = m_ref[0]
    h = x_ref[...] * (1.0 + m[1:2]) + m[0:1]
    qkv = _dot(h.astype(BF16), w_ref[...])
    if rope:
        cos = cos_ref[...]
        sin = sin_ref[...]
        lane = lax.broadcasted_iota(jnp.int32, cos.shape, 1)
        first = jnp.bitwise_and(lane, AXIS_PAIRS) == 0
    for hd in range(N_HEADS + N_KV_HEADS):
        xh = qkv[:, hd * HEAD_DIM:(hd + 1) * HEAD_DIM]
        gain = qg_ref[...] if hd < N_HEADS else kg_ref[...]
        n = xh * lax.rsqrt(jnp.mean(xh * xh, axis=-1, keepdims=True) + RMS_EPS) * gain
        if rope:
            up = pltpu.roll(n, HEAD_DIM - AXIS_PAIRS, 1)
            down = pltpu.roll(n, AXIS_PAIRS, 1)
            n = n * cos + jnp.where(first, up, down) * sin
        if hd < N_HEADS:
            q_ref[:, hd * HEAD_DIM:(hd + 1) * HEAD_DIM] = (n * Q_SCALE_LOG2).astype(BF16)
        else:
            k_ref[pl.ds(hd - N_HEADS, rows, stride=N_KV_HEADS), :] = n
    for g in range(N_KV_HEADS):
        v_ref[pl.ds(g, rows, stride=N_KV_HEADS), :] = (
            qkv[:, D_Q + D_KV + g * HEAD_DIM:D_Q + D_KV + (g + 1) * HEAD_DIM])


def _rope_tables(seq):
    pos = jnp.arange(seq, dtype=jnp.int32)
    row = (pos // GRID_W).astype(F32)
    col = (pos % GRID_W).astype(F32)
    inv = ROPE_THETA ** (-jnp.arange(AXIS_PAIRS, dtype=F32) / AXIS_PAIRS)
    ar = row[:, None] * inv
    ac = col[:, None] * inv
    cos = jnp.concatenate([jnp.cos(ar), jnp.cos(ar), jnp.cos(ac), jnp.cos(ac)], axis=-1)
    sin = jnp.concatenate([-jnp.sin(ar), jnp.sin(ar), -jnp.sin(ac), jnp.sin(ac)], axis=-1)
    return cos, sin


def _qkv_pair_kernel(*refs, n_first):
    xp_ref, mp_ref, xs_ref, ms_ref, w_ref, qg_ref, kg_ref, cos_ref, sin_ref = refs[:9]
    qp_ref, kp_ref, vp_ref, qs_ref, ks_ref, vs_ref = refs[-6:]

    @pl.when(pl.program_id(0) < n_first)
    def _():
        _qkv_kernel(xp_ref, mp_ref, w_ref, qg_ref, kg_ref, qp_ref, kp_ref, vp_ref, rope=False)

    @pl.when(pl.program_id(0) >= n_first)
    def _():
        _qkv_kernel(xs_ref, ms_ref, w_ref, qg_ref, kg_ref, cos_ref, sin_ref, qs_ref, ks_ref, vs_ref, rope=True)


def _qkv(xp, mp, seq_p, xs, ms, seq_s, w_qkv, wl, q_gain, k_gain, kv_prev):
    nbp, nbs = xp.shape[0] // seq_p, xs.shape[0] // seq_s
    tm_p, tm_s = seq_p, seq_s // 2
    nts = seq_s // tm_s
    n_first = nbp
    first = lambda i: jnp.minimum(i, n_first - 1)
    second = lambda i: jnp.maximum(i - n_first, 0)
    cos, sin = _rope_tables(seq_s)
    kv_layers = kv_prev[0].shape[1]
    in_specs = [pl.BlockSpec((tm_p, D_MODEL), lambda i: (first(i), 0)),
                pl.BlockSpec((1, 6, D_MODEL), lambda i: (0, 0, 0)),
                pl.BlockSpec((tm_s, D_MODEL), lambda i: (second(i), 0)),
                pl.BlockSpec((1, 6, D_MODEL), lambda i: (second(i) // nts, 0, 0)),
                pl.BlockSpec((None, D_MODEL, QKV_DIM), lambda i: (wl, 0, 0)),
                pl.BlockSpec((1, HEAD_DIM), lambda i: (0, 0)),
                pl.BlockSpec((1, HEAD_DIM), lambda i: (0, 0)),
                pl.BlockSpec((tm_s, HEAD_DIM), lambda i: (second(i) % nts, 0)),
                pl.BlockSpec((tm_s, HEAD_DIM), lambda i: (second(i) % nts, 0)),
                pl.BlockSpec(memory_space=pl.ANY),
                pl.BlockSpec(memory_space=pl.ANY)]
    args = [xp, mp, xs, ms, w_qkv, q_gain.reshape(1, HEAD_DIM), k_gain.reshape(1, HEAD_DIM), cos, sin,
            kv_prev[0], kv_prev[1]]
    kvp_spec = pl.BlockSpec((None, None, tm_p * N_KV_HEADS, HEAD_DIM), lambda i: (first(i), wl, 0, 0))
    kvs_spec = pl.BlockSpec((None, None, tm_s * N_KV_HEADS, HEAD_DIM),
                            lambda i: (second(i) // nts, 0, second(i) % nts, 0))
    kvp_shape = jax.ShapeDtypeStruct((nbp, kv_layers, seq_p * N_KV_HEADS, HEAD_DIM), F32)
    kvs_shape = jax.ShapeDtypeStruct((nbs, 1, seq_s * N_KV_HEADS, HEAD_DIM), F32)
    return pl.pallas_call(
        functools.partial(_qkv_pair_kernel, n_first=n_first),
        grid=(n_first + nbs * nts,),
        in_specs=in_specs,
        out_specs=[pl.BlockSpec((tm_p, D_Q), lambda i: (first(i), 0)), kvp_spec, kvp_spec,
                   pl.BlockSpec((tm_s, D_Q), lambda i: (second(i), 0)), kvs_spec, kvs_spec],
        out_shape=[jax.ShapeDtypeStruct((xp.shape[0], D_Q), BF16), kvp_shape, kvp_shape,
                   jax.ShapeDtypeStruct((xs.shape[0], D_Q), BF16), kvs_shape, kvs_shape],
        input_output_aliases={9: 1, 10: 2},
        compiler_params=_params(1),
        name="qkv",
    )(*args)


def _attn_kernel(*refs, has_cache, stack):
    if has_cache:
        (q_ref, k_ref, v_ref, ck_ref, cv_ref, x_ref, m_ref, wo_ref, lng_ref, lnb_ref,
         o_ref, oh_sc) = refs
    else:
        q_ref, k_ref, v_ref, x_ref, m_ref, wo_ref, lng_ref, lnb_ref, o_ref, oh_sc = refs
    for g in range(N_KV_HEADS):
        sl = slice(g * HEAD_DIM, (g + 1) * HEAD_DIM)
        head_rows = lambda ref: ref[pl.ds(g, ref.shape[0] // N_KV_HEADS, stride=N_KV_HEADS), :]
        kg = head_rows(k_ref).astype(BF16)
        vg = head_rows(v_ref).astype(BF16)
        if has_cache:
            ckg = head_rows(ck_ref).astype(BF16)
            cvg = head_rows(cv_ref).astype(BF16)
        tq = q_ref.shape[0]
        for r0 in range(0, KV_REP, stack):
            heads = [g * KV_REP + r0 + r for r in range(stack)]
            qh = jnp.concatenate([q_ref[:, h * HEAD_DIM:(h + 1) * HEAD_DIM] for h in heads], axis=0)
            s1 = _dot_nt(qh, kg)
            mx = jnp.max(s1, axis=-1, keepdims=True)
            if has_cache:
                s2 = _dot_nt(qh, ckg)
                mx = jnp.maximum(mx, jnp.max(s2, axis=-1, keepdims=True))
            p1 = jnp.exp2(s1 - mx)
            den = jnp.sum(p1, axis=-1, keepdims=True)
            o = _dot(p1.astype(BF16), vg)
            if has_cache:
                p2 = jnp.exp2(s2 - mx)
                den = den + jnp.sum(p2, axis=-1, keepdims=True)
                o = o + _dot(p2.astype(BF16), cvg)
            o = (o / den).astype(BF16)
            for r, h in enumerate(heads):
                oh_sc[:, h * HEAD_DIM:(h + 1) * HEAD_DIM] = o[r * tq:(r + 1) * tq]
    mix = _dot(oh_sc[...], wo_ref[...])
    m = m_ref[0]
    r = DEEPNORM_ALPHA * x_ref[...] + m[2:3] * mix
    o_ref[...] = _layer_norm(r, lng_ref[...], lnb_ref[...])


def _attention(q, k, v, kv_slot, cache_k, cache_v, layer_j, x, mods, w_o, ln_g, ln_b, nb, seq, tq):
    nt = seq // tq
    per_batch = mods.shape[0] > 1
    has_cache = cache_k is not None
    tok = pl.BlockSpec((tq, D_MODEL), lambda b, t: (b * nt + t, 0))
    kv = pl.BlockSpec((None, None, seq * N_KV_HEADS, HEAD_DIM), lambda b, t: (b, kv_slot, 0, 0))
    in_specs, args = [tok, kv, kv], [q, k, v]
    if has_cache:
        past = cache_k.shape[2]
        cspec = pl.BlockSpec((None, None, past * N_KV_HEADS, HEAD_DIM), lambda b, t: (b, layer_j, 0, 0))
        in_specs += [cspec, cspec]
        flat = (cache_k.shape[0], cache_k.shape[1], past * N_KV_HEADS, HEAD_DIM)
        args += [cache_k.reshape(flat), cache_v.reshape(flat)]
    in_specs += [tok,
                 pl.BlockSpec((1, 6, D_MODEL), lambda b, t: (b if per_batch else 0, 0, 0)),
                 pl.BlockSpec((None, D_Q, D_MODEL), lambda b, t: (layer_j, 0, 0)),
                 pl.BlockSpec((1, D_MODEL), lambda b, t: (0, 0)),
                 pl.BlockSpec((1, D_MODEL), lambda b, t: (0, 0))]
    args += [x, mods, w_o, ln_g.reshape(1, D_MODEL), ln_b.reshape(1, D_MODEL)]
    return pl.pallas_call(
        functools.partial(_attn_kernel, has_cache=has_cache, stack=2 if has_cache else KV_REP),
        grid=(nb, nt),
        in_specs=in_specs,
        out_specs=tok,
        out_shape=jax.ShapeDtypeStruct((nb * seq, D_MODEL), F32),
        scratch_shapes=[pltpu.VMEM((tq, D_Q), BF16)],
        compiler_params=_params(2),
        name="attn_cache" if has_cache else "attn",
    )(*args)


def _latent_h0(st):
    f = st[:, 0].transpose(2, 1, 0, 3)
    b = st[:, 1].transpose(2, 1, 0, 3)
    z = jnp.zeros_like(f)
    even = jnp.concatenate([f, z], axis=-1)
    odd = jnp.concatenate([z, b], axis=-1)
    h0 = jnp.stack([even, odd], axis=3)
    return h0.reshape(N_GROUPS, 2, 2 * st.shape[0], 2 * STATE_DIM)


def kernel(x_prompt, x_sample, c, cache_k, cache_v, state_s5, c_ctx, w_mod, b_mod, ln_g, ln_b, w_s5_in, s5_a_re, s5_a_im, s5_log_dt, s5_b_re, s5_b_im, s5_c_re, s5_c_im, s5_d, w_s5_glu, w_s5_out, w_qkv, q_norm_g, k_norm_g, w_o, w_ffn_in, w_ffn_out):
    nbp, seqp, _ = x_prompt.shape
    nbs, seqs, _ = x_sample.shape
    xp = x_prompt.reshape(nbp * seqp, D_MODEL)
    xs = x_sample.reshape(nbs * seqs, D_MODEL)
    s5_slots_s, s5_seq_s = 2 * nbs, seqs // 2

    cond = jnp.concatenate([c_ctx[None, :], c, jnp.zeros((8 - 1 - nbs, D_MODEL), F32)], axis=0)
    mods = _adaln(cond, w_mod, b_mod)
    mods_all = mods.reshape(DEPTH, 8, 6, D_MODEL)
    mods_p = mods[:, 0:1].reshape(DEPTH, 1, 6, D_MODEL)
    mods_s = mods[:, 1:1 + nbs].reshape(DEPTH, nbs, 6, D_MODEL)

    w_in, w_glu, w_out = w_s5_in.astype(BF16), w_s5_glu.astype(BF16), w_s5_out.astype(BF16)
    wq, wo = w_qkv.astype(BF16), w_o.astype(BF16)

    kv_rows = (nbp, DEPTH // 2, seqp * N_KV_HEADS, HEAD_DIM)
    new_kv, new_s = (jnp.zeros(kv_rows, F32), jnp.zeros(kv_rows, F32)), []
    for layer in range(DEPTH):
        j = layer // 2
        mp, ms = mods_p[layer], mods_s[layer]
        lg0, lb0, lg1, lb1 = ln_g[layer, 0], ln_b[layer, 0], ln_g[layer, 1], ln_b[layer, 1]
        if layer % 2 == 0:
            w1, w2, lam = _s5_tables(s5_a_re[j], s5_a_im[j], s5_log_dt[j], s5_b_re[j], s5_b_im[j],
                                     s5_c_re[j], s5_c_im[j])
            mp_slots = jnp.broadcast_to(mp, (nbp, 6, D_MODEL))
            ms_slots = jnp.repeat(ms, 2, axis=0)
            xp3 = xp.reshape(nbp, seqp, D_MODEL)
            xs3 = xs.reshape(s5_slots_s, s5_seq_s, D_MODEL)
            up, us = _s5_in(xp3, mp_slots, xs3, ms_slots, w_in, j, 128)
            yp, fin = _s5_core(up, w1, w2, lam, s5_d[j], None, True)
            ys, _ = _s5_core(us, w1, w2, lam, s5_d[j], _latent_h0(state_s5[:, j]), False)
            fin = fin.reshape(N_GROUPS, 2, nbp, 2, STATE_DIM)
            new_s.append(fin.transpose(2, 3, 1, 0, 4))
            xp, xs = _s5_out(yp, xp3, mp_slots, ys, xs3, ms_slots, w_glu, w_out, j, lg0, lb0, 64)
            xp = xp.reshape(nbp * seqp, D_MODEL)
            xs = xs.reshape(nbs * seqs, D_MODEL)
        else:
            qp, kp, vp, qs, ks, vs = _qkv(xp, mp, seqp, xs, ms, seqs, wq, j, q_norm_g[j], k_norm_g[j], new_kv)
            new_kv = (kp, vp)
            xp = _attention(qp, kp, vp, j, None, None, j, xp, mp, wo, lg0, lb0, nbp, seqp, 256)
            xs = _attention(qs, ks, vs, 0, cache_k, cache_v, j, xs, ms, wo, lg0, lb0, nbs, seqs, 256)
        xp, xs = _ffn(xp, xs, mods_all[layer], seqs, w_ffn_in, w_ffn_out, layer, lg1, lb1, 512)

    y_prompt = xp.reshape(nbp, seqp, D_MODEL)
    y_sample = xs.reshape(nbs, seqs, D_MODEL)
    kv_out = (nbp, DEPTH // 2, seqp, N_KV_HEADS, HEAD_DIM)
    return (y_prompt, y_sample, new_kv[0].reshape(kv_out), new_kv[1].reshape(kv_out),
            jnp.stack(new_s, axis=1))
```

```python
import functools
import math

import jax
import jax.numpy as jnp
from jax import lax
from jax.experimental import pallas as pl
from jax.experimental.pallas import tpu as pltpu

F32 = jnp.float32
BF16 = jnp.bfloat16

D_MODEL = 1024
DEPTH = 4
N_GROUPS = 64
GROUP_CH = 16
STATE_DIM = 64
HEAD_DIM = 128
N_HEADS = 8
N_KV_HEADS = 2
KV_REP = N_HEADS // N_KV_HEADS
D_Q = N_HEADS * HEAD_DIM
D_KV = N_KV_HEADS * HEAD_DIM
QKV_DIM = D_Q + 2 * D_KV
GRID_W = 64
ROPE_THETA = 10000.0
AXIS_PAIRS = HEAD_DIM // 4
ATTN_SCALE = HEAD_DIM ** -0.5
Q_SCALE_LOG2 = ATTN_SCALE * math.log2(math.e)
DEEPNORM_ALPHA = (2.0 * DEPTH) ** 0.25
LN_EPS = 1e-6
RMS_EPS = 1e-6

V7X_VMEM_LIMIT_BYTES = 56 * 1024 * 1024
LANES = 128
SUBLANES = 8
MXU_TILE = 256
FFN_STAGE_SLOTS = 2
GROUPS_PER_BLOCK = LANES // GROUP_CH
N_GROUP_BLOCKS = N_GROUPS // GROUPS_PER_BLOCK
S5_CHUNK = MXU_TILE // GROUP_CH
CHUNK_LANES = S5_CHUNK * GROUP_CH
FOLD_UNROLL = 4
PERM_SEQS = SUBLANES
PERM_STEPS = MXU_TILE // PERM_SEQS
NT_DIMS = (((1,), (1,)), ((), ()))


def _params(n_axes):
    return pltpu.CompilerParams(dimension_semantics=("arbitrary",) * n_axes,
                                vmem_limit_bytes=V7X_VMEM_LIMIT_BYTES)


def _sigmoid(x):
    return 1.0 / (1.0 + jnp.exp(-x))


def _gelu_tanh(x):
    cdf = 0.5 * (1.0 + jnp.tanh(math.sqrt(2.0 / math.pi) * (x + 0.044715 * (x * x * x))))
    return x * cdf


def _layer_norm(r, g, b):
    mu = jnp.mean(r, axis=-1, keepdims=True)
    d = r - mu
    var = jnp.mean(d * d, axis=-1, keepdims=True)
    return d * lax.rsqrt(var + LN_EPS) * g + b


def _dot(a, b):
    return jnp.dot(a, b, preferred_element_type=F32)


def _dot_nt(a, b):
    return lax.dot_general(a, b, NT_DIMS, preferred_element_type=F32)


def _adaln_kernel(c_ref, w_ref, b_ref, o_ref):
    c = c_ref[...]
    s = c * _sigmoid(c)
    o_ref[0] = _dot(s.astype(BF16), w_ref[0].astype(BF16)) + b_ref[0]


def _adaln(cond, w_mod, b_mod):
    tn = 1536
    n = 6 * D_MODEL
    return pl.pallas_call(
        _adaln_kernel,
        grid=(DEPTH, n // tn),
        in_specs=[pl.BlockSpec((8, D_MODEL), lambda l, j: (0, 0)),
                  pl.BlockSpec((1, D_MODEL, tn), lambda l, j: (l, 0, j)),
                  pl.BlockSpec((1, 1, tn), lambda l, j: (l, 0, j))],
        out_specs=pl.BlockSpec((1, 8, tn), lambda l, j: (l, 0, j)),
        out_shape=jax.ShapeDtypeStruct((DEPTH, 8, n), F32),
        compiler_params=_params(2),
        name="adaln",
    )(cond, w_mod, b_mod.reshape(DEPTH, 1, n))


def _s5_prep_kernel(a_ref, ldt_ref, bt_ref, c_ref, w1_ref, w2_ref, lam_ref):
    half = STATE_DIM
    lane1 = lax.broadcasted_iota(jnp.int32, (1, LANES), 1)
    sgn1 = jnp.where(lane1 < half, -1.0, 1.0)
    lane_h = lax.broadcasted_iota(jnp.int32, (GROUP_CH, LANES), 1)
    first_h = lane_h < half
    conj_h = jnp.where(first_h, 1.0, -1.0)
    lane_c = lax.broadcasted_iota(jnp.int32, (GROUP_CH, CHUNK_LANES), 1)
    t = S5_CHUNK

    def cmul(pr, pi, x):
        return pr * x + (pi * sgn1) * pltpu.roll(x, half, 1)

    def pack_states(f, b):
        return (jnp.where(first_h, f, pltpu.roll(b, half, 1)),
                jnp.where(first_h, pltpu.roll(f, half, 1), b))

    def group(g, carry):
        kt, qs, cks, lam_t = [], [], [], []
        for d in range(2):
            a_re = a_ref[0, d, g]
            a_im = a_ref[1, d, g]
            dt = jnp.exp(ldt_ref[d, g])
            mag = jnp.exp(dt * a_re)
            lr = mag * jnp.cos(dt * a_im)
            li = mag * jnp.sin(dt * a_im)
            den = a_re * a_re + a_im * a_im
            nr = lr - 1.0
            k_re = (nr * a_re + li * a_im) / den
            k_im = (li * a_re - nr * a_im) / den
            bb = cmul(k_re, k_im, bt_ref[d, g])
            cc = c_ref[d, g]
            pr = jnp.ones((1, LANES), F32)
            pi = jnp.zeros((1, LANES), F32)
            ck, q = [], []
            for k in range(t + 1):
                ck.append(cmul(pr, pi, cc))
                if k < t:
                    q.append(cmul(pr, pi, bb))
                    pr, pi = pr * lr - pi * li, pr * li + pi * lr
            lam_t.append((pr, pi))
            order = range(t) if d == 0 else range(t - 1, -1, -1)
            rhs = jnp.concatenate([ck[k] for k in order], axis=0).astype(BF16)
            kt.append(_dot_nt((bb * conj_h).astype(BF16), rhs))
            qs.append(q)
            cks.append(ck)
        for j in range(t):
            rows = slice(j * GROUP_CH, (j + 1) * GROUP_CH)
            tf = kt[0] if j == 0 else pltpu.roll(kt[0], GROUP_CH * j, 1)
            tf = jnp.where(lane_c >= GROUP_CH * j, tf, 0.0)
            back = GROUP_CH * (t - 1 - j)
            tb = kt[1] if back == 0 else pltpu.roll(kt[1], CHUNK_LANES - back, 1)
            tb = jnp.where(lane_c < GROUP_CH * (j + 1), tb, 0.0)
            w1_ref[g, rows, 0:CHUNK_LANES] = (tf + tb).astype(BF16)
            s_re, s_im = pack_states(qs[0][t - 1 - j], qs[1][j])
            w1_ref[g, rows, CHUNK_LANES:CHUNK_LANES + LANES] = s_re.astype(BF16)
            w1_ref[g, rows, CHUNK_LANES + LANES:CHUNK_LANES + 2 * LANES] = s_im.astype(BF16)
            c_re, c_im = pack_states(cks[0][j + 1] * conj_h, cks[1][t - j] * conj_h)
            w2_ref[g, rows, 0:LANES] = c_re.astype(BF16)
            w2_ref[g, rows, LANES:2 * LANES] = c_im.astype(BF16)
        fwd1 = lane1 < half
        lam_ref[g, 0:1, :] = jnp.where(fwd1, lam_t[0][0], lam_t[1][0])
        lam_ref[g, 1:2, :] = jnp.where(fwd1, lam_t[0][1], lam_t[1][1])
        return carry

    lax.fori_loop(0, GROUPS_PER_BLOCK, group, 0, unroll=2)


def _s5_tables(a_re, a_im, log_dt, b_re, b_im, c_re, c_im):
    g, p, h = N_GROUPS, STATE_DIM, GROUP_CH
    dup = lambda x: jnp.concatenate([x, x], axis=-1)
    a2 = jnp.stack([dup(a_re), dup(a_im)]).reshape(2, 2, g, 1, 2 * p)
    bt = jnp.concatenate([b_re.transpose(0, 1, 3, 2), b_im.transpose(0, 1, 3, 2)], axis=-1)
    cc = jnp.concatenate([c_re, c_im], axis=-1)
    gb = GROUPS_PER_BLOCK
    return pl.pallas_call(
        _s5_prep_kernel,
        grid=(N_GROUP_BLOCKS,),
        in_specs=[pl.BlockSpec((2, 2, gb, 1, 2 * p), lambda i: (0, 0, i, 0, 0)),
                  pl.BlockSpec((2, gb, 1, 1), lambda i: (0, i, 0, 0)),
                  pl.BlockSpec((2, gb, h, 2 * p), lambda i: (0, i, 0, 0)),
                  pl.BlockSpec((2, gb, h, 2 * p), lambda i: (0, i, 0, 0))],
        out_specs=[pl.BlockSpec((gb, CHUNK_LANES, CHUNK_LANES + 2 * LANES), lambda i: (i, 0, 0)),
                   pl.BlockSpec((gb, CHUNK_LANES, 2 * LANES), lambda i: (i, 0, 0)),
                   pl.BlockSpec((gb, 2, LANES), lambda i: (i, 0, 0))],
        out_shape=[jax.ShapeDtypeStruct((g, CHUNK_LANES, CHUNK_LANES + 2 * LANES), BF16),
                   jax.ShapeDtypeStruct((g, CHUNK_LANES, 2 * LANES), BF16),
                   jax.ShapeDtypeStruct((g, 2, LANES), F32)],
        compiler_params=_params(1),
        name="s5_prep",
    )(a2, log_dt.reshape(2, g, 1, 1), bt, cc)


def _row_perm(to_time_major):
    n = PERM_SEQS * PERM_STEPS
    r = lax.broadcasted_iota(jnp.int32, (n, n), 0)
    c = lax.broadcasted_iota(jnp.int32, (n, n), 1)
    if to_time_major:
        src = jnp.bitwise_and(r, PERM_SEQS - 1) * PERM_STEPS + lax.shift_right_logical(r, 3)
    else:
        src = jnp.bitwise_and(r, PERM_STEPS - 1) * PERM_SEQS + lax.shift_right_logical(r, 5)
    return jnp.where(c == src, 1.0, 0.0).astype(BF16)


def _s5_in_kernel(x_ref, m_ref, w_ref, o_ref, *, tt):
    scale = 1.0 + m_ref[:, 1:2, :]
    shift = m_ref[:, 0:1, :]
    perm = _row_perm(True)
    n = PERM_SEQS * PERM_STEPS
    for k in range(tt // (2 * PERM_STEPS)):
        pieces = []
        for hf in range(2):
            t0 = (2 * k + hf) * PERM_STEPS
            h = x_ref[:, t0:t0 + PERM_STEPS, :] * scale + shift
            hb = h.reshape(n, D_MODEL).astype(BF16)
            pieces.append(_dot(perm, hb).astype(BF16))
        u = _dot(jnp.concatenate(pieces, axis=0), w_ref[...])
        o_ref[2 * k * PERM_STEPS:(2 * k + 2) * PERM_STEPS] = u.reshape(2 * PERM_STEPS, PERM_SEQS, D_MODEL)


def _two_group_steps(xa, xb, tt):
    ta, tb = xa.shape[1] // tt, xb.shape[1] // tt
    na = (xa.shape[0] // PERM_SEQS) * ta
    nb = (xb.shape[0] // PERM_SEQS) * tb

    def tile_a(i):
        i = jnp.minimum(i, na - 1)
        return i // ta, i % ta

    def tile_b(i):
        i = jnp.maximum(i - na, 0)
        return i // tb, i % tb

    return na, nb, tile_a, tile_b


def _s5_in_pair_kernel(xa_ref, ma_ref, xb_ref, mb_ref, w_ref, oa_ref, ob_ref, *, tt, n_first):
    @pl.when(pl.program_id(0) < n_first)
    def _():
        _s5_in_kernel(xa_ref, ma_ref, w_ref, oa_ref, tt=tt)

    @pl.when(pl.program_id(0) >= n_first)
    def _():
        _s5_in_kernel(xb_ref, mb_ref, w_ref, ob_ref, tt=tt)


def _s5_in(xa, ma, xb, mb, w_in, wl, tt):
    na, nb, tile_a, tile_b = _two_group_steps(xa, xb, tt)
    seq_major = lambda tile: (lambda i: (*tile(i), 0))
    time_major = lambda tile: (lambda i: (*tile(i)[::-1], 0))
    slot_only = lambda tile: (lambda i: (tile(i)[0], 0, 0))
    return pl.pallas_call(
        functools.partial(_s5_in_pair_kernel, tt=tt, n_first=na),
        grid=(na + nb,),
        in_specs=[pl.BlockSpec((PERM_SEQS, tt, D_MODEL), seq_major(tile_a)),
                  pl.BlockSpec((PERM_SEQS, 6, D_MODEL), slot_only(tile_a)),
                  pl.BlockSpec((PERM_SEQS, tt, D_MODEL), seq_major(tile_b)),
                  pl.BlockSpec((PERM_SEQS, 6, D_MODEL), slot_only(tile_b)),
                  pl.BlockSpec((None, D_MODEL, D_MODEL), lambda i: (wl, 0, 0))],
        out_specs=[pl.BlockSpec((tt, PERM_SEQS, D_MODEL), time_major(tile_a)),
                   pl.BlockSpec((tt, PERM_SEQS, D_MODEL), time_major(tile_b))],
        out_shape=[jax.ShapeDtypeStruct((x.shape[1], x.shape[0], D_MODEL), F32) for x in (xa, xb)],
        compiler_params=_params(1),
        name="s5_in",
    )(xa, ma, xb, mb, w_in)


def _block_transpose8(v):
    lane = lax.broadcasted_iota(jnp.int32, (SUBLANES, LANES), 1)
    v = list(v)
    for d in (4, 2, 1):
        low = jnp.bitwise_and(lane, GROUP_CH * d) == 0
        nxt = list(v)
        for i in range(8):
            if i & d == 0:
                a, b = v[i], v[i + d]
                if 2 * GROUP_CH * d == LANES:
                    r = pltpu.roll(jnp.where(low, b, a), LANES // 2, 1)
                    nxt[i] = jnp.where(low, a, r)
                    nxt[i + d] = jnp.where(low, r, b)
                else:
                    nxt[i] = jnp.where(low, a, pltpu.roll(b, GROUP_CH * d, 1))
                    nxt[i + d] = jnp.where(low, pltpu.roll(a, LANES - GROUP_CH * d, 1), b)
        v = nxt
    return v


def _s5_core_kernel(*refs, seq, nq, paired, has_fin):
    refs = list(refs)
    u_ref, w1_ref, w2_ref, lam_ref, d_ref = refs[:5]
    pos = 5
    h0_ref = None
    if paired:
        h0_ref = refs[pos]
        pos += 1
    y_ref = refs[pos]
    pos += 1
    fin_ref = refs[pos] if has_fin else None
    z_sc, yt_sc, bs_sc, sp_sc = refs[-4 - int(paired):][:4]
    sp2_sc = refs[-1] if paired else None
    gpb = GROUPS_PER_BLOCK
    nc = seq // S5_CHUNK
    ns = nq * SUBLANES
    half = STATE_DIM

    def fold(c, carry):
        for qi in range(nq):
            sl = slice(qi * SUBLANES, (qi + 1) * SUBLANES)
            r0 = pl.multiple_of(c * ns + qi * SUBLANES, SUBLANES)
            for hf in range(2):
                v = [u_ref[c * S5_CHUNK + hf * 8 + t, sl, :] for t in range(8)]
                w = _block_transpose8(v)
                for g in range(gpb):
                    z_sc[g, pl.ds(r0, SUBLANES), hf * LANES:(hf + 1) * LANES] = w[g]
        return carry

    lax.fori_loop(0, nc, fold, 0, unroll=FOLD_UNROLL // nq)

    for g in range(gpb):
        m1 = _dot(z_sc[g].astype(BF16), w1_ref[g])
        yt_sc[g] = m1[:, 0:CHUNK_LANES]
        bs_sc[g] = m1[:, CHUNK_LANES:]

    lane = lax.broadcasted_iota(jnp.int32, (SUBLANES, LANES), 1)
    fwd = jnp.bitwise_and(lane, half) == 0
    lam = [(jnp.broadcast_to(lam_ref[g, 0:1, :], (SUBLANES, LANES)),
            jnp.broadcast_to(lam_ref[g, 1:2, :], (SUBLANES, LANES))) for g in range(gpb)]

    def run_pass(init, dst):
        def body(i, carry):
            out = []
            for g in range(gpb):
                l_re, l_im = lam[g]
                for qi in range(nq):
                    s_re, s_im = carry[2 * (g * nq + qi)], carry[2 * (g * nq + qi) + 1]
                    ri = pl.ds(pl.multiple_of(i * ns + qi * SUBLANES, SUBLANES), SUBLANES)
                    rr = pl.ds(pl.multiple_of((nc - 1 - i) * ns + qi * SUBLANES, SUBLANES), SUBLANES)
                    dst[g, ri, 0:half] = s_re[:, 0:half]
                    dst[g, ri, LANES:LANES + half] = s_im[:, 0:half]
                    dst[g, rr, half:LANES] = s_re[:, half:LANES]
                    dst[g, rr, LANES + half:2 * LANES] = s_im[:, half:LANES]
                    x_re = jnp.where(fwd, bs_sc[g, ri, 0:LANES], bs_sc[g, rr, 0:LANES])
                    x_im = jnp.where(fwd, bs_sc[g, ri, LANES:2 * LANES], bs_sc[g, rr, LANES:2 * LANES])
                    out.append(l_re * s_re - l_im * s_im + x_re)
                    out.append(l_re * s_im + l_im * s_re + x_im)
            return tuple(out)

        return lax.fori_loop(0, nc, body, init)

    if paired:
        init = []
        for g in range(gpb):
            init += [h0_ref[g, 0], h0_ref[g, 1]]
        mid = run_pass(tuple(init), sp_sc)
        handed = [jnp.where(fwd, pltpu.roll(s, 1, 0), pltpu.roll(s, SUBLANES - 1, 0)) for s in mid]
        fin = run_pass(tuple(handed), sp2_sc)
    else:
        zero = jnp.zeros((SUBLANES, LANES), F32)
        fin = run_pass((zero,) * (2 * gpb * nq), sp_sc)

    if has_fin:
        for g in range(gpb):
            for qi in range(nq):
                sl = slice(qi * SUBLANES, (qi + 1) * SUBLANES)
                fin_ref[g, 0, sl, :] = fin[2 * (g * nq + qi)]
                fin_ref[g, 1, sl, :] = fin[2 * (g * nq + qi) + 1]

    if paired:
        shape = (nc * ns, 2 * LANES)
        row = lax.broadcasted_iota(jnp.int32, shape, 0)
        col = lax.broadcasted_iota(jnp.int32, shape, 1)
        first_pass = (jnp.bitwise_and(row, 1) == 0) == (jnp.bitwise_and(col, half) == 0)
    for g in range(gpb):
        states = sp_sc[g]
        if paired:
            states = jnp.where(first_pass, states, sp2_sc[g])
        yt_sc[g] = yt_sc[g] + _dot_nt(states.astype(BF16), w2_ref[g])

    d = jnp.broadcast_to(d_ref[...], (SUBLANES, LANES))

    def unfold(c, carry):
        for qi in range(nq):
            sl = slice(qi * SUBLANES, (qi + 1) * SUBLANES)
            r0 = pl.multiple_of(c * ns + qi * SUBLANES, SUBLANES)
            for hf in range(2):
                w = [yt_sc[g, pl.ds(r0, SUBLANES), hf * LANES:(hf + 1) * LANES] for g in range(gpb)]
                v = _block_transpose8(w)
                for t in range(8):
                    step = c * S5_CHUNK + hf * 8 + t
                    y_ref[step, sl, :] = v[t] + d * u_ref[step, sl, :]
        return carry

    lax.fori_loop(0, nc, unfold, 0, unroll=FOLD_UNROLL // nq)


def _s5_core(u, w1, w2, lam, d_skip, h0, want_final):
    seq, slots, _ = u.shape
    paired = h0 is not None
    nq = 1 if paired else slots // SUBLANES
    ns = nq * SUBLANES
    gpb = GROUPS_PER_BLOCK
    rows = (seq // S5_CHUNK) * ns
    u_spec = pl.BlockSpec((seq, ns, LANES), lambda s, g: (0, s, g))
    in_specs = [u_spec,
                pl.BlockSpec((gpb, CHUNK_LANES, CHUNK_LANES + 2 * LANES), lambda s, g: (g, 0, 0)),
                pl.BlockSpec((gpb, CHUNK_LANES, 2 * LANES), lambda s, g: (g, 0, 0)),
                pl.BlockSpec((gpb, 2, LANES), lambda s, g: (g, 0, 0)),
                pl.BlockSpec((1, LANES), lambda s, g: (0, g))]
    args = [u, w1, w2, lam, d_skip.reshape(1, D_MODEL)]
    if paired:
        in_specs.append(pl.BlockSpec((gpb, 2, SUBLANES, LANES), lambda s, g: (g, 0, s, 0)))
        args.append(h0)
    out_specs = [u_spec]
    out_shape = [jax.ShapeDtypeStruct(u.shape, F32)]
    if want_final:
        out_specs.append(pl.BlockSpec((gpb, 2, ns, LANES), lambda s, g: (g, 0, s, 0)))
        out_shape.append(jax.ShapeDtypeStruct((N_GROUPS, 2, slots, LANES), F32))
    res = pl.pallas_call(
        functools.partial(_s5_core_kernel, seq=seq, nq=nq, paired=paired, has_fin=want_final),
        grid=(slots // ns, N_GROUP_BLOCKS),
        in_specs=in_specs,
        out_specs=out_specs,
        out_shape=out_shape,
        scratch_shapes=[pltpu.VMEM((gpb, rows, CHUNK_LANES), F32) for _ in range(5 if paired else 4)],
        compiler_params=_params(2),
        name="s5_core",
    )(*args)
    return (res[0], res[1]) if want_final else (res[0], None)


def _s5_out_kernel(y_ref, x_ref, m_ref, wab_ref, wo_ref, lng_ref, lnb_ref, o_ref, *, tt, f):
    gate = m_ref[:, 2:3, :]
    perm = _row_perm(False)
    n = PERM_SEQS * PERM_STEPS
    lng = lng_ref[...].reshape(1, 1, D_MODEL)
    lnb = lnb_ref[...].reshape(1, 1, D_MODEL)
    for k in range(tt // (2 * PERM_STEPS)):
        t0 = 2 * k * PERM_STEPS
        ht = _gelu_tanh(y_ref[t0:t0 + 2 * PERM_STEPS].reshape(2 * n, D_MODEL)).astype(BF16)
        hb = jnp.concatenate([_dot(perm, ht[0:n]).astype(BF16), _dot(perm, ht[n:2 * n]).astype(BF16)],
                             axis=0)
        acc = None
        for c0 in range(0, f, MXU_TILE):
            val = _dot(hb, wab_ref[:, c0:c0 + MXU_TILE])
            gte = _dot(hb, wab_ref[:, f + c0:f + c0 + MXU_TILE])
            z = (val * _sigmoid(gte)).astype(BF16)
            part = _dot(z, wo_ref[c0:c0 + MXU_TILE, :])
            acc = part if acc is None else acc + part
        for hf in range(2):
            ts = slice(t0 + hf * PERM_STEPS, t0 + (hf + 1) * PERM_STEPS)
            mix = acc[hf * n:(hf + 1) * n].reshape(PERM_SEQS, PERM_STEPS, D_MODEL)
            r = DEEPNORM_ALPHA * x_ref[:, ts, :] + gate * mix
            o_ref[:, ts, :] = _layer_norm(r, lng, lnb)


def _s5_out_pair_kernel(ya_ref, xa_ref, ma_ref, yb_ref, xb_ref, mb_ref, wab_ref, wo_ref, lng_ref, lnb_ref,
                        oa_ref, ob_ref, *, tt, f, n_first):
    @pl.when(pl.program_id(0) < n_first)
    def _():
        _s5_out_kernel(ya_ref, xa_ref, ma_ref, wab_ref, wo_ref, lng_ref, lnb_ref, oa_ref, tt=tt, f=f)

    @pl.when(pl.program_id(0) >= n_first)
    def _():
        _s5_out_kernel(yb_ref, xb_ref, mb_ref, wab_ref, wo_ref, lng_ref, lnb_ref, ob_ref, tt=tt, f=f)


def _s5_out(ya, xa, ma, yb, xb, mb, w_glu, w_out, wl, ln_g, ln_b, tt):
    f = w_out.shape[1]
    na, nb, tile_a, tile_b = _two_group_steps(xa, xb, tt)
    seq_major = lambda tile: pl.BlockSpec((PERM_SEQS, tt, D_MODEL), lambda i: (*tile(i), 0))
    time_major = lambda tile: pl.BlockSpec((tt, PERM_SEQS, D_MODEL), lambda i: (*tile(i)[::-1], 0))
    slot_only = lambda tile: pl.BlockSpec((PERM_SEQS, 6, D_MODEL), lambda i: (tile(i)[0], 0, 0))
    resident = pl.Buffered(1)
    return pl.pallas_call(
        functools.partial(_s5_out_pair_kernel, tt=tt, f=f, n_first=na),
        grid=(na + nb,),
        in_specs=[time_major(tile_a), seq_major(tile_a), slot_only(tile_a),
                  time_major(tile_b), seq_major(tile_b), slot_only(tile_b),
                  pl.BlockSpec((None, D_MODEL, 2 * f), lambda i: (wl, 0, 0), pipeline_mode=resident),
                  pl.BlockSpec((None, f, D_MODEL), lambda i: (wl, 0, 0), pipeline_mode=resident),
                  pl.BlockSpec((1, D_MODEL), lambda i: (0, 0)),
                  pl.BlockSpec((1, D_MODEL), lambda i: (0, 0))],
        out_specs=[seq_major(tile_a), seq_major(tile_b)],
        out_shape=[jax.ShapeDtypeStruct(xa.shape, F32), jax.ShapeDtypeStruct(xb.shape, F32)],
        compiler_params=_params(1),
        name="s5_out",
    )(ya, xa, ma, yb, xb, mb, w_glu, w_out, ln_g.reshape(1, D_MODEL), ln_b.reshape(1, D_MODEL))


def _ffn_kernel(xp_ref, xs_ref, m_ref, wi_hbm, wo_hbm, lng_ref, lnb_ref, op_ref, os_ref,
                h_sc, wab_sc, wo_sc, stage_a, stage_b, stage_o, sem, *, f, wl, n_first):
    n_slab = f // MXU_TILE
    step = pl.program_id(0)
    m = m_ref[0]

    def slab_copies(k, slot):
        c0 = k * MXU_TILE
        return (pltpu.make_async_copy(wi_hbm.at[wl, :, pl.ds(c0, MXU_TILE)], stage_a.at[slot], sem.at[0, slot]),
                pltpu.make_async_copy(wi_hbm.at[wl, :, pl.ds(f + c0, MXU_TILE)], stage_b.at[slot], sem.at[1, slot]),
                pltpu.make_async_copy(wo_hbm.at[wl, pl.ds(c0, MXU_TILE), :], stage_o.at[slot], sem.at[2, slot]))

    def run(load_weights, x_ref, o_ref):
        h = x_ref[...] * (1.0 + m[4:5]) + m[3:4]
        h_sc[...] = h.astype(BF16)
        ahead = FFN_STAGE_SLOTS - 1
        if load_weights:
            for k in range(min(ahead, n_slab)):
                for cp in slab_copies(k, k % FFN_STAGE_SLOTS):
                    cp.start()
        acc = None
        for k in range(n_slab):
            c0 = k * MXU_TILE
            if load_weights:
                slot = k % FFN_STAGE_SLOTS
                if k + ahead < n_slab:
                    for cp in slab_copies(k + ahead, (k + ahead) % FFN_STAGE_SLOTS):
                        cp.start()
                for cp in slab_copies(k, slot):
                    cp.wait()
                wab_sc[:, c0:c0 + MXU_TILE] = stage_a[slot].astype(BF16)
                wab_sc[:, f + c0:f + c0 + MXU_TILE] = stage_b[slot].astype(BF16)
                wo_sc[c0:c0 + MXU_TILE, :] = stage_o[slot].astype(BF16)
            a = _dot(h_sc[...], wab_sc[:, c0:c0 + MXU_TILE])
            b = _dot(h_sc[...], wab_sc[:, f + c0:f + c0 + MXU_TILE])
            z = (a * _sigmoid(a)) * b
            part = _dot(z.astype(BF16), wo_sc[c0:c0 + MXU_TILE, :])
            acc = part if acc is None else acc + part
        r = DEEPNORM_ALPHA * x_ref[...] + m[5:6] * acc
        o_ref[...] = _layer_norm(r, lng_ref[...], lnb_ref[...])

    @pl.when(step == 0)
    def _():
        run(True, xp_ref, op_ref)

    @pl.when(jnp.logical_and(step > 0, step < n_first))
    def _():
        run(False, xp_ref, op_ref)

    @pl.when(step >= n_first)
    def _():
        run(False, xs_ref, os_ref)


def _ffn(xp, xs, mods, seq_s, w_in, w_out, wl, ln_g, ln_b, tm):
    f = w_out.shape[1]
    n_first = xp.shape[0] // tm
    n_second = xs.shape[0] // tm
    tiles_per_seq = seq_s // tm
    first_tile = lambda i: (jnp.minimum(i, n_first - 1), 0)
    second_tile = lambda i: (jnp.maximum(i - n_first, 0), 0)
    cond_row = lambda i: (jnp.where(i < n_first, 0, 1 + jnp.maximum(i - n_first, 0) // tiles_per_seq), 0, 0)
    return pl.pallas_call(
        functools.partial(_ffn_kernel, f=f, wl=wl, n_first=n_first),
        grid=(n_first + n_second,),
        in_specs=[pl.BlockSpec((tm, D_MODEL), first_tile),
                  pl.BlockSpec((tm, D_MODEL), second_tile),
                  pl.BlockSpec((1, 6, D_MODEL), cond_row),
                  pl.BlockSpec(memory_space=pl.ANY),
                  pl.BlockSpec(memory_space=pl.ANY),
                  pl.BlockSpec((1, D_MODEL), lambda i: (0, 0)),
                  pl.BlockSpec((1, D_MODEL), lambda i: (0, 0))],
        out_specs=[pl.BlockSpec((tm, D_MODEL), first_tile), pl.BlockSpec((tm, D_MODEL), second_tile)],
        out_shape=[jax.ShapeDtypeStruct(xp.shape, F32), jax.ShapeDtypeStruct(xs.shape, F32)],
        scratch_shapes=[pltpu.VMEM((tm, D_MODEL), BF16),
                        pltpu.VMEM((D_MODEL, 2 * f), BF16),
                        pltpu.VMEM((f, D_MODEL), BF16),
                        pltpu.VMEM((FFN_STAGE_SLOTS, D_MODEL, MXU_TILE), F32),
                        pltpu.VMEM((FFN_STAGE_SLOTS, D_MODEL, MXU_TILE), F32),
                        pltpu.VMEM((FFN_STAGE_SLOTS, MXU_TILE, D_MODEL), F32),
                        pltpu.SemaphoreType.DMA((3, FFN_STAGE_SLOTS))],
        compiler_params=_params(1),
        name="ffn",
    )(xp, xs, mods, w_in, w_out, ln_g.reshape(1, D_MODEL), ln_b.reshape(1, D_MODEL))


def _qkv_kernel(*refs, rope):
    if rope:
        x_ref, m_ref, w_ref, qg_ref, kg_ref, cos_ref, sin_ref = refs[:7]
    else:
        x_ref, m_ref, w_ref, qg_ref, kg_ref = refs[:5]
    q_ref, k_ref, v_ref = refs[-3:]
    rows = x_ref.shape[0]
    m = m_ref[0]
    h = x_ref[...] * (1.0 + m[1:2]) + m[0:1]
    qkv = _dot(h.astype(BF16), w_ref[...])
    if rope:
        cos = cos_ref[...]
        sin = sin_ref[...]
        lane = lax.broadcasted_iota(jnp.int32, cos.shape, 1)
        first = jnp.bitwise_and(lane, AXIS_PAIRS) == 0
    for hd in range(N_HEADS + N_KV_HEADS):
        xh = qkv[:, hd * HEAD_DIM:(hd + 1) * HEAD_DIM]
        gain = qg_ref[...] if hd < N_HEADS else kg_ref[...]
        n = xh * lax.rsqrt(jnp.mean(xh * xh, axis=-1, keepdims=True) + RMS_EPS) * gain
        if rope:
            up = pltpu.roll(n, HEAD_DIM - AXIS_PAIRS, 1)
            down = pltpu.roll(n, AXIS_PAIRS, 1)
            n = n * cos + jnp.where(first, up, down) * sin
        if hd < N_HEADS:
            q_ref[:, hd * HEAD_DIM:(hd + 1) * HEAD_DIM] = (n * Q_SCALE_LOG2).astype(BF16)
        else:
            k_ref[pl.ds(hd - N_HEADS, rows, stride=N_KV_HEADS), :] = n
    for g in range(N_KV_HEADS):
        v_ref[pl.ds(g, rows, stride=N_KV_HEADS), :] = (
            qkv[:, D_Q + D_KV + g * HEAD_DIM:D_Q + D_KV + (g + 1) * HEAD_DIM])


def _rope_tables(seq):
    pos = jnp.arange(seq, dtype=jnp.int32)
    row = (pos // GRID_W).astype(F32)
    col = (pos % GRID_W).astype(F32)
    inv = ROPE_THETA ** (-jnp.arange(AXIS_PAIRS, dtype=F32) / AXIS_PAIRS)
    ar = row[:, None] * inv
    ac = col[:, None] * inv
    cos = jnp.concatenate([jnp.cos(ar), jnp.cos(ar), jnp.cos(ac), jnp.cos(ac)], axis=-1)
    sin = jnp.concatenate([-jnp.sin(ar), jnp.sin(ar), -jnp.sin(ac), jnp.sin(ac)], axis=-1)
    return cos, sin


def _qkv_pair_kernel(*refs, n_first):
    xp_ref, mp_ref, xs_ref, ms_ref, w_ref, qg_ref, kg_ref, cos_ref, sin_ref = refs[:9]
    qp_ref, kp_ref, vp_ref, qs_ref, ks_ref, vs_ref = refs[-6:]

    @pl.when(pl.program_id(0) < n_first)
    def _():
        _qkv_kernel(xp_ref, mp_ref, w_ref, qg_ref, kg_ref, qp_ref, kp_ref, vp_ref, rope=False)

    @pl.when(pl.program_id(0) >= n_first)
    def _():
        _qkv_kernel(xs_ref, ms_ref, w_ref, qg_ref, kg_ref, cos_ref, sin_ref, qs_ref, ks_ref, vs_ref, rope=True)


def _qkv(xp, mp, seq_p, xs, ms, seq_s, w_qkv, wl, q_gain, k_gain, kv_prev):
    nbp, nbs = xp.shape[0] // seq_p, xs.shape[0] // seq_s
    tm_p, tm_s = seq_p, seq_s // 2
    nts = seq_s // tm_s
    n_first = nbp
    first = lambda i: jnp.minimum(i, n_first - 1)
    second = lambda i: jnp.maximum(i - n_first, 0)
    cos, sin = _rope_tables(seq_s)
    kv_layers = kv_prev[0].shape[1]
    in_specs = [pl.BlockSpec((tm_p, D_MODEL), lambda i: (first(i), 0)),
                pl.BlockSpec((1, 6, D_MODEL), lambda i: (0, 0, 0)),
                pl.BlockSpec((tm_s, D_MODEL), lambda i: (second(i), 0)),
                pl.BlockSpec((1, 6, D_MODEL), lambda i: (second(i) // nts, 0, 0)),
                pl.BlockSpec((None, D_MODEL, QKV_DIM), lambda i: (wl, 0, 0)),
                pl.BlockSpec((1, HEAD_DIM), lambda i: (0, 0)),
                pl.BlockSpec((1, HEAD_DIM), lambda i: (0, 0)),
                pl.BlockSpec((tm_s, HEAD_DIM), lambda i: (second(i) % nts, 0)),
                pl.BlockSpec((tm_s, HEAD_DIM), lambda i: (second(i) % nts, 0)),
                pl.BlockSpec(memory_space=pl.ANY),
                pl.BlockSpec(memory_space=pl.ANY)]
    args = [xp, mp, xs, ms, w_qkv, q_gain.reshape(1, HEAD_DIM), k_gain.reshape(1, HEAD_DIM), cos, sin,
            kv_prev[0], kv_prev[1]]
    kvp_spec = pl.BlockSpec((None, None, tm_p * N_KV_HEADS, HEAD_DIM), lambda i: (first(i), wl, 0, 0))
    kvs_spec = pl.BlockSpec((None, None, tm_s * N_KV_HEADS, HEAD_DIM),
                            lambda i: (second(i) // nts, 0, second(i) % nts, 0))
    kvp_shape = jax.ShapeDtypeStruct((nbp, kv_layers, seq_p * N_KV_HEADS, HEAD_DIM), F32)
    kvs_shape = jax.ShapeDtypeStruct((nbs, 1, seq_s * N_KV_HEADS, HEAD_DIM), F32)
    return pl.pallas_call(
        functools.partial(_qkv_pair_kernel, n_first=n_first),
        grid=(n_first + nbs * nts,),
        in_specs=in_specs,
        out_specs=[pl.BlockSpec((tm_p, D_Q), lambda i: (first(i), 0)), kvp_spec, kvp_spec,
                   pl.BlockSpec((tm_s, D_Q), lambda i: (second(i), 0)), kvs_spec, kvs_spec],
        out_shape=[jax.ShapeDtypeStruct((xp.shape[0], D_Q), BF16), kvp_shape, kvp_shape,
                   jax.ShapeDtypeStruct((xs.shape[0], D_Q), BF16), kvs_shape, kvs_shape],
        input_output_aliases={9: 1, 10: 2},
        compiler_params=_params(1),
        name="qkv",
    )(*args)


def _attn_kernel(*refs, has_cache, stack):
    if has_cache:
        (q_ref, k_ref, v_ref, ck_ref, cv_ref, x_ref, m_ref, wo_ref, lng_ref, lnb_ref,
         o_ref, oh_sc) = refs
    else:
        q_ref, k_ref, v_ref, x_ref, m_ref, wo_ref, lng_ref, lnb_ref, o_ref, oh_sc = refs
    for g in range(N_KV_HEADS):
        sl = slice(g * HEAD_DIM, (g + 1) * HEAD_DIM)
        head_rows = lambda ref: ref[pl.ds(g, ref.shape[0] // N_KV_HEADS, stride=N_KV_HEADS), :]
        kg = head_rows(k_ref).astype(BF16)
        vg = head_rows(v_ref).astype(BF16)
        if has_cache:
            ckg = head_rows(ck_ref).astype(BF16)
            cvg = head_rows(cv_ref).astype(BF16)
        tq = q_ref.shape[0]
        for r0 in range(0, KV_REP, stack):
            heads = [g * KV_REP + r0 + r for r in range(stack)]
            qh = jnp.concatenate([q_ref[:, h * HEAD_DIM:(h + 1) * HEAD_DIM] for h in heads], axis=0)
            s1 = _dot_nt(qh, kg)
            mx = jnp.max(s1, axis=-1, keepdims=True)
            if has_cache:
                s2 = _dot_nt(qh, ckg)
                mx = jnp.maximum(mx, jnp.max(s2, axis=-1, keepdims=True))
            p1 = jnp.exp2(s1 - mx)
            den = jnp.sum(p1, axis=-1, keepdims=True)
            o = _dot(p1.astype(BF16), vg)
            if has_cache:
                p2 = jnp.exp2(s2 - mx)
                den = den + jnp.sum(p2, axis=-1, keepdims=True)
                o = o + _dot(p2.astype(BF16), cvg)
            o = (o / den).astype(BF16)
            for r, h in enumerate(heads):
                oh_sc[:, h * HEAD_DIM:(h + 1) * HEAD_DIM] = o[r * tq:(r + 1) * tq]
    mix = _dot(oh_sc[...], wo_ref[...])
    m = m_ref[0]
    r = DEEPNORM_ALPHA * x_ref[...] + m[2:3] * mix
    o_ref[...] = _layer_norm(r, lng_ref[...], lnb_ref[...])


def _attention(q, k, v, kv_slot, cache_k, cache_v, layer_j, x, mods, w_o, ln_g, ln_b, nb, seq, tq):
    nt = seq // tq
    per_batch = mods.shape[0] > 1
    has_cache = cache_k is not None
    tok = pl.BlockSpec((tq, D_MODEL), lambda b, t: (b * nt + t, 0))
    kv = pl.BlockSpec((None, None, seq * N_KV_HEADS, HEAD_DIM), lambda b, t: (b, kv_slot, 0, 0))
    in_specs, args = [tok, kv, kv], [q, k, v]
    if has_cache:
        past = cache_k.shape[2]
        cspec = pl.BlockSpec((None, None, past * N_KV_HEADS, HEAD_DIM), lambda b, t: (b, layer_j, 0, 0))
        in_specs += [cspec, cspec]
        flat = (cache_k.shape[0], cache_k.shape[1], past * N_KV_HEADS, HEAD_DIM)
        args += [cache_k.reshape(flat), cache_v.reshape(flat)]
    in_specs += [tok,
                 pl.BlockSpec((1, 6, D_MODEL), lambda b, t: (b if per_batch else 0, 0, 0)),
                 pl.BlockSpec((None, D_Q, D_MODEL), lambda b, t: (layer_j, 0, 0)),
                 pl.BlockSpec((1, D_MODEL), lambda b, t: (0, 0)),
                 pl.BlockSpec((1, D_MODEL), lambda b, t: (0, 0))]
    args += [x, mods, w_o, ln_g.reshape(1, D_MODEL), ln_b.reshape(1, D_MODEL)]
    return pl.pallas_call(
        functools.partial(_attn_kernel, has_cache=has_cache, stack=1 if has_cache else KV_REP),
        grid=(nb, nt),
        in_specs=in_specs,
        out_specs=tok,
        out_shape=jax.ShapeDtypeStruct((nb * seq, D_MODEL), F32),
        scratch_shapes=[pltpu.VMEM((tq, D_Q), BF16)],
        compiler_params=_params(2),
        name="attn_cache" if has_cache else "attn",
    )(*args)


def _latent_h0(st):
    f = st[:, 0].transpose(2, 1, 0, 3)
    b = st[:, 1].transpose(2, 1, 0, 3)
    z = jnp.zeros_like(f)
    even = jnp.concatenate([f, z], axis=-1)
    odd = jnp.concatenate([z, b], axis=-1)
    h0 = jnp.stack([even, odd], axis=3)
    return h0.reshape(N_GROUPS, 2, 2 * st.shape[0], 2 * STATE_DIM)


def kernel(x_prompt, x_sample, c, cache_k, cache_v, state_s5, c_ctx, w_mod, b_mod, ln_g, ln_b, w_s5_in, s5_a_re, s5_a_im, s5_log_dt, s5_b_re, s5_b_im, s5_c_re, s5_c_im, s5_d, w_s5_glu, w_s5_out, w_qkv, q_norm_g, k_norm_g, w_o, w_ffn_in, w_ffn_out):
    nbp, seqp, _ = x_prompt.shape
    nbs, seqs, _ = x_sample.shape
    xp = x_prompt.reshape(nbp * seqp, D_MODEL)
    xs = x_sample.reshape(nbs * seqs, D_MODEL)
    s5_slots_s, s5_seq_s = 2 * nbs, seqs // 2

    cond = jnp.concatenate([c_ctx[None, :], c, jnp.zeros((8 - 1 - nbs, D_MODEL), F32)], axis=0)
    mods = _adaln(cond, w_mod, b_mod)
    mods_all = mods.reshape(DEPTH, 8, 6, D_MODEL)
    mods_p = mods[:, 0:1].reshape(DEPTH, 1, 6, D_MODEL)
    mods_s = mods[:, 1:1 + nbs].reshape(DEPTH, nbs, 6, D_MODEL)

    w_in, w_glu, w_out = w_s5_in.astype(BF16), w_s5_glu.astype(BF16), w_s5_out.astype(BF16)
    wq, wo = w_qkv.astype(BF16), w_o.astype(BF16)

    kv_rows = (nbp, DEPTH // 2, seqp * N_KV_HEADS, HEAD_DIM)
    new_kv, new_s = (jnp.zeros(kv_rows, F32), jnp.zeros(kv_rows, F32)), []
    for layer in range(DEPTH):
        j = layer // 2
        mp, ms = mods_p[layer], mods_s[layer]
        lg0, lb0, lg1, lb1 = ln_g[layer, 0], ln_b[layer, 0], ln_g[layer, 1], ln_b[layer, 1]
        if layer % 2 == 0:
            w1, w2, lam = _s5_tables(s5_a_re[j], s5_a_im[j], s5_log_dt[j], s5_b_re[j], s5_b_im[j],
                                     s5_c_re[j], s5_c_im[j])
            mp_slots = jnp.broadcast_to(mp, (nbp, 6, D_MODEL))
            ms_slots = jnp.repeat(ms, 2, axis=0)
            xp3 = xp.reshape(nbp, seqp, D_MODEL)
            xs3 = xs.reshape(s5_slots_s, s5_seq_s, D_MODEL)
            up, us = _s5_in(xp3, mp_slots, xs3, ms_slots, w_in, j, 128)
            yp, fin = _s5_core(up, w1, w2, lam, s5_d[j], None, True)
            ys, _ = _s5_core(us, w1, w2, lam, s5_d[j], _latent_h0(state_s5[:, j]), False)
            fin = fin.reshape(N_GROUPS, 2, nbp, 2, STATE_DIM)
            new_s.append(fin.transpose(2, 3, 1, 0, 4))
            xp, xs = _s5_out(yp, xp3, mp_slots, ys, xs3, ms_slots, w_glu, w_out, j, lg0, lb0, 64)
            xp = xp.reshape(nbp * seqp, D_MODEL)
            xs = xs.reshape(nbs * seqs, D_MODEL)
        else:
            qp, kp, vp, qs, ks, vs = _qkv(xp, mp, seqp, xs, ms, seqs, wq, j, q_norm_g[j], k_norm_g[j], new_kv)
            new_kv = (kp, vp)
            xp = _attention(qp, kp, vp, j, None, None, j, xp, mp, wo, lg0, lb0, nbp, seqp, 256)
            xs = _attention(qs, ks, vs, 0, cache_k, cache_v, j, xs, ms, wo, lg0, lb0, nbs, seqs, 512)
        xp, xs = _ffn(xp, xs, mods_all[layer], seqs, w_ffn_in, w_ffn_out, layer, lg1, lb1, 512)

    y_prompt = xp.reshape(nbp, seqp, D_MODEL)
    y_sample = xs.reshape(nbs, seqs, D_MODEL)
    kv_out = (nbp, DEPTH // 2, seqp, N_KV_HEADS, HEAD_DIM)
    return (y_prompt, y_sample, new_kv[0].reshape(kv_out), new_kv[1].reshape(kv_out),
            jnp.stack(new_s, axis=1))
```

```python
import functools
import math

import jax
import jax.numpy as jnp
from jax import lax
from jax.experimental import pallas as pl
from jax.experimental.pallas import tpu as pltpu

F32 = jnp.float32
BF16 = jnp.bfloat16

D_MODEL = 1024
DEPTH = 4
N_GROUPS = 64
GROUP_CH = 16
STATE_DIM = 64
HEAD_DIM = 128
N_HEADS = 8
N_KV_HEADS = 2
KV_REP = N_HEADS // N_KV_HEADS
D_Q = N_HEADS * HEAD_DIM
D_KV = N_KV_HEADS * HEAD_DIM
QKV_DIM = D_Q + 2 * D_KV
GRID_W = 64
ROPE_THETA = 10000.0
AXIS_PAIRS = HEAD_DIM // 4
ATTN_SCALE = HEAD_DIM ** -0.5
Q_SCALE_LOG2 = ATTN_SCALE * math.log2(math.e)
DEEPNORM_ALPHA = (2.0 * DEPTH) ** 0.25
LN_EPS = 1e-6
RMS_EPS = 1e-6

V7X_VMEM_LIMIT_BYTES = 56 * 1024 * 1024
LANES = 128
SUBLANES = 8
MXU_TILE = 256
FFN_STAGE_SLOTS = 2
GROUPS_PER_BLOCK = LANES // GROUP_CH
N_GROUP_BLOCKS = N_GROUPS // GROUPS_PER_BLOCK
S5_CHUNK = MXU_TILE // GROUP_CH
CHUNK_LANES = S5_CHUNK * GROUP_CH
FOLD_UNROLL = 4
PERM_SEQS = SUBLANES
PERM_STEPS = MXU_TILE // PERM_SEQS
NT_DIMS = (((1,), (1,)), ((), ()))


def _params(n_axes):
    return pltpu.CompilerParams(dimension_semantics=("arbitrary",) * n_axes,
                                vmem_limit_bytes=V7X_VMEM_LIMIT_BYTES)


def _sigmoid(x):
    return 1.0 / (1.0 + jnp.exp(-x))


def _gelu_tanh(x):
    cdf = 0.5 * (1.0 + jnp.tanh(math.sqrt(2.0 / math.pi) * (x + 0.044715 * (x * x * x))))
    return x * cdf


def _layer_norm(r, g, b):
    mu = jnp.mean(r, axis=-1, keepdims=True)
    d = r - mu
    var = jnp.mean(d * d, axis=-1, keepdims=True)
    return d * lax.rsqrt(var + LN_EPS) * g + b


def _dot(a, b):
    return jnp.dot(a, b, preferred_element_type=F32)


def _dot_nt(a, b):
    return lax.dot_general(a, b, NT_DIMS, preferred_element_type=F32)


def _adaln_kernel(c_ref, w_ref, b_ref, o_ref):
    c = c_ref[...]
    s = c * _sigmoid(c)
    o_ref[0] = _dot(s.astype(BF16), w_ref[0].astype(BF16)) + b_ref[0]


def _adaln(cond, w_mod, b_mod):
    tn = 1536
    n = 6 * D_MODEL
    return pl.pallas_call(
        _adaln_kernel,
        grid=(DEPTH, n // tn),
        in_specs=[pl.BlockSpec((8, D_MODEL), lambda l, j: (0, 0)),
                  pl.BlockSpec((1, D_MODEL, tn), lambda l, j: (l, 0, j)),
                  pl.BlockSpec((1, 1, tn), lambda l, j: (l, 0, j))],
        out_specs=pl.BlockSpec((1, 8, tn), lambda l, j: (l, 0, j)),
        out_shape=jax.ShapeDtypeStruct((DEPTH, 8, n), F32),
        compiler_params=_params(2),
        name="adaln",
    )(cond, w_mod, b_mod.reshape(DEPTH, 1, n))


def _s5_prep_kernel(a_ref, ldt_ref, bt_ref, c_ref, w1_ref, w2_ref, lam_ref):
    half = STATE_DIM
    lane1 = lax.broadcasted_iota(jnp.int32, (1, LANES), 1)
    sgn1 = jnp.where(lane1 < half, -1.0, 1.0)
    lane_h = lax.broadcasted_iota(jnp.int32, (GROUP_CH, LANES), 1)
    first_h = lane_h < half
    conj_h = jnp.where(first_h, 1.0, -1.0)
    lane_c = lax.broadcasted_iota(jnp.int32, (GROUP_CH, CHUNK_LANES), 1)
    t = S5_CHUNK

    def cmul(pr, pi, x):
        return pr * x + (pi * sgn1) * pltpu.roll(x, half, 1)

    def pack_states(f, b):
        return (jnp.where(first_h, f, pltpu.roll(b, half, 1)),
                jnp.where(first_h, pltpu.roll(f, half, 1), b))

    def group(g, carry):
        kt, qs, cks, lam_t = [], [], [], []
        for d in range(2):
            a_re = a_ref[0, d, g]
            a_im = a_ref[1, d, g]
            dt = jnp.exp(ldt_ref[d, g])
            mag = jnp.exp(dt * a_re)
            lr = mag * jnp.cos(dt * a_im)
            li = mag * jnp.sin(dt * a_im)
            den = a_re * a_re + a_im * a_im
            nr = lr - 1.0
            k_re = (nr * a_re + li * a_im) / den
            k_im = (li * a_re - nr * a_im) / den
            bb = cmul(k_re, k_im, bt_ref[d, g])
            cc = c_ref[d, g]
            pr = jnp.ones((1, LANES), F32)
            pi = jnp.zeros((1, LANES), F32)
            ck, q = [], []
            for k in range(t + 1):
                ck.append(cmul(pr, pi, cc))
                if k < t:
                    q.append(cmul(pr, pi, bb))
                    pr, pi = pr * lr - pi * li, pr * li + pi * lr
            lam_t.append((pr, pi))
            order = range(t) if d == 0 else range(t - 1, -1, -1)
            rhs = jnp.concatenate([ck[k] for k in order], axis=0).astype(BF16)
            kt.append(_dot_nt((bb * conj_h).astype(BF16), rhs))
            qs.append(q)
            cks.append(ck)
        for j in range(t):
            rows = slice(j * GROUP_CH, (j + 1) * GROUP_CH)
            tf = kt[0] if j == 0 else pltpu.roll(kt[0], GROUP_CH * j, 1)
            tf = jnp.where(lane_c >= GROUP_CH * j, tf, 0.0)
            back = GROUP_CH * (t - 1 - j)
            tb = kt[1] if back == 0 else pltpu.roll(kt[1], CHUNK_LANES - back, 1)
            tb = jnp.where(lane_c < GROUP_CH * (j + 1), tb, 0.0)
            w1_ref[g, rows, 0:CHUNK_LANES] = (tf + tb).astype(BF16)
            s_re, s_im = pack_states(qs[0][t - 1 - j], qs[1][j])
            w1_ref[g, rows, CHUNK_LANES:CHUNK_LANES + LANES] = s_re.astype(BF16)
            w1_ref[g, rows, CHUNK_LANES + LANES:CHUNK_LANES + 2 * LANES] = s_im.astype(BF16)
            c_re, c_im = pack_states(cks[0][j + 1] * conj_h, cks[1][t - j] * conj_h)
            w2_ref[g, rows, 0:LANES] = c_re.astype(BF16)
            w2_ref[g, rows, LANES:2 * LANES] = c_im.astype(BF16)
        fwd1 = lane1 < half
        lam_ref[g, 0:1, :] = jnp.where(fwd1, lam_t[0][0], lam_t[1][0])
        lam_ref[g, 1:2, :] = jnp.where(fwd1, lam_t[0][1], lam_t[1][1])
        return carry

    lax.fori_loop(0, GROUPS_PER_BLOCK, group, 0, unroll=2)


def _s5_tables(a_re, a_im, log_dt, b_re, b_im, c_re, c_im):
    g, p, h = N_GROUPS, STATE_DIM, GROUP_CH
    dup = lambda x: jnp.concatenate([x, x], axis=-1)
    a2 = jnp.stack([dup(a_re), dup(a_im)]).reshape(2, 2, g, 1, 2 * p)
    bt = jnp.concatenate([b_re.transpose(0, 1, 3, 2), b_im.transpose(0, 1, 3, 2)], axis=-1)
    cc = jnp.concatenate([c_re, c_im], axis=-1)
    gb = GROUPS_PER_BLOCK
    return pl.pallas_call(
        _s5_prep_kernel,
        grid=(N_GROUP_BLOCKS,),
        in_specs=[pl.BlockSpec((2, 2, gb, 1, 2 * p), lambda i: (0, 0, i, 0, 0)),
                  pl.BlockSpec((2, gb, 1, 1), lambda i: (0, i, 0, 0)),
                  pl.BlockSpec((2, gb, h, 2 * p), lambda i: (0, i, 0, 0)),
                  pl.BlockSpec((2, gb, h, 2 * p), lambda i: (0, i, 0, 0))],
        out_specs=[pl.BlockSpec((gb, CHUNK_LANES, CHUNK_LANES + 2 * LANES), lambda i: (i, 0, 0)),
                   pl.BlockSpec((gb, CHUNK_LANES, 2 * LANES), lambda i: (i, 0, 0)),
                   pl.BlockSpec((gb, 2, LANES), lambda i: (i, 0, 0))],
        out_shape=[jax.ShapeDtypeStruct((g, CHUNK_LANES, CHUNK_LANES + 2 * LANES), BF16),
                   jax.ShapeDtypeStruct((g, CHUNK_LANES, 2 * LANES), BF16),
                   jax.ShapeDtypeStruct((g, 2, LANES), F32)],
        compiler_params=_params(1),
        name="s5_prep",
    )(a2, log_dt.reshape(2, g, 1, 1), bt, cc)


def _row_perm(to_time_major):
    n = PERM_SEQS * PERM_STEPS
    r = lax.broadcasted_iota(jnp.int32, (n, n), 0)
    c = lax.broadcasted_iota(jnp.int32, (n, n), 1)
    if to_time_major:
        src = jnp.bitwise_and(r, PERM_SEQS - 1) * PERM_STEPS + lax.shift_right_logical(r, 3)
    else:
        src = jnp.bitwise_and(r, PERM_STEPS - 1) * PERM_SEQS + lax.shift_right_logical(r, 5)
    return jnp.where(c == src, 1.0, 0.0).astype(BF16)


def _s5_in_kernel(x_ref, m_ref, w_ref, o_ref, *, tt):
    scale = 1.0 + m_ref[:, 1:2, :]
    shift = m_ref[:, 0:1, :]
    perm = _row_perm(True)
    n = PERM_SEQS * PERM_STEPS
    for k in range(tt // (2 * PERM_STEPS)):
        pieces = []
        for hf in range(2):
            t0 = (2 * k + hf) * PERM_STEPS
            h = x_ref[:, t0:t0 + PERM_STEPS, :] * scale + shift
            hb = h.reshape(n, D_MODEL).astype(BF16)
            pieces.append(_dot(perm, hb).astype(BF16))
        u = _dot(jnp.concatenate(pieces, axis=0), w_ref[...])
        o_ref[2 * k * PERM_STEPS:(2 * k + 2) * PERM_STEPS] = u.reshape(2 * PERM_STEPS, PERM_SEQS, D_MODEL)


def _two_group_steps(xa, xb, tt):
    ta, tb = xa.shape[1] // tt, xb.shape[1] // tt
    na = (xa.shape[0] // PERM_SEQS) * ta
    nb = (xb.shape[0] // PERM_SEQS) * tb

    def tile_a(i):
        i = jnp.minimum(i, na - 1)
        return i // ta, i % ta

    def tile_b(i):
        i = jnp.maximum(i - na, 0)
        return i // tb, i % tb

    return na, nb, tile_a, tile_b


def _s5_in_pair_kernel(xa_ref, ma_ref, xb_ref, mb_ref, w_ref, oa_ref, ob_ref, *, tt, n_first):
    @pl.when(pl.program_id(0) < n_first)
    def _():
        _s5_in_kernel(xa_ref, ma_ref, w_ref, oa_ref, tt=tt)

    @pl.when(pl.program_id(0) >= n_first)
    def _():
        _s5_in_kernel(xb_ref, mb_ref, w_ref, ob_ref, tt=tt)


def _s5_in(xa, ma, xb, mb, w_in, wl, tt):
    na, nb, tile_a, tile_b = _two_group_steps(xa, xb, tt)
    seq_major = lambda tile: (lambda i: (*tile(i), 0))
    time_major = lambda tile: (lambda i: (*tile(i)[::-1], 0))
    slot_only = lambda tile: (lambda i: (tile(i)[0], 0, 0))
    return pl.pallas_call(
        functools.partial(_s5_in_pair_kernel, tt=tt, n_first=na),
        grid=(na + nb,),
        in_specs=[pl.BlockSpec((PERM_SEQS, tt, D_MODEL), seq_major(tile_a)),
                  pl.BlockSpec((PERM_SEQS, 6, D_MODEL), slot_only(tile_a)),
                  pl.BlockSpec((PERM_SEQS, tt, D_MODEL), seq_major(tile_b)),
                  pl.BlockSpec((PERM_SEQS, 6, D_MODEL), slot_only(tile_b)),
                  pl.BlockSpec((None, D_MODEL, D_MODEL), lambda i: (wl, 0, 0))],
        out_specs=[pl.BlockSpec((tt, PERM_SEQS, D_MODEL), time_major(tile_a)),
                   pl.BlockSpec((tt, PERM_SEQS, D_MODEL), time_major(tile_b))],
        out_shape=[jax.ShapeDtypeStruct((x.shape[1], x.shape[0], D_MODEL), F32) for x in (xa, xb)],
        compiler_params=_params(1),
        name="s5_in",
    )(xa, ma, xb, mb, w_in)


def _block_transpose8(v):
    lane = lax.broadcasted_iota(jnp.int32, (SUBLANES, LANES), 1)
    v = list(v)
    for d in (4, 2, 1):
        low = jnp.bitwise_and(lane, GROUP_CH * d) == 0
        nxt = list(v)
        for i in range(8):
            if i & d == 0:
                a, b = v[i], v[i + d]
                if 2 * GROUP_CH * d == LANES:
                    r = pltpu.roll(jnp.where(low, b, a), LANES // 2, 1)
                    nxt[i] = jnp.where(low, a, r)
                    nxt[i + d] = jnp.where(low, r, b)
                else:
                    nxt[i] = jnp.where(low, a, pltpu.roll(b, GROUP_CH * d, 1))
                    nxt[i + d] = jnp.where(low, pltpu.roll(a, LANES - GROUP_CH * d, 1), b)
        v = nxt
    return v


def _s5_core_kernel(*refs, seq, nq, paired, has_fin):
    refs = list(refs)
    u_ref, w1_ref, w2_ref, lam_ref, d_ref = refs[:5]
    pos = 5
    h0_ref = None
    if paired:
        h0_ref = refs[pos]
        pos += 1
    y_ref = refs[pos]
    pos += 1
    fin_ref = refs[pos] if has_fin else None
    z_sc, yt_sc, bs_sc, sp_sc = refs[-4 - int(paired):][:4]
    sp2_sc = refs[-1] if paired else None
    gpb = GROUPS_PER_BLOCK
    nc = seq // S5_CHUNK
    ns = nq * SUBLANES
    half = STATE_DIM

    def fold(c, carry):
        for qi in range(nq):
            sl = slice(qi * SUBLANES, (qi + 1) * SUBLANES)
            r0 = pl.multiple_of(c * ns + qi * SUBLANES, SUBLANES)
            for hf in range(2):
                v = [u_ref[c * S5_CHUNK + hf * 8 + t, sl, :] for t in range(8)]
                w = _block_transpose8(v)
                for g in range(gpb):
                    z_sc[g, pl.ds(r0, SUBLANES), hf * LANES:(hf + 1) * LANES] = w[g]
        return carry

    lax.fori_loop(0, nc, fold, 0, unroll=FOLD_UNROLL // nq)

    for g in range(gpb):
        m1 = _dot(z_sc[g].astype(BF16), w1_ref[g])
        yt_sc[g] = m1[:, 0:CHUNK_LANES]
        bs_sc[g] = m1[:, CHUNK_LANES:]

    lane = lax.broadcasted_iota(jnp.int32, (SUBLANES, LANES), 1)
    fwd = jnp.bitwise_and(lane, half) == 0
    lam = [(jnp.broadcast_to(lam_ref[g, 0:1, :], (SUBLANES, LANES)),
            jnp.broadcast_to(lam_ref[g, 1:2, :], (SUBLANES, LANES))) for g in range(gpb)]

    def run_pass(init, dst):
        def body(i, carry):
            out = []
            for g in range(gpb):
                l_re, l_im = lam[g]
                for qi in range(nq):
                    s_re, s_im = carry[2 * (g * nq + qi)], carry[2 * (g * nq + qi) + 1]
                    ri = pl.ds(pl.multiple_of(i * ns + qi * SUBLANES, SUBLANES), SUBLANES)
                    rr = pl.ds(pl.multiple_of((nc - 1 - i) * ns + qi * SUBLANES, SUBLANES), SUBLANES)
                    dst[g, ri, 0:half] = s_re[:, 0:half]
                    dst[g, ri, LANES:LANES + half] = s_im[:, 0:half]
                    dst[g, rr, half:LANES] = s_re[:, half:LANES]
                    dst[g, rr, LANES + half:2 * LANES] = s_im[:, half:LANES]
                    x_re = jnp.where(fwd, bs_sc[g, ri, 0:LANES], bs_sc[g, rr, 0:LANES])
                    x_im = jnp.where(fwd, bs_sc[g, ri, LANES:2 * LANES], bs_sc[g, rr, LANES:2 * LANES])
                    out.append(l_re * s_re - l_im * s_im + x_re)
                    out.append(l_re * s_im + l_im * s_re + x_im)
            return tuple(out)

        return lax.fori_loop(0, nc, body, init)

    if paired:
        init = []
        for g in range(gpb):
            init += [h0_ref[g, 0], h0_ref[g, 1]]
        mid = run_pass(tuple(init), sp_sc)
        handed = [jnp.where(fwd, pltpu.roll(s, 1, 0), pltpu.roll(s, SUBLANES - 1, 0)) for s in mid]
        fin = run_pass(tuple(handed), sp2_sc)
    else:
        zero = jnp.zeros((SUBLANES, LANES), F32)
        fin = run_pass((zero,) * (2 * gpb * nq), sp_sc)

    if has_fin:
        for g in range(gpb):
            for qi in range(nq):
                sl = slice(qi * SUBLANES, (qi + 1) * SUBLANES)
                fin_ref[g, 0, sl, :] = fin[2 * (g * nq + qi)]
                fin_ref[g, 1, sl, :] = fin[2 * (g * nq + qi) + 1]

    if paired:
        shape = (nc * ns, 2 * LANES)
        row = lax.broadcasted_iota(jnp.int32, shape, 0)
        col = lax.broadcasted_iota(jnp.int32, shape, 1)
        first_pass = (jnp.bitwise_and(row, 1) == 0) == (jnp.bitwise_and(col, half) == 0)
    for g in range(gpb):
        states = sp_sc[g]
        if paired:
            states = jnp.where(first_pass, states, sp2_sc[g])
        yt_sc[g] = yt_sc[g] + _dot_nt(states.astype(BF16), w2_ref[g])

    d = jnp.broadcast_to(d_ref[...], (SUBLANES, LANES))

    def unfold(c, carry):
        for qi in range(nq):
            sl = slice(qi * SUBLANES, (qi + 1) * SUBLANES)
            r0 = pl.multiple_of(c * ns + qi * SUBLANES, SUBLANES)
            for hf in range(2):
                w = [yt_sc[g, pl.ds(r0, SUBLANES), hf * LANES:(hf + 1) * LANES] for g in range(gpb)]
                v = _block_transpose8(w)
                for t in range(8):
                    step = c * S5_CHUNK + hf * 8 + t
                    y_ref[step, sl, :] = v[t] + d * u_ref[step, sl, :]
        return carry

    lax.fori_loop(0, nc, unfold, 0, unroll=FOLD_UNROLL // nq)


def _s5_core(u, w1, w2, lam, d_skip, h0, want_final):
    seq, slots, _ = u.shape
    paired = h0 is not None
    nq = 1 if paired else slots // SUBLANES
    ns = nq * SUBLANES
    gpb = GROUPS_PER_BLOCK
    rows = (seq // S5_CHUNK) * ns
    u_spec = pl.BlockSpec((seq, ns, LANES), lambda s, g: (0, s, g))
    in_specs = [u_spec,
                pl.BlockSpec((gpb, CHUNK_LANES, CHUNK_LANES + 2 * LANES), lambda s, g: (g, 0, 0)),
                pl.BlockSpec((gpb, CHUNK_LANES, 2 * LANES), lambda s, g: (g, 0, 0)),
                pl.BlockSpec((gpb, 2, LANES), lambda s, g: (g, 0, 0)),
                pl.BlockSpec((1, LANES), lambda s, g: (0, g))]
    args = [u, w1, w2, lam, d_skip.reshape(1, D_MODEL)]
    if paired:
        in_specs.append(pl.BlockSpec((gpb, 2, SUBLANES, LANES), lambda s, g: (g, 0, s, 0)))
        args.append(h0)
    out_specs = [u_spec]
    out_shape = [jax.ShapeDtypeStruct(u.shape, F32)]
    if want_final:
        out_specs.append(pl.BlockSpec((gpb, 2, ns, LANES), lambda s, g: (g, 0, s, 0)))
        out_shape.append(jax.ShapeDtypeStruct((N_GROUPS, 2, slots, LANES), F32))
    res = pl.pallas_call(
        functools.partial(_s5_core_kernel, seq=seq, nq=nq, paired=paired, has_fin=want_final),
        grid=(slots // ns, N_GROUP_BLOCKS),
        in_specs=in_specs,
        out_specs=out_specs,
        out_shape=out_shape,
        scratch_shapes=[pltpu.VMEM((gpb, rows, CHUNK_LANES), F32) for _ in range(5 if paired else 4)],
        compiler_params=_params(2),
        name="s5_core",
    )(*args)
    return (res[0], res[1]) if want_final else (res[0], None)


def _s5_out_kernel(y_ref, x_ref, m_ref, wab_ref, wo_ref, lng_ref, lnb_ref, o_ref, *, tt, f):
    gate = m_ref[:, 2:3, :]
    perm = _row_perm(False)
    n = PERM_SEQS * PERM_STEPS
    lng = lng_ref[...].reshape(1, 1, D_MODEL)
    lnb = lnb_ref[...].reshape(1, 1, D_MODEL)
    for k in range(tt // (2 * PERM_STEPS)):
        t0 = 2 * k * PERM_STEPS
        ht = _gelu_tanh(y_ref[t0:t0 + 2 * PERM_STEPS].reshape(2 * n, D_MODEL)).astype(BF16)
        hb = jnp.concatenate([_dot(perm, ht[0:n]).astype(BF16), _dot(perm, ht[n:2 * n]).astype(BF16)],
                             axis=0)
        acc = None
        for c0 in range(0, f, MXU_TILE):
            val = _dot(hb, wab_ref[:, c0:c0 + MXU_TILE])
            gte = _dot(hb, wab_ref[:, f + c0:f + c0 + MXU_TILE])
            z = (val * _sigmoid(gte)).astype(BF16)
            part = _dot(z, wo_ref[c0:c0 + MXU_TILE, :])
            acc = part if acc is None else acc + part
        for hf in range(2):
            ts = slice(t0 + hf * PERM_STEPS, t0 + (hf + 1) * PERM_STEPS)
            mix = acc[hf * n:(hf + 1) * n].reshape(PERM_SEQS, PERM_STEPS, D_MODEL)
            r = DEEPNORM_ALPHA * x_ref[:, ts, :] + gate * mix
            o_ref[:, ts, :] = _layer_norm(r, lng, lnb)


def _s5_out_pair_kernel(ya_ref, xa_ref, ma_ref, yb_ref, xb_ref, mb_ref, wab_ref, wo_ref, lng_ref, lnb_ref,
                        oa_ref, ob_ref, *, tt, f, n_first):
    @pl.when(pl.program_id(0) < n_first)
    def _():
        _s5_out_kernel(ya_ref, xa_ref, ma_ref, wab_ref, wo_ref, lng_ref, lnb_ref, oa_ref, tt=tt, f=f)

    @pl.when(pl.program_id(0) >= n_first)
    def _():
        _s5_out_kernel(yb_ref, xb_ref, mb_ref, wab_ref, wo_ref, lng_ref, lnb_ref, ob_ref, tt=tt, f=f)


def _s5_out(ya, xa, ma, yb, xb, mb, w_glu, w_out, wl, ln_g, ln_b, tt):
    f = w_out.shape[1]
    na, nb, tile_a, tile_b = _two_group_steps(xa, xb, tt)
    seq_major = lambda tile: pl.BlockSpec((PERM_SEQS, tt, D_MODEL), lambda i: (*tile(i), 0))
    time_major = lambda tile: pl.BlockSpec((tt, PERM_SEQS, D_MODEL), lambda i: (*tile(i)[::-1], 0))
    slot_only = lambda tile: pl.BlockSpec((PERM_SEQS, 6, D_MODEL), lambda i: (tile(i)[0], 0, 0))
    resident = pl.Buffered(1)
    return pl.pallas_call(
        functools.partial(_s5_out_pair_kernel, tt=tt, f=f, n_first=na),
        grid=(na + nb,),
        in_specs=[time_major(tile_a), seq_major(tile_a), slot_only(tile_a),
                  time_major(tile_b), seq_major(tile_b), slot_only(tile_b),
                  pl.BlockSpec((None, D_MODEL, 2 * f), lambda i: (wl, 0, 0), pipeline_mode=resident),
                  pl.BlockSpec((None, f, D_MODEL), lambda i: (wl, 0, 0), pipeline_mode=resident),
                  pl.BlockSpec((1, D_MODEL), lambda i: (0, 0)),
                  pl.BlockSpec((1, D_MODEL), lambda i: (0, 0))],
        out_specs=[seq_major(tile_a), seq_major(tile_b)],
        out_shape=[jax.ShapeDtypeStruct(xa.shape, F32), jax.ShapeDtypeStruct(xb.shape, F32)],
        compiler_params=_params(1),
        name="s5_out",
    )(ya, xa, ma, yb, xb, mb, w_glu, w_out, ln_g.reshape(1, D_MODEL), ln_b.reshape(1, D_MODEL))


def _ffn_kernel(xp_ref, xs_ref, m_ref, wi_hbm, wo_hbm, lng_ref, lnb_ref, op_ref, os_ref,
                h_sc, wab_sc, wo_sc, stage_a, stage_b, stage_o, sem, *, f, wl, n_first):
    n_slab = f // MXU_TILE
    step = pl.program_id(0)
    m = m_ref[0]

    def slab_copies(k, slot):
        c0 = k * MXU_TILE
        return (pltpu.make_async_copy(wi_hbm.at[wl, :, pl.ds(c0, MXU_TILE)], stage_a.at[slot], sem.at[0, slot]),
                pltpu.make_async_copy(wi_hbm.at[wl, :, pl.ds(f + c0, MXU_TILE)], stage_b.at[slot], sem.at[1, slot]),
                pltpu.make_async_copy(wo_hbm.at[wl, pl.ds(c0, MXU_TILE), :], stage_o.at[slot], sem.at[2, slot]))

    def run(load_weights, x_ref, o_ref):
        h = x_ref[...] * (1.0 + m[4:5]) + m[3:4]
        h_sc[...] = h.astype(BF16)
        ahead = FFN_STAGE_SLOTS - 1
        if load_weights:
            for k in range(min(ahead, n_slab)):
                for cp in slab_copies(k, k % FFN_STAGE_SLOTS):
                    cp.start()
        acc = None
        for k in range(n_slab):
            c0 = k * MXU_TILE
            if load_weights:
                slot = k % FFN_STAGE_SLOTS
                if k + ahead < n_slab:
                    for cp in slab_copies(k + ahead, (k + ahead) % FFN_STAGE_SLOTS):
                        cp.start()
                for cp in slab_copies(k, slot):
                    cp.wait()
                wab_sc[:, c0:c0 + MXU_TILE] = stage_a[slot].astype(BF16)
                wab_sc[:, f + c0:f + c0 + MXU_TILE] = stage_b[slot].astype(BF16)
                wo_sc[c0:c0 + MXU_TILE, :] = stage_o[slot].astype(BF16)
            a = _dot(h_sc[...], wab_sc[:, c0:c0 + MXU_TILE])
            b = _dot(h_sc[...], wab_sc[:, f + c0:f + c0 + MXU_TILE])
            z = (a * _sigmoid(a)) * b
            part = _dot(z.astype(BF16), wo_sc[c0:c0 + MXU_TILE, :])
            acc = part if acc is None else acc + part
        r = DEEPNORM_ALPHA * x_ref[...] + m[5:6] * acc
        o_ref[...] = _layer_norm(r, lng_ref[...], lnb_ref[...])

    @pl.when(step == 0)
    def _():
        run(True, xp_ref, op_ref)

    @pl.when(jnp.logical_and(step > 0, step < n_first))
    def _():
        run(False, xp_ref, op_ref)

    @pl.when(step >= n_first)
    def _():
        run(False, xs_ref, os_ref)


def _ffn(xp, xs, mods, seq_s, w_in, w_out, wl, ln_g, ln_b, tm):
    f = w_out.shape[1]
    n_first = xp.shape[0] // tm
    n_second = xs.shape[0] // tm
    tiles_per_seq = seq_s // tm
    first_tile = lambda i: (jnp.minimum(i, n_first - 1), 0)
    second_tile = lambda i: (jnp.maximum(i - n_first, 0), 0)
    cond_row = lambda i: (jnp.where(i < n_first, 0, 1 + jnp.maximum(i - n_first, 0) // tiles_per_seq), 0, 0)
    return pl.pallas_call(
        functools.partial(_ffn_kernel, f=f, wl=wl, n_first=n_first),
        grid=(n_first + n_second,),
        in_specs=[pl.BlockSpec((tm, D_MODEL), first_tile),
                  pl.BlockSpec((tm, D_MODEL), second_tile),
                  pl.BlockSpec((1, 6, D_MODEL), cond_row),
                  pl.BlockSpec(memory_space=pl.ANY),
                  pl.BlockSpec(memory_space=pl.ANY),
                  pl.BlockSpec((1, D_MODEL), lambda i: (0, 0)),
                  pl.BlockSpec((1, D_MODEL), lambda i: (0, 0))],
        out_specs=[pl.BlockSpec((tm, D_MODEL), first_tile), pl.BlockSpec((tm, D_MODEL), second_tile)],
        out_shape=[jax.ShapeDtypeStruct(xp.shape, F32), jax.ShapeDtypeStruct(xs.shape, F32)],
        scratch_shapes=[pltpu.VMEM((tm, D_MODEL), BF16),
                        pltpu.VMEM((D_MODEL, 2 * f), BF16),
                        pltpu.VMEM((f, D_MODEL), BF16),
                        pltpu.VMEM((FFN_STAGE_SLOTS, D_MODEL, MXU_TILE), F32),
                        pltpu.VMEM((FFN_STAGE_SLOTS, D_MODEL, MXU_TILE), F32),
                        pltpu.VMEM((FFN_STAGE_SLOTS, MXU_TILE, D_MODEL), F32),
                        pltpu.SemaphoreType.DMA((3, FFN_STAGE_SLOTS))],
        compiler_params=_params(1),
        name="ffn",
    )(xp, xs, mods, w_in, w_out, ln_g.reshape(1, D_MODEL), ln_b.reshape(1, D_MODEL))


def _qkv_kernel(*refs, rope):
    if rope:
        (x_ref, m_ref, w_ref, qg_ref, kg_ref, cos_ref, sin_ref,
         wsw_ref, qgsw_ref, kgsw_ref) = refs[:10]
    else:
        x_ref, m_ref, w_ref, qg_ref, kg_ref = refs[:5]
    q_ref, k_ref, v_ref = refs[-3:]
    rows = x_ref.shape[0]
    m = m_ref[0]
    hb = (x_ref[...] * (1.0 + m[1:2]) + m[0:1]).astype(BF16)
    qkv = _dot(hb, w_ref[...])
    if rope:
        qk_sw = _dot(hb, wsw_ref[...])
        cos = cos_ref[...]
        sin = sin_ref[...]
    for hd in range(N_HEADS + N_KV_HEADS):
        xh = qkv[:, hd * HEAD_DIM:(hd + 1) * HEAD_DIM]
        gain = qg_ref[...] if hd < N_HEADS else kg_ref[...]
        inv = lax.rsqrt(jnp.mean(xh * xh, axis=-1, keepdims=True) + RMS_EPS)
        n = xh * inv * gain
        if rope:
            gain_sw = qgsw_ref[...] if hd < N_HEADS else kgsw_ref[...]
            n_sw = qk_sw[:, hd * HEAD_DIM:(hd + 1) * HEAD_DIM] * inv * gain_sw
            n = n * cos + n_sw * sin
        if hd < N_HEADS:
            q_ref[:, hd * HEAD_DIM:(hd + 1) * HEAD_DIM] = (n * Q_SCALE_LOG2).astype(BF16)
        else:
            k_ref[pl.ds(hd - N_HEADS, rows, stride=N_KV_HEADS), :] = n
    for g in range(N_KV_HEADS):
        v_ref[pl.ds(g, rows, stride=N_KV_HEADS), :] = (
            qkv[:, D_Q + D_KV + g * HEAD_DIM:D_Q + D_KV + (g + 1) * HEAD_DIM])


def _rope_tables(seq):
    pos = jnp.arange(seq, dtype=jnp.int32)
    row = (pos // GRID_W).astype(F32)
    col = (pos % GRID_W).astype(F32)
    inv = ROPE_THETA ** (-jnp.arange(AXIS_PAIRS, dtype=F32) / AXIS_PAIRS)
    ar = row[:, None] * inv
    ac = col[:, None] * inv
    cos = jnp.concatenate([jnp.cos(ar), jnp.cos(ar), jnp.cos(ac), jnp.cos(ac)], axis=-1)
    sin = jnp.concatenate([-jnp.sin(ar), jnp.sin(ar), -jnp.sin(ac), jnp.sin(ac)], axis=-1)
    return cos, sin


def _rope_partner(x):
    lead = x.shape[:-1]
    y = x.reshape(*lead, x.shape[-1] // (2 * AXIS_PAIRS), 2, AXIS_PAIRS)
    return y[..., ::-1, :].reshape(x.shape)


def _qkv_pair_kernel(*refs, n_first):
    (xp_ref, mp_ref, xs_ref, ms_ref, w_ref, qg_ref, kg_ref, cos_ref, sin_ref,
     wsw_ref, qgsw_ref, kgsw_ref) = refs[:12]
    qp_ref, kp_ref, vp_ref, qs_ref, ks_ref, vs_ref = refs[-6:]

    @pl.when(pl.program_id(0) < n_first)
    def _():
        _qkv_kernel(xp_ref, mp_ref, w_ref, qg_ref, kg_ref, qp_ref, kp_ref, vp_ref, rope=False)

    @pl.when(pl.program_id(0) >= n_first)
    def _():
        _qkv_kernel(xs_ref, ms_ref, w_ref, qg_ref, kg_ref, cos_ref, sin_ref, wsw_ref, qgsw_ref, kgsw_ref,
                    qs_ref, ks_ref, vs_ref, rope=True)


def _qkv(xp, mp, seq_p, xs, ms, seq_s, w_qkv, w_partner, wl, q_gain, k_gain, kv_prev):
    nbp, nbs = xp.shape[0] // seq_p, xs.shape[0] // seq_s
    tm_p, tm_s = seq_p, seq_s // 2
    nts = seq_s // tm_s
    n_first = nbp
    first = lambda i: jnp.minimum(i, n_first - 1)
    second = lambda i: jnp.maximum(i - n_first, 0)
    cos, sin = _rope_tables(seq_s)
    kv_layers = kv_prev[0].shape[1]
    in_specs = [pl.BlockSpec((tm_p, D_MODEL), lambda i: (first(i), 0)),
                pl.BlockSpec((1, 6, D_MODEL), lambda i: (0, 0, 0)),
                pl.BlockSpec((tm_s, D_MODEL), lambda i: (second(i), 0)),
                pl.BlockSpec((1, 6, D_MODEL), lambda i: (second(i) // nts, 0, 0)),
                pl.BlockSpec((None, D_MODEL, QKV_DIM), lambda i: (wl, 0, 0)),
                pl.BlockSpec((1, HEAD_DIM), lambda i: (0, 0)),
                pl.BlockSpec((1, HEAD_DIM), lambda i: (0, 0)),
                pl.BlockSpec((tm_s, HEAD_DIM), lambda i: (second(i) % nts, 0)),
                pl.BlockSpec((tm_s, HEAD_DIM), lambda i: (second(i) % nts, 0)),
                pl.BlockSpec((None, D_MODEL, D_Q + D_KV), lambda i: (wl, 0, 0)),
                pl.BlockSpec((1, HEAD_DIM), lambda i: (0, 0)),
                pl.BlockSpec((1, HEAD_DIM), lambda i: (0, 0)),
                pl.BlockSpec(memory_space=pl.ANY),
                pl.BlockSpec(memory_space=pl.ANY)]
    args = [xp, mp, xs, ms, w_qkv, q_gain.reshape(1, HEAD_DIM), k_gain.reshape(1, HEAD_DIM), cos, sin,
            w_partner, _rope_partner(q_gain).reshape(1, HEAD_DIM), _rope_partner(k_gain).reshape(1, HEAD_DIM),
            kv_prev[0], kv_prev[1]]
    kvp_spec = pl.BlockSpec((None, None, tm_p * N_KV_HEADS, HEAD_DIM), lambda i: (first(i), wl, 0, 0))
    kvs_spec = pl.BlockSpec((None, None, tm_s * N_KV_HEADS, HEAD_DIM),
                            lambda i: (second(i) // nts, 0, second(i) % nts, 0))
    kvp_shape = jax.ShapeDtypeStruct((nbp, kv_layers, seq_p * N_KV_HEADS, HEAD_DIM), F32)
    kvs_shape = jax.ShapeDtypeStruct((nbs, 1, seq_s * N_KV_HEADS, HEAD_DIM), F32)
    return pl.pallas_call(
        functools.partial(_qkv_pair_kernel, n_first=n_first),
        grid=(n_first + nbs * nts,),
        in_specs=in_specs,
        out_specs=[pl.BlockSpec((tm_p, D_Q), lambda i: (first(i), 0)), kvp_spec, kvp_spec,
                   pl.BlockSpec((tm_s, D_Q), lambda i: (second(i), 0)), kvs_spec, kvs_spec],
        out_shape=[jax.ShapeDtypeStruct((xp.shape[0], D_Q), BF16), kvp_shape, kvp_shape,
                   jax.ShapeDtypeStruct((xs.shape[0], D_Q), BF16), kvs_shape, kvs_shape],
        input_output_aliases={len(args) - 2: 1, len(args) - 1: 2},
        compiler_params=_params(1),
        name="qkv",
    )(*args)


def _attn_kernel(*refs, has_cache, stack):
    if has_cache:
        (q_ref, k_ref, v_ref, ck_ref, cv_ref, x_ref, m_ref, wo_ref, lng_ref, lnb_ref,
         o_ref, oh_sc) = refs
    else:
        q_ref, k_ref, v_ref, x_ref, m_ref, wo_ref, lng_ref, lnb_ref, o_ref, oh_sc = refs
    for g in range(N_KV_HEADS):
        sl = slice(g * HEAD_DIM, (g + 1) * HEAD_DIM)
        head_rows = lambda ref: ref[pl.ds(g, ref.shape[0] // N_KV_HEADS, stride=N_KV_HEADS), :]
        kg = head_rows(k_ref).astype(BF16)
        vg = head_rows(v_ref).astype(BF16)
        if has_cache:
            ckg = head_rows(ck_ref).astype(BF16)
            cvg = head_rows(cv_ref).astype(BF16)
        tq = q_ref.shape[0]
        for r0 in range(0, KV_REP, stack):
            heads = [g * KV_REP + r0 + r for r in range(stack)]
            qh = jnp.concatenate([q_ref[:, h * HEAD_DIM:(h + 1) * HEAD_DIM] for h in heads], axis=0)
            s1 = _dot_nt(qh, kg)
            mx = jnp.max(s1, axis=-1, keepdims=True)
            if has_cache:
                s2 = _dot_nt(qh, ckg)
                mx = jnp.maximum(mx, jnp.max(s2, axis=-1, keepdims=True))
            p1 = jnp.exp2(s1 - mx)
            den = jnp.sum(p1, axis=-1, keepdims=True)
            o = _dot(p1.astype(BF16), vg)
            if has_cache:
                p2 = jnp.exp2(s2 - mx)
                den = den + jnp.sum(p2, axis=-1, keepdims=True)
                o = o + _dot(p2.astype(BF16), cvg)
            o = (o / den).astype(BF16)
            for r, h in enumerate(heads):
                oh_sc[:, h * HEAD_DIM:(h + 1) * HEAD_DIM] = o[r * tq:(r + 1) * tq]
    mix = _dot(oh_sc[...], wo_ref[...])
    m = m_ref[0]
    r = DEEPNORM_ALPHA * x_ref[...] + m[2:3] * mix
    o_ref[...] = _layer_norm(r, lng_ref[...], lnb_ref[...])


def _attention(q, k, v, kv_slot, cache_k, cache_v, layer_j, x, mods, w_o, ln_g, ln_b, nb, seq, tq):
    nt = seq // tq
    per_batch = mods.shape[0] > 1
    has_cache = cache_k is not None
    tok = pl.BlockSpec((tq, D_MODEL), lambda b, t: (b * nt + t, 0))
    kv = pl.BlockSpec((None, None, seq * N_KV_HEADS, HEAD_DIM), lambda b, t: (b, kv_slot, 0, 0))
    in_specs, args = [tok, kv, kv], [q, k, v]
    if has_cache:
        past = cache_k.shape[2]
        cspec = pl.BlockSpec((None, None, past * N_KV_HEADS, HEAD_DIM), lambda b, t: (b, layer_j, 0, 0))
        in_specs += [cspec, cspec]
        flat = (cache_k.shape[0], cache_k.shape[1], past * N_KV_HEADS, HEAD_DIM)
        args += [cache_k.reshape(flat), cache_v.reshape(flat)]
    in_specs += [tok,
                 pl.BlockSpec((1, 6, D_MODEL), lambda b, t: (b if per_batch else 0, 0, 0)),
                 pl.BlockSpec((None, D_Q, D_MODEL), lambda b, t: (layer_j, 0, 0)),
                 pl.BlockSpec((1, D_MODEL), lambda b, t: (0, 0)),
                 pl.BlockSpec((1, D_MODEL), lambda b, t: (0, 0))]
    args += [x, mods, w_o, ln_g.reshape(1, D_MODEL), ln_b.reshape(1, D_MODEL)]
    return pl.pallas_call(
        functools.partial(_attn_kernel, has_cache=has_cache, stack=1 if has_cache else KV_REP),
        grid=(nb, nt),
        in_specs=in_specs,
        out_specs=tok,
        out_shape=jax.ShapeDtypeStruct((nb * seq, D_MODEL), F32),
        scratch_shapes=[pltpu.VMEM((tq, D_Q), BF16)],
        compiler_params=_params(2),
        name="attn_cache" if has_cache else "attn",
    )(*args)


def _latent_h0(st):
    f = st[:, 0].transpose(2, 1, 0, 3)
    b = st[:, 1].transpose(2, 1, 0, 3)
    z = jnp.zeros_like(f)
    even = jnp.concatenate([f, z], axis=-1)
    odd = jnp.concatenate([z, b], axis=-1)
    h0 = jnp.stack([even, odd], axis=3)
    return h0.reshape(N_GROUPS, 2, 2 * st.shape[0], 2 * STATE_DIM)


def kernel(x_prompt, x_sample, c, cache_k, cache_v, state_s5, c_ctx, w_mod, b_mod, ln_g, ln_b, w_s5_in, s5_a_re, s5_a_im, s5_log_dt, s5_b_re, s5_b_im, s5_c_re, s5_c_im, s5_d, w_s5_glu, w_s5_out, w_qkv, q_norm_g, k_norm_g, w_o, w_ffn_in, w_ffn_out):
    nbp, seqp, _ = x_prompt.shape
    nbs, seqs, _ = x_sample.shape
    xp = x_prompt.reshape(nbp * seqp, D_MODEL)
    xs = x_sample.reshape(nbs * seqs, D_MODEL)
    s5_slots_s, s5_seq_s = 2 * nbs, seqs // 2

    cond = jnp.concatenate([c_ctx[None, :], c, jnp.zeros((8 - 1 - nbs, D_MODEL), F32)], axis=0)
    mods = _adaln(cond, w_mod, b_mod)
    mods_all = mods.reshape(DEPTH, 8, 6, D_MODEL)
    mods_p = mods[:, 0:1].reshape(DEPTH, 1, 6, D_MODEL)
    mods_s = mods[:, 1:1 + nbs].reshape(DEPTH, nbs, 6, D_MODEL)

    w_in, w_glu, w_out = w_s5_in.astype(BF16), w_s5_glu.astype(BF16), w_s5_out.astype(BF16)
    wq, wo = w_qkv.astype(BF16), w_o.astype(BF16)
    wq_partner = _rope_partner(wq[:, :, :D_Q + D_KV])

    kv_rows = (nbp, DEPTH // 2, seqp * N_KV_HEADS, HEAD_DIM)
    new_kv, new_s = (jnp.zeros(kv_rows, F32), jnp.zeros(kv_rows, F32)), []
    for layer in range(DEPTH):
        j = layer // 2
        mp, ms = mods_p[layer], mods_s[layer]
        lg0, lb0, lg1, lb1 = ln_g[layer, 0], ln_b[layer, 0], ln_g[layer, 1], ln_b[layer, 1]
        if layer % 2 == 0:
            w1, w2, lam = _s5_tables(s5_a_re[j], s5_a_im[j], s5_log_dt[j], s5_b_re[j], s5_b_im[j],
                                     s5_c_re[j], s5_c_im[j])
            mp_slots = jnp.broadcast_to(mp, (nbp, 6, D_MODEL))
            ms_slots = jnp.repeat(ms, 2, axis=0)
            xp3 = xp.reshape(nbp, seqp, D_MODEL)
            xs3 = xs.reshape(s5_slots_s, s5_seq_s, D_MODEL)
            up, us = _s5_in(xp3, mp_slots, xs3, ms_slots, w_in, j, 128)
            yp, fin = _s5_core(up, w1, w2, lam, s5_d[j], None, True)
            ys, _ = _s5_core(us, w1, w2, lam, s5_d[j], _latent_h0(state_s5[:, j]), False)
            fin = fin.reshape(N_GROUPS, 2, nbp, 2, STATE_DIM)
            new_s.append(fin.transpose(2, 3, 1, 0, 4))
            xp, xs = _s5_out(yp, xp3, mp_slots, ys, xs3, ms_slots, w_glu, w_out, j, lg0, lb0, 64)
            xp = xp.reshape(nbp * seqp, D_MODEL)
            xs = xs.reshape(nbs * seqs, D_MODEL)
        else:
            qp, kp, vp, qs, ks, vs = _qkv(xp, mp, seqp, xs, ms, seqs, wq, wq_partner, j,
                                          q_norm_g[j], k_norm_g[j], new_kv)
            new_kv = (kp, vp)
            xp = _attention(qp, kp, vp, j, None, None, j, xp, mp, wo, lg0, lb0, nbp, seqp, 256)
            xs = _attention(qs, ks, vs, 0, cache_k, cache_v, j, xs, ms, wo, lg0, lb0, nbs, seqs, 512)
        xp, xs = _ffn(xp, xs, mods_all[layer], seqs, w_ffn_in, w_ffn_out, layer, lg1, lb1, 512)

    y_prompt = xp.reshape(nbp, seqp, D_MODEL)
    y_sample = xs.reshape(nbs, seqs, D_MODEL)
    kv_out = (nbp, DEPTH // 2, seqp, N_KV_HEADS, HEAD_DIM)
    return (y_prompt, y_sample, new_kv[0].reshape(kv_out), new_kv[1].reshape(kv_out),
            jnp.stack(new_s, axis=1))
```

```python
import functools
import math

import jax
import jax.numpy as jnp
from jax import lax
from jax.experimental import pallas as pl
from jax.experimental.pallas import tpu as pltpu

F32 = jnp.float32
BF16 = jnp.bfloat16

D_MODEL = 1024
DEPTH = 4
N_GROUPS = 64
GROUP_CH = 16
STATE_DIM = 64
HEAD_DIM = 128
N_HEADS = 8
N_KV_HEADS = 2
KV_REP = N_HEADS // N_KV_HEADS
D_Q = N_HEADS * HEAD_DIM
D_KV = N_KV_HEADS * HEAD_DIM
QKV_DIM = D_Q + 2 * D_KV
GRID_W = 64
ROPE_THETA = 10000.0
AXIS_PAIRS = HEAD_DIM // 4
ATTN_SCALE = HEAD_DIM ** -0.5
Q_SCALE_LOG2 = ATTN_SCALE * math.log2(math.e)
DEEPNORM_ALPHA = (2.0 * DEPTH) ** 0.25
LN_EPS = 1e-6
RMS_EPS = 1e-6

V7X_VMEM_LIMIT_BYTES = 56 * 1024 * 1024
LANES = 128
SUBLANES = 8
MXU_TILE = 256
FFN_STAGE_SLOTS = 2
GROUPS_PER_BLOCK = LANES // GROUP_CH
N_GROUP_BLOCKS = N_GROUPS // GROUPS_PER_BLOCK
S5_CHUNK = MXU_TILE // GROUP_CH
CHUNK_LANES = S5_CHUNK * GROUP_CH
FOLD_UNROLL = 4
PERM_SEQS = SUBLANES
PERM_STEPS = MXU_TILE // PERM_SEQS
NT_DIMS = (((1,), (1,)), ((), ()))


def _params(n_axes):
    return pltpu.CompilerParams(dimension_semantics=("arbitrary",) * n_axes,
                                vmem_limit_bytes=V7X_VMEM_LIMIT_BYTES)


def _sigmoid(x):
    return 1.0 / (1.0 + jnp.exp(-x))


def _gelu_tanh(x):
    cdf = 0.5 * (1.0 + jnp.tanh(math.sqrt(2.0 / math.pi) * (x + 0.044715 * (x * x * x))))
    return x * cdf


def _layer_norm(r, g, b):
    mu = jnp.mean(r, axis=-1, keepdims=True)
    d = r - mu
    var = jnp.mean(d * d, axis=-1, keepdims=True)
    return d * lax.rsqrt(var + LN_EPS) * g + b


def _dot(a, b):
    return jnp.dot(a, b, preferred_element_type=F32)


def _dot_nt(a, b):
    return lax.dot_general(a, b, NT_DIMS, preferred_element_type=F32)


def _adaln_kernel(c_ref, w_ref, b_ref, o_ref):
    c = c_ref[...]
    s = c * _sigmoid(c)
    o_ref[0] = _dot(s.astype(BF16), w_ref[0].astype(BF16)) + b_ref[0]


def _adaln(cond, w_mod, b_mod):
    tn = 1536
    n = 6 * D_MODEL
    return pl.pallas_call(
        _adaln_kernel,
        grid=(DEPTH, n // tn),
        in_specs=[pl.BlockSpec((8, D_MODEL), lambda l, j: (0, 0)),
                  pl.BlockSpec((1, D_MODEL, tn), lambda l, j: (l, 0, j)),
                  pl.BlockSpec((1, 1, tn), lambda l, j: (l, 0, j))],
        out_specs=pl.BlockSpec((1, 8, tn), lambda l, j: (l, 0, j)),
        out_shape=jax.ShapeDtypeStruct((DEPTH, 8, n), F32),
        compiler_params=_params(2),
        name="adaln",
    )(cond, w_mod, b_mod.reshape(DEPTH, 1, n))


def _s5_prep_kernel(a_ref, ldt_ref, bt_ref, c_ref, w1_ref, w2_ref, lam_ref):
    half = STATE_DIM
    lane1 = lax.broadcasted_iota(jnp.int32, (1, LANES), 1)
    sgn1 = jnp.where(lane1 < half, -1.0, 1.0)
    lane_h = lax.broadcasted_iota(jnp.int32, (GROUP_CH, LANES), 1)
    first_h = lane_h < half
    conj_h = jnp.where(first_h, 1.0, -1.0)
    lane_c = lax.broadcasted_iota(jnp.int32, (GROUP_CH, CHUNK_LANES), 1)
    t = S5_CHUNK

    def cmul(pr, pi, x):
        return pr * x + (pi * sgn1) * pltpu.roll(x, half, 1)

    def pack_states(f, b):
        return (jnp.where(first_h, f, pltpu.roll(b, half, 1)),
                jnp.where(first_h, pltpu.roll(f, half, 1), b))

    def group(g, carry):
        kt, qs, cks, lam_t = [], [], [], []
        for d in range(2):
            a_re = a_ref[0, d, g]
            a_im = a_ref[1, d, g]
            dt = jnp.exp(ldt_ref[d, g])
            mag = jnp.exp(dt * a_re)
            lr = mag * jnp.cos(dt * a_im)
            li = mag * jnp.sin(dt * a_im)
            den = a_re * a_re + a_im * a_im
            nr = lr - 1.0
            k_re = (nr * a_re + li * a_im) / den
            k_im = (li * a_re - nr * a_im) / den
            bb = cmul(k_re, k_im, bt_ref[d, g])
            cc = c_ref[d, g]
            pr = jnp.ones((1, LANES), F32)
            pi = jnp.zeros((1, LANES), F32)
            ck, q = [], []
            for k in range(t + 1):
                ck.append(cmul(pr, pi, cc))
                if k < t:
                    q.append(cmul(pr, pi, bb))
                    pr, pi = pr * lr - pi * li, pr * li + pi * lr
            lam_t.append((pr, pi))
            order = range(t) if d == 0 else range(t - 1, -1, -1)
            rhs = jnp.concatenate([ck[k] for k in order], axis=0).astype(BF16)
            kt.append(_dot_nt((bb * conj_h).astype(BF16), rhs))
            qs.append(q)
            cks.append(ck)
        for j in range(t):
            rows = slice(j * GROUP_CH, (j + 1) * GROUP_CH)
            tf = kt[0] if j == 0 else pltpu.roll(kt[0], GROUP_CH * j, 1)
            tf = jnp.where(lane_c >= GROUP_CH * j, tf, 0.0)
            back = GROUP_CH * (t - 1 - j)
            tb = kt[1] if back == 0 else pltpu.roll(kt[1], CHUNK_LANES - back, 1)
            tb = jnp.where(lane_c < GROUP_CH * (j + 1), tb, 0.0)
            w1_ref[g, rows, 0:CHUNK_LANES] = (tf + tb).astype(BF16)
            s_re, s_im = pack_states(qs[0][t - 1 - j], qs[1][j])
            w1_ref[g, rows, CHUNK_LANES:CHUNK_LANES + LANES] = s_re.astype(BF16)
            w1_ref[g, rows, CHUNK_LANES + LANES:CHUNK_LANES + 2 * LANES] = s_im.astype(BF16)
            c_re, c_im = pack_states(cks[0][j + 1] * conj_h, cks[1][t - j] * conj_h)
            w2_ref[g, rows, 0:LANES] = c_re.astype(BF16)
            w2_ref[g, rows, LANES:2 * LANES] = c_im.astype(BF16)
        fwd1 = lane1 < half
        lam_ref[g, 0:1, :] = jnp.where(fwd1, lam_t[0][0], lam_t[1][0])
        lam_ref[g, 1:2, :] = jnp.where(fwd1, lam_t[0][1], lam_t[1][1])
        return carry

    lax.fori_loop(0, GROUPS_PER_BLOCK, group, 0, unroll=2)


def _s5_tables(a_re, a_im, log_dt, b_re, b_im, c_re, c_im):
    g, p, h = N_GROUPS, STATE_DIM, GROUP_CH
    dup = lambda x: jnp.concatenate([x, x], axis=-1)
    a2 = jnp.stack([dup(a_re), dup(a_im)]).reshape(2, 2, g, 1, 2 * p)
    bt = jnp.concatenate([b_re.transpose(0, 1, 3, 2), b_im.transpose(0, 1, 3, 2)], axis=-1)
    cc = jnp.concatenate([c_re, c_im], axis=-1)
    gb = GROUPS_PER_BLOCK
    return pl.pallas_call(
        _s5_prep_kernel,
        grid=(N_GROUP_BLOCKS,),
        in_specs=[pl.BlockSpec((2, 2, gb, 1, 2 * p), lambda i: (0, 0, i, 0, 0)),
                  pl.BlockSpec((2, gb, 1, 1), lambda i: (0, i, 0, 0)),
                  pl.BlockSpec((2, gb, h, 2 * p), lambda i: (0, i, 0, 0)),
                  pl.BlockSpec((2, gb, h, 2 * p), lambda i: (0, i, 0, 0))],
        out_specs=[pl.BlockSpec((gb, CHUNK_LANES, CHUNK_LANES + 2 * LANES), lambda i: (i, 0, 0)),
                   pl.BlockSpec((gb, CHUNK_LANES, 2 * LANES), lambda i: (i, 0, 0)),
                   pl.BlockSpec((gb, 2, LANES), lambda i: (i, 0, 0))],
        out_shape=[jax.ShapeDtypeStruct((g, CHUNK_LANES, CHUNK_LANES + 2 * LANES), BF16),
                   jax.ShapeDtypeStruct((g, CHUNK_LANES, 2 * LANES), BF16),
                   jax.ShapeDtypeStruct((g, 2, LANES), F32)],
        compiler_params=_params(1),
        name="s5_prep",
    )(a2, log_dt.reshape(2, g, 1, 1), bt, cc)


def _row_perm(to_time_major):
    n = PERM_SEQS * PERM_STEPS
    r = lax.broadcasted_iota(jnp.int32, (n, n), 0)
    c = lax.broadcasted_iota(jnp.int32, (n, n), 1)
    if to_time_major:
        src = jnp.bitwise_and(r, PERM_SEQS - 1) * PERM_STEPS + lax.shift_right_logical(r, 3)
    else:
        src = jnp.bitwise_and(r, PERM_STEPS - 1) * PERM_SEQS + lax.shift_right_logical(r, 5)
    return jnp.where(c == src, 1.0, 0.0).astype(BF16)


def _s5_in_kernel(x_ref, m_ref, w_ref, o_ref, *, tt):
    scale = 1.0 + m_ref[:, 1:2, :]
    shift = m_ref[:, 0:1, :]
    perm = _row_perm(True)
    n = PERM_SEQS * PERM_STEPS
    for k in range(tt // (2 * PERM_STEPS)):
        pieces = []
        for hf in range(2):
            t0 = (2 * k + hf) * PERM_STEPS
            h = x_ref[:, t0:t0 + PERM_STEPS, :] * scale + shift
            hb = h.reshape(n, D_MODEL).astype(BF16)
            pieces.append(_dot(perm, hb).astype(BF16))
        u = _dot(jnp.concatenate(pieces, axis=0), w_ref[...])
        o_ref[2 * k * PERM_STEPS:(2 * k + 2) * PERM_STEPS] = u.reshape(2 * PERM_STEPS, PERM_SEQS, D_MODEL)


def _two_group_steps(xa, xb, tt):
    ta, tb = xa.shape[1] // tt, xb.shape[1] // tt
    na = (xa.shape[0] // PERM_SEQS) * ta
    nb = (xb.shape[0] // PERM_SEQS) * tb

    def tile_a(i):
        i = jnp.minimum(i, na - 1)
        return i // ta, i % ta

    def tile_b(i):
        i = jnp.maximum(i - na, 0)
        return i // tb, i % tb

    return na, nb, tile_a, tile_b


def _s5_in_pair_kernel(xa_ref, ma_ref, xb_ref, mb_ref, w32_ref, oa_ref, ob_ref, w_ref, *, tt, n_first):
    @pl.when(pl.program_id(0) == 0)
    def _():
        w_ref[...] = w32_ref[...].astype(BF16)

    @pl.when(pl.program_id(0) < n_first)
    def _():
        _s5_in_kernel(xa_ref, ma_ref, w_ref, oa_ref, tt=tt)

    @pl.when(pl.program_id(0) >= n_first)
    def _():
        _s5_in_kernel(xb_ref, mb_ref, w_ref, ob_ref, tt=tt)


def _s5_in(xa, ma, xb, mb, w_in, wl, tt):
    na, nb, tile_a, tile_b = _two_group_steps(xa, xb, tt)
    seq_major = lambda tile: (lambda i: (*tile(i), 0))
    time_major = lambda tile: (lambda i: (*tile(i)[::-1], 0))
    slot_only = lambda tile: (lambda i: (tile(i)[0], 0, 0))
    return pl.pallas_call(
        functools.partial(_s5_in_pair_kernel, tt=tt, n_first=na),
        grid=(na + nb,),
        in_specs=[pl.BlockSpec((PERM_SEQS, tt, D_MODEL), seq_major(tile_a)),
                  pl.BlockSpec((PERM_SEQS, 6, D_MODEL), slot_only(tile_a)),
                  pl.BlockSpec((PERM_SEQS, tt, D_MODEL), seq_major(tile_b)),
                  pl.BlockSpec((PERM_SEQS, 6, D_MODEL), slot_only(tile_b)),
                  pl.BlockSpec((None, D_MODEL, D_MODEL), lambda i: (wl, 0, 0), pipeline_mode=pl.Buffered(1))],
        out_specs=[pl.BlockSpec((tt, PERM_SEQS, D_MODEL), time_major(tile_a)),
                   pl.BlockSpec((tt, PERM_SEQS, D_MODEL), time_major(tile_b))],
        out_shape=[jax.ShapeDtypeStruct((x.shape[1], x.shape[0], D_MODEL), F32) for x in (xa, xb)],
        scratch_shapes=[pltpu.VMEM((D_MODEL, D_MODEL), BF16)],
        compiler_params=_params(1),
        name="s5_in",
    )(xa, ma, xb, mb, w_in)


def _block_transpose8(v):
    lane = lax.broadcasted_iota(jnp.int32, (SUBLANES, LANES), 1)
    v = list(v)
    for d in (4, 2, 1):
        low = jnp.bitwise_and(lane, GROUP_CH * d) == 0
        nxt = list(v)
        for i in range(8):
            if i & d == 0:
                a, b = v[i], v[i + d]
                if 2 * GROUP_CH * d == LANES:
                    r = pltpu.roll(jnp.where(low, b, a), LANES // 2, 1)
                    nxt[i] = jnp.where(low, a, r)
                    nxt[i + d] = jnp.where(low, r, b)
                else:
                    nxt[i] = jnp.where(low, a, pltpu.roll(b, GROUP_CH * d, 1))
                    nxt[i + d] = jnp.where(low, pltpu.roll(a, LANES - GROUP_CH * d, 1), b)
        v = nxt
    return v


def _s5_core_kernel(*refs, seq, nq, paired, has_fin):
    refs = list(refs)
    u_ref, w1_ref, w2_ref, lam_ref, d_ref = refs[:5]
    pos = 5
    h0_ref = None
    if paired:
        h0_ref = refs[pos]
        pos += 1
    y_ref = refs[pos]
    pos += 1
    fin_ref = refs[pos] if has_fin else None
    z_sc, yt_sc, bs_sc, sp_sc = refs[-4 - int(paired):][:4]
    sp2_sc = refs[-1] if paired else None
    gpb = GROUPS_PER_BLOCK
    nc = seq // S5_CHUNK
    ns = nq * SUBLANES
    half = STATE_DIM

    def fold(c, carry):
        for qi in range(nq):
            sl = slice(qi * SUBLANES, (qi + 1) * SUBLANES)
            r0 = pl.multiple_of(c * ns + qi * SUBLANES, SUBLANES)
            for hf in range(2):
                v = [u_ref[c * S5_CHUNK + hf * 8 + t, sl, :] for t in range(8)]
                w = _block_transpose8(v)
                for g in range(gpb):
                    z_sc[g, pl.ds(r0, SUBLANES), hf * LANES:(hf + 1) * LANES] = w[g]
        return carry

    lax.fori_loop(0, nc, fold, 0, unroll=FOLD_UNROLL // nq)

    for g in range(gpb):
        m1 = _dot(z_sc[g].astype(BF16), w1_ref[g])
        yt_sc[g] = m1[:, 0:CHUNK_LANES]
        bs_sc[g] = m1[:, CHUNK_LANES:]

    lane = lax.broadcasted_iota(jnp.int32, (SUBLANES, LANES), 1)
    fwd = jnp.bitwise_and(lane, half) == 0
    lam = [(jnp.broadcast_to(lam_ref[g, 0:1, :], (SUBLANES, LANES)),
            jnp.broadcast_to(lam_ref[g, 1:2, :], (SUBLANES, LANES))) for g in range(gpb)]

    def run_pass(init, dst):
        def body(i, carry):
            out = []
            for g in range(gpb):
                l_re, l_im = lam[g]
                for qi in range(nq):
                    s_re, s_im = carry[2 * (g * nq + qi)], carry[2 * (g * nq + qi) + 1]
                    ri = pl.ds(pl.multiple_of(i * ns + qi * SUBLANES, SUBLANES), SUBLANES)
                    rr = pl.ds(pl.multiple_of((nc - 1 - i) * ns + qi * SUBLANES, SUBLANES), SUBLANES)
                    dst[g, ri, 0:half] = s_re[:, 0:half]
                    dst[g, ri, LANES:LANES + half] = s_im[:, 0:half]
                    dst[g, rr, half:LANES] = s_re[:, half:LANES]
                    dst[g, rr, LANES + half:2 * LANES] = s_im[:, half:LANES]
                    x_re = jnp.where(fwd, bs_sc[g, ri, 0:LANES], bs_sc[g, rr, 0:LANES])
                    x_im = jnp.where(fwd, bs_sc[g, ri, LANES:2 * LANES], bs_sc[g, rr, LANES:2 * LANES])
                    out.append(l_re * s_re - l_im * s_im + x_re)
                    out.append(l_re * s_im + l_im * s_re + x_im)
            return tuple(out)

        return lax.fori_loop(0, nc, body, init)

    if paired:
        init = []
        for g in range(gpb):
            init += [h0_ref[g, 0], h0_ref[g, 1]]
        mid = run_pass(tuple(init), sp_sc)
        handed = [jnp.where(fwd, pltpu.roll(s, 1, 0), pltpu.roll(s, SUBLANES - 1, 0)) for s in mid]
        fin = run_pass(tuple(handed), sp2_sc)
    else:
        zero = jnp.zeros((SUBLANES, LANES), F32)
        fin = run_pass((zero,) * (2 * gpb * nq), sp_sc)

    if has_fin:
        for g in range(gpb):
            for qi in range(nq):
                sl = slice(qi * SUBLANES, (qi + 1) * SUBLANES)
                fin_ref[g, 0, sl, :] = fin[2 * (g * nq + qi)]
                fin_ref[g, 1, sl, :] = fin[2 * (g * nq + qi) + 1]

    if paired:
        shape = (nc * ns, 2 * LANES)
        row = lax.broadcasted_iota(jnp.int32, shape, 0)
        col = lax.broadcasted_iota(jnp.int32, shape, 1)
        first_pass = (jnp.bitwise_and(row, 1) == 0) == (jnp.bitwise_and(col, half) == 0)
    for g in range(gpb):
        states = sp_sc[g]
        if paired:
            states = jnp.where(first_pass, states, sp2_sc[g])
        yt_sc[g] = yt_sc[g] + _dot_nt(states.astype(BF16), w2_ref[g])

    d = jnp.broadcast_to(d_ref[...], (SUBLANES, LANES))

    def unfold(c, carry):
        for qi in range(nq):
            sl = slice(qi * SUBLANES, (qi + 1) * SUBLANES)
            r0 = pl.multiple_of(c * ns + qi * SUBLANES, SUBLANES)
            for hf in range(2):
                w = [yt_sc[g, pl.ds(r0, SUBLANES), hf * LANES:(hf + 1) * LANES] for g in range(gpb)]
                v = _block_transpose8(w)
                for t in range(8):
                    step = c * S5_CHUNK + hf * 8 + t
                    y_ref[step, sl, :] = v[t] + d * u_ref[step, sl, :]
        return carry

    lax.fori_loop(0, nc, unfold, 0, unroll=FOLD_UNROLL // nq)


def _s5_core(u, w1, w2, lam, d_skip, h0, want_final):
    seq, slots, _ = u.shape
    paired = h0 is not None
    nq = 1 if paired else slots // SUBLANES
    ns = nq * SUBLANES
    gpb = GROUPS_PER_BLOCK
    rows = (seq // S5_CHUNK) * ns
    u_spec = pl.BlockSpec((seq, ns, LANES), lambda s, g: (0, s, g))
    in_specs = [u_spec,
                pl.BlockSpec((gpb, CHUNK_LANES, CHUNK_LANES + 2 * LANES), lambda s, g: (g, 0, 0)),
                pl.BlockSpec((gpb, CHUNK_LANES, 2 * LANES), lambda s, g: (g, 0, 0)),
                pl.BlockSpec((gpb, 2, LANES), lambda s, g: (g, 0, 0)),
                pl.BlockSpec((1, LANES), lambda s, g: (0, g))]
    args = [u, w1, w2, lam, d_skip.reshape(1, D_MODEL)]
    if paired:
        in_specs.append(pl.BlockSpec((gpb, 2, SUBLANES, LANES), lambda s, g: (g, 0, s, 0)))
        args.append(h0)
    out_specs = [u_spec]
    out_shape = [jax.ShapeDtypeStruct(u.shape, F32)]
    if want_final:
        out_specs.append(pl.BlockSpec((gpb, 2, ns, LANES), lambda s, g: (g, 0, s, 0)))
        out_shape.append(jax.ShapeDtypeStruct((N_GROUPS, 2, slots, LANES), F32))
    res = pl.pallas_call(
        functools.partial(_s5_core_kernel, seq=seq, nq=nq, paired=paired, has_fin=want_final),
        grid=(slots // ns, N_GROUP_BLOCKS),
        in_specs=in_specs,
        out_specs=out_specs,
        out_shape=out_shape,
        scratch_shapes=[pltpu.VMEM((gpb, rows, CHUNK_LANES), F32) for _ in range(5 if paired else 4)],
        compiler_params=_params(2),
        name="s5_core",
    )(*args)
    return (res[0], res[1]) if want_final else (res[0], None)


def _s5_out_kernel(y_ref, x_ref, m_ref, wab_ref, wo_ref, lng_ref, lnb_ref, o_ref, *, tt, f):
    gate = m_ref[:, 2:3, :]
    perm = _row_perm(False)
    n = PERM_SEQS * PERM_STEPS
    lng = lng_ref[...].reshape(1, 1, D_MODEL)
    lnb = lnb_ref[...].reshape(1, 1, D_MODEL)
    for k in range(tt // (2 * PERM_STEPS)):
        t0 = 2 * k * PERM_STEPS
        ht = _gelu_tanh(y_ref[t0:t0 + 2 * PERM_STEPS].reshape(2 * n, D_MODEL)).astype(BF16)
        hb = jnp.concatenate([_dot(perm, ht[0:n]).astype(BF16), _dot(perm, ht[n:2 * n]).astype(BF16)],
                             axis=0)
        acc = None
        for c0 in range(0, f, MXU_TILE):
            val = _dot(hb, wab_ref[:, c0:c0 + MXU_TILE])
            gte = _dot(hb, wab_ref[:, f + c0:f + c0 + MXU_TILE])
            z = (val * _sigmoid(gte)).astype(BF16)
            part = _dot(z, wo_ref[c0:c0 + MXU_TILE, :])
            acc = part if acc is None else acc + part
        for hf in range(2):
            ts = slice(t0 + hf * PERM_STEPS, t0 + (hf + 1) * PERM_STEPS)
            mix = acc[hf * n:(hf + 1) * n].reshape(PERM_SEQS, PERM_STEPS, D_MODEL)
            r = DEEPNORM_ALPHA * x_ref[:, ts, :] + gate * mix
            o_ref[:, ts, :] = _layer_norm(r, lng, lnb)


def _s5_out_pair_kernel(ya_ref, xa_ref, ma_ref, yb_ref, xb_ref, mb_ref, wab32_ref, wo32_ref, lng_ref, lnb_ref,
                        oa_ref, ob_ref, wab_ref, wo_ref, *, tt, f, n_first):
    @pl.when(pl.program_id(0) == 0)
    def _():
        for c0 in range(0, 2 * f, MXU_TILE):
            wab_ref[:, c0:c0 + MXU_TILE] = wab32_ref[:, c0:c0 + MXU_TILE].astype(BF16)
        for r0 in range(0, f, MXU_TILE):
            wo_ref[r0:r0 + MXU_TILE, :] = wo32_ref[r0:r0 + MXU_TILE, :].astype(BF16)

    @pl.when(pl.program_id(0) < n_first)
    def _():
        _s5_out_kernel(ya_ref, xa_ref, ma_ref, wab_ref, wo_ref, lng_ref, lnb_ref, oa_ref, tt=tt, f=f)

    @pl.when(pl.program_id(0) >= n_first)
    def _():
        _s5_out_kernel(yb_ref, xb_ref, mb_ref, wab_ref, wo_ref, lng_ref, lnb_ref, ob_ref, tt=tt, f=f)


def _s5_out(ya, xa, ma, yb, xb, mb, w_glu, w_out, wl, ln_g, ln_b, tt):
    f = w_out.shape[1]
    na, nb, tile_a, tile_b = _two_group_steps(xa, xb, tt)
    seq_major = lambda tile: pl.BlockSpec((PERM_SEQS, tt, D_MODEL), lambda i: (*tile(i), 0))
    time_major = lambda tile: pl.BlockSpec((tt, PERM_SEQS, D_MODEL), lambda i: (*tile(i)[::-1], 0))
    slot_only = lambda tile: pl.BlockSpec((PERM_SEQS, 6, D_MODEL), lambda i: (tile(i)[0], 0, 0))
    resident = pl.Buffered(1)
    return pl.pallas_call(
        functools.partial(_s5_out_pair_kernel, tt=tt, f=f, n_first=na),
        grid=(na + nb,),
        in_specs=[time_major(tile_a), seq_major(tile_a), slot_only(tile_a),
                  time_major(tile_b), seq_major(tile_b), slot_only(tile_b),
                  pl.BlockSpec((None, D_MODEL, 2 * f), lambda i: (wl, 0, 0), pipeline_mode=resident),
                  pl.BlockSpec((None, f, D_MODEL), lambda i: (wl, 0, 0), pipeline_mode=resident),
                  pl.BlockSpec((1, D_MODEL), lambda i: (0, 0)),
                  pl.BlockSpec((1, D_MODEL), lambda i: (0, 0))],
        out_specs=[seq_major(tile_a), seq_major(tile_b)],
        out_shape=[jax.ShapeDtypeStruct(xa.shape, F32), jax.ShapeDtypeStruct(xb.shape, F32)],
        scratch_shapes=[pltpu.VMEM((D_MODEL, 2 * f), BF16), pltpu.VMEM((f, D_MODEL), BF16)],
        compiler_params=_params(1),
        name="s5_out",
    )(ya, xa, ma, yb, xb, mb, w_glu, w_out, ln_g.reshape(1, D_MODEL), ln_b.reshape(1, D_MODEL))


def _ffn_kernel(xp_ref, xs_ref, m_ref, wi_hbm, wo_hbm, lng_ref, lnb_ref, op_ref, os_ref,
                h_sc, wab_sc, wo_sc, stage_a, stage_b, stage_o, sem, *, f, wl, n_first):
    n_slab = f // MXU_TILE
    step = pl.program_id(0)
    m = m_ref[0]

    def slab_copies(k, slot):
        c0 = k * MXU_TILE
        return (pltpu.make_async_copy(wi_hbm.at[wl, :, pl.ds(c0, MXU_TILE)], stage_a.at[slot], sem.at[0, slot]),
                pltpu.make_async_copy(wi_hbm.at[wl, :, pl.ds(f + c0, MXU_TILE)], stage_b.at[slot], sem.at[1, slot]),
                pltpu.make_async_copy(wo_hbm.at[wl, pl.ds(c0, MXU_TILE), :], stage_o.at[slot], sem.at[2, slot]))

    def run(load_weights, x_ref, o_ref):
        h = x_ref[...] * (1.0 + m[4:5]) + m[3:4]
        h_sc[...] = h.astype(BF16)
        ahead = FFN_STAGE_SLOTS - 1
        if load_weights:
            for k in range(min(ahead, n_slab)):
                for cp in slab_copies(k, k % FFN_STAGE_SLOTS):
                    cp.start()
        acc = None
        for k in range(n_slab):
            c0 = k * MXU_TILE
            if load_weights:
                slot = k % FFN_STAGE_SLOTS
                if k + ahead < n_slab:
                    for cp in slab_copies(k + ahead, (k + ahead) % FFN_STAGE_SLOTS):
                        cp.start()
                for cp in slab_copies(k, slot):
                    cp.wait()
                wab_sc[:, c0:c0 + MXU_TILE] = stage_a[slot].astype(BF16)
                wab_sc[:, f + c0:f + c0 + MXU_TILE] = stage_b[slot].astype(BF16)
                wo_sc[c0:c0 + MXU_TILE, :] = stage_o[slot].astype(BF16)
            a = _dot(h_sc[...], wab_sc[:, c0:c0 + MXU_TILE])
            b = _dot(h_sc[...], wab_sc[:, f + c0:f + c0 + MXU_TILE])
            z = (a * _sigmoid(a)) * b
            part = _dot(z.astype(BF16), wo_sc[c0:c0 + MXU_TILE, :])
            acc = part if acc is None else acc + part
        r = DEEPNORM_ALPHA * x_ref[...] + m[5:6] * acc
        o_ref[...] = _layer_norm(r, lng_ref[...], lnb_ref[...])

    @pl.when(step == 0)
    def _():
        run(True, xp_ref, op_ref)

    @pl.when(jnp.logical_and(step > 0, step < n_first))
    def _():
        run(False, xp_ref, op_ref)

    @pl.when(step >= n_first)
    def _():
        run(False, xs_ref, os_ref)


def _ffn(xp, xs, mods, seq_s, w_in, w_out, wl, ln_g, ln_b, tm):
    f = w_out.shape[1]
    n_first = xp.shape[0] // tm
    n_second = xs.shape[0] // tm
    tiles_per_seq = seq_s // tm
    first_tile = lambda i: (jnp.minimum(i, n_first - 1), 0)
    second_tile = lambda i: (jnp.maximum(i - n_first, 0), 0)
    cond_row = lambda i: (jnp.where(i < n_first, 0, 1 + jnp.maximum(i - n_first, 0) // tiles_per_seq), 0, 0)
    return pl.pallas_call(
        functools.partial(_ffn_kernel, f=f, wl=wl, n_first=n_first),
        grid=(n_first + n_second,),
        in_specs=[pl.BlockSpec((tm, D_MODEL), first_tile),
                  pl.BlockSpec((tm, D_MODEL), second_tile),
                  pl.BlockSpec((1, 6, D_MODEL), cond_row),
                  pl.BlockSpec(memory_space=pl.ANY),
                  pl.BlockSpec(memory_space=pl.ANY),
                  pl.BlockSpec((1, D_MODEL), lambda i: (0, 0)),
                  pl.BlockSpec((1, D_MODEL), lambda i: (0, 0))],
        out_specs=[pl.BlockSpec((tm, D_MODEL), first_tile), pl.BlockSpec((tm, D_MODEL), second_tile)],
        out_shape=[jax.ShapeDtypeStruct(xp.shape, F32), jax.ShapeDtypeStruct(xs.shape, F32)],
        scratch_shapes=[pltpu.VMEM((tm, D_MODEL), BF16),
                        pltpu.VMEM((D_MODEL, 2 * f), BF16),
                        pltpu.VMEM((f, D_MODEL), BF16),
                        pltpu.VMEM((FFN_STAGE_SLOTS, D_MODEL, MXU_TILE), F32),
                        pltpu.VMEM((FFN_STAGE_SLOTS, D_MODEL, MXU_TILE), F32),
                        pltpu.VMEM((FFN_STAGE_SLOTS, MXU_TILE, D_MODEL), F32),
                        pltpu.SemaphoreType.DMA((3, FFN_STAGE_SLOTS))],
        compiler_params=_params(1),
        name="ffn",
    )(xp, xs, mods, w_in, w_out, ln_g.reshape(1, D_MODEL), ln_b.reshape(1, D_MODEL))


def _qkv_kernel(*refs, rope):
    if rope:
        (x_ref, m_ref, w_ref, qg_ref, kg_ref, cos_ref, sin_ref,
         wsw_ref, qgsw_ref, kgsw_ref) = refs[:10]
    else:
        x_ref, m_ref, w_ref, qg_ref, kg_ref = refs[:5]
    q_ref, k_ref, v_ref = refs[-3:]
    rows = x_ref.shape[0]
    m = m_ref[0]
    hb = (x_ref[...] * (1.0 + m[1:2]) + m[0:1]).astype(BF16)
    qkv = _dot(hb, w_ref[...])
    if rope:
        qk_sw = _dot(hb, wsw_ref[...])
        cos = cos_ref[...]
        sin = sin_ref[...]
    for hd in range(N_HEADS + N_KV_HEADS):
        xh = qkv[:, hd * HEAD_DIM:(hd + 1) * HEAD_DIM]
        gain = qg_ref[...] if hd < N_HEADS else kg_ref[...]
        inv = lax.rsqrt(jnp.mean(xh * xh, axis=-1, keepdims=True) + RMS_EPS)
        n = xh * inv * gain
        if rope:
            gain_sw = qgsw_ref[...] if hd < N_HEADS else kgsw_ref[...]
            n_sw = qk_sw[:, hd * HEAD_DIM:(hd + 1) * HEAD_DIM] * inv * gain_sw
            n = n * cos + n_sw * sin
        if hd < N_HEADS:
            q_ref[:, hd * HEAD_DIM:(hd + 1) * HEAD_DIM] = (n * Q_SCALE_LOG2).astype(BF16)
        else:
            k_ref[pl.ds(hd - N_HEADS, rows, stride=N_KV_HEADS), :] = n
    for g in range(N_KV_HEADS):
        v_ref[pl.ds(g, rows, stride=N_KV_HEADS), :] = (
            qkv[:, D_Q + D_KV + g * HEAD_DIM:D_Q + D_KV + (g + 1) * HEAD_DIM])


def _rope_tables(seq):
    pos = jnp.arange(seq, dtype=jnp.int32)
    row = (pos // GRID_W).astype(F32)
    col = (pos % GRID_W).astype(F32)
    inv = ROPE_THETA ** (-jnp.arange(AXIS_PAIRS, dtype=F32) / AXIS_PAIRS)
    ar = row[:, None] * inv
    ac = col[:, None] * inv
    cos = jnp.concatenate([jnp.cos(ar), jnp.cos(ar), jnp.cos(ac), jnp.cos(ac)], axis=-1)
    sin = jnp.concatenate([-jnp.sin(ar), jnp.sin(ar), -jnp.sin(ac), jnp.sin(ac)], axis=-1)
    return cos, sin


def _rope_partner(x):
    lead = x.shape[:-1]
    y = x.reshape(*lead, x.shape[-1] // (2 * AXIS_PAIRS), 2, AXIS_PAIRS)
    return y[..., ::-1, :].reshape(x.shape)


def _qkv_pair_kernel(*refs, n_first):
    (xp_ref, mp_ref, xs_ref, ms_ref, w32_ref, qg_ref, kg_ref, cos_ref, sin_ref,
     qgsw_ref, kgsw_ref) = refs[:11]
    qp_ref, kp_ref, vp_ref, qs_ref, ks_ref, vs_ref = refs[-8:-2]
    w_ref, wsw_ref = refs[-2:]

    @pl.when(pl.program_id(0) == 0)
    def _():
        lane = lax.broadcasted_iota(jnp.int32, (D_MODEL, HEAD_DIM), 1)
        first = jnp.bitwise_and(lane, AXIS_PAIRS) == 0
        for c0 in range(0, QKV_DIM, HEAD_DIM):
            w = w32_ref[:, c0:c0 + HEAD_DIM]
            w_ref[:, c0:c0 + HEAD_DIM] = w.astype(BF16)
            if c0 < D_Q + D_KV:
                up = pltpu.roll(w, HEAD_DIM - AXIS_PAIRS, 1)
                down = pltpu.roll(w, AXIS_PAIRS, 1)
                wsw_ref[:, c0:c0 + HEAD_DIM] = jnp.where(first, up, down).astype(BF16)

    @pl.when(pl.program_id(0) < n_first)
    def _():
        _qkv_kernel(xp_ref, mp_ref, w_ref, qg_ref, kg_ref, qp_ref, kp_ref, vp_ref, rope=False)

    @pl.when(pl.program_id(0) >= n_first)
    def _():
        _qkv_kernel(xs_ref, ms_ref, w_ref, qg_ref, kg_ref, cos_ref, sin_ref, wsw_ref, qgsw_ref, kgsw_ref,
                    qs_ref, ks_ref, vs_ref, rope=True)


def _qkv(xp, mp, seq_p, xs, ms, seq_s, w_qkv, wl, q_gain, k_gain, kv_prev):
    nbp, nbs = xp.shape[0] // seq_p, xs.shape[0] // seq_s
    tm_p, tm_s = seq_p, seq_s // 2
    nts = seq_s // tm_s
    n_first = nbp
    first = lambda i: jnp.minimum(i, n_first - 1)
    second = lambda i: jnp.maximum(i - n_first, 0)
    cos, sin = _rope_tables(seq_s)
    kv_layers = kv_prev[0].shape[1]
    in_specs = [pl.BlockSpec((tm_p, D_MODEL), lambda i: (first(i), 0)),
                pl.BlockSpec((1, 6, D_MODEL), lambda i: (0, 0, 0)),
                pl.BlockSpec((tm_s, D_MODEL), lambda i: (second(i), 0)),
                pl.BlockSpec((1, 6, D_MODEL), lambda i: (second(i) // nts, 0, 0)),
                pl.BlockSpec((None, D_MODEL, QKV_DIM), lambda i: (wl, 0, 0), pipeline_mode=pl.Buffered(1)),
                pl.BlockSpec((1, HEAD_DIM), lambda i: (0, 0)),
                pl.BlockSpec((1, HEAD_DIM), lambda i: (0, 0)),
                pl.BlockSpec((tm_s, HEAD_DIM), lambda i: (second(i) % nts, 0)),
                pl.BlockSpec((tm_s, HEAD_DIM), lambda i: (second(i) % nts, 0)),
                pl.BlockSpec((1, HEAD_DIM), lambda i: (0, 0)),
                pl.BlockSpec((1, HEAD_DIM), lambda i: (0, 0)),
                pl.BlockSpec(memory_space=pl.ANY),
                pl.BlockSpec(memory_space=pl.ANY)]
    args = [xp, mp, xs, ms, w_qkv, q_gain.reshape(1, HEAD_DIM), k_gain.reshape(1, HEAD_DIM), cos, sin,
            _rope_partner(q_gain).reshape(1, HEAD_DIM), _rope_partner(k_gain).reshape(1, HEAD_DIM),
            kv_prev[0], kv_prev[1]]
    kvp_spec = pl.BlockSpec((None, None, tm_p * N_KV_HEADS, HEAD_DIM), lambda i: (first(i), wl, 0, 0))
    kvs_spec = pl.BlockSpec((None, None, tm_s * N_KV_HEADS, HEAD_DIM),
                            lambda i: (second(i) // nts, 0, second(i) % nts, 0))
    kvp_shape = jax.ShapeDtypeStruct((nbp, kv_layers, seq_p * N_KV_HEADS, HEAD_DIM), F32)
    kvs_shape = jax.ShapeDtypeStruct((nbs, 1, seq_s * N_KV_HEADS, HEAD_DIM), F32)
    return pl.pallas_call(
        functools.partial(_qkv_pair_kernel, n_first=n_first),
        grid=(n_first + nbs * nts,),
        in_specs=in_specs,
        out_specs=[pl.BlockSpec((tm_p, D_Q), lambda i: (first(i), 0)), kvp_spec, kvp_spec,
                   pl.BlockSpec((tm_s, D_Q), lambda i: (second(i), 0)), kvs_spec, kvs_spec],
        out_shape=[jax.ShapeDtypeStruct((xp.shape[0], D_Q), BF16), kvp_shape, kvp_shape,
                   jax.ShapeDtypeStruct((xs.shape[0], D_Q), BF16), kvs_shape, kvs_shape],
        input_output_aliases={len(args) - 2: 1, len(args) - 1: 2},
        scratch_shapes=[pltpu.VMEM((D_MODEL, QKV_DIM), BF16), pltpu.VMEM((D_MODEL, D_Q + D_KV), BF16)],
        compiler_params=_params(1),
        name="qkv",
    )(*args)


def _attn_kernel(*refs, has_cache, stack, nbat):
    if has_cache:
        (q_ref, k_ref, v_ref, ck_ref, cv_ref, x_ref, m_ref, wo_ref, lng_ref, lnb_ref,
         o_ref, oh_sc) = refs
    else:
        q_ref, k_ref, v_ref, x_ref, m_ref, wo_ref, lng_ref, lnb_ref, o_ref, oh_sc = refs
    tq = q_ref.shape[0] // nbat
    for bb, g in [(bb, g) for bb in range(nbat) for g in range(N_KV_HEADS)]:
        rs = slice(bb * tq, (bb + 1) * tq)

        def head_rows(ref):
            rows = pl.ds(g, ref.shape[-2] // N_KV_HEADS, stride=N_KV_HEADS)
            return ref[bb, rows, :] if nbat > 1 else ref[rows, :]

        kg = head_rows(k_ref).astype(BF16)
        vg = head_rows(v_ref).astype(BF16)
        if has_cache:
            ckg = head_rows(ck_ref).astype(BF16)
            cvg = head_rows(cv_ref).astype(BF16)
        for r0 in range(0, KV_REP, stack):
            heads = [g * KV_REP + r0 + r for r in range(stack)]
            qh = jnp.concatenate([q_ref[rs, h * HEAD_DIM:(h + 1) * HEAD_DIM] for h in heads], axis=0)
            s1 = _dot_nt(qh, kg)
            mx = jnp.max(s1, axis=-1, keepdims=True)
            if has_cache:
                s2 = _dot_nt(qh, ckg)
                mx = jnp.maximum(mx, jnp.max(s2, axis=-1, keepdims=True))
            p1 = jnp.exp2(s1 - mx)
            den = jnp.sum(p1, axis=-1, keepdims=True)
            o = _dot(p1.astype(BF16), vg)
            if has_cache:
                p2 = jnp.exp2(s2 - mx)
                den = den + jnp.sum(p2, axis=-1, keepdims=True)
                o = o + _dot(p2.astype(BF16), cvg)
            o = (o / den).astype(BF16)
            for r, h in enumerate(heads):
                oh_sc[rs, h * HEAD_DIM:(h + 1) * HEAD_DIM] = o[r * tq:(r + 1) * tq]
    mix = _dot(oh_sc[...], wo_ref[...])
    m = m_ref[0]
    r = DEEPNORM_ALPHA * x_ref[...] + m[2:3] * mix
    o_ref[...] = _layer_norm(r, lng_ref[...], lnb_ref[...])


def _attention(q, k, v, kv_slot, cache_k, cache_v, layer_j, x, mods, w_o, ln_g, ln_b, nb, seq, tq, nbat=1):
    nt = seq // tq
    per_batch = mods.shape[0] > 1
    has_cache = cache_k is not None
    assert nbat == 1 or (nt == 1 and not per_batch and not has_cache and nb % nbat == 0)
    tok = pl.BlockSpec((nbat * tq, D_MODEL), lambda b, t: (b * nt + t, 0))
    kv_block = (None, None, seq * N_KV_HEADS, HEAD_DIM) if nbat == 1 else (nbat, None, seq * N_KV_HEADS, HEAD_DIM)
    kv = pl.BlockSpec(kv_block, lambda b, t: (b, kv_slot, 0, 0))
    in_specs, args = [tok, kv, kv], [q, k, v]
    if has_cache:
        past = cache_k.shape[2]
        cspec = pl.BlockSpec((None, None, past * N_KV_HEADS, HEAD_DIM), lambda b, t: (b, layer_j, 0, 0))
        in_specs += [cspec, cspec]
        flat = (cache_k.shape[0], cache_k.shape[1], past * N_KV_HEADS, HEAD_DIM)
        args += [cache_k.reshape(flat), cache_v.reshape(flat)]
    in_specs += [tok,
                 pl.BlockSpec((1, 6, D_MODEL), lambda b, t: (b if per_batch else 0, 0, 0)),
                 pl.BlockSpec((None, D_Q, D_MODEL), lambda b, t: (layer_j, 0, 0)),
                 pl.BlockSpec((1, D_MODEL), lambda b, t: (0, 0)),
                 pl.BlockSpec((1, D_MODEL), lambda b, t: (0, 0))]
    args += [x, mods, w_o, ln_g.reshape(1, D_MODEL), ln_b.reshape(1, D_MODEL)]
    return pl.pallas_call(
        functools.partial(_attn_kernel, has_cache=has_cache, stack=1 if has_cache else KV_REP, nbat=nbat),
        grid=(nb // nbat, nt),
        in_specs=in_specs,
        out_specs=tok,
        out_shape=jax.ShapeDtypeStruct((nb * seq, D_MODEL), F32),
        scratch_shapes=[pltpu.VMEM((nbat * tq, D_Q), BF16)],
        compiler_params=_params(2),
        name="attn_cache" if has_cache else "attn",
    )(*args)


def _latent_h0(st):
    f = st[:, 0].transpose(2, 1, 0, 3)
    b = st[:, 1].transpose(2, 1, 0, 3)
    z = jnp.zeros_like(f)
    even = jnp.concatenate([f, z], axis=-1)
    odd = jnp.concatenate([z, b], axis=-1)
    h0 = jnp.stack([even, odd], axis=3)
    return h0.reshape(N_GROUPS, 2, 2 * st.shape[0], 2 * STATE_DIM)


def kernel(x_prompt, x_sample, c, cache_k, cache_v, state_s5, c_ctx, w_mod, b_mod, ln_g, ln_b, w_s5_in, s5_a_re, s5_a_im, s5_log_dt, s5_b_re, s5_b_im, s5_c_re, s5_c_im, s5_d, w_s5_glu, w_s5_out, w_qkv, q_norm_g, k_norm_g, w_o, w_ffn_in, w_ffn_out):
    nbp, seqp, _ = x_prompt.shape
    nbs, seqs, _ = x_sample.shape
    xp = x_prompt.reshape(nbp * seqp, D_MODEL)
    xs = x_sample.reshape(nbs * seqs, D_MODEL)
    s5_slots_s, s5_seq_s = 2 * nbs, seqs // 2

    cond = jnp.concatenate([c_ctx[None, :], c, jnp.zeros((8 - 1 - nbs, D_MODEL), F32)], axis=0)
    mods = _adaln(cond, w_mod, b_mod)
    mods_all = mods.reshape(DEPTH, 8, 6, D_MODEL)
    mods_p = mods[:, 0:1].reshape(DEPTH, 1, 6, D_MODEL)
    mods_s = mods[:, 1:1 + nbs].reshape(DEPTH, nbs, 6, D_MODEL)

    w_in, w_glu, w_out = w_s5_in, w_s5_glu, w_s5_out
    wo = w_o.astype(BF16)

    kv_rows = (nbp, DEPTH // 2, seqp * N_KV_HEADS, HEAD_DIM)
    new_kv, new_s = (jnp.zeros(kv_rows, F32), jnp.zeros(kv_rows, F32)), []
    for layer in range(DEPTH):
        j = layer // 2
        mp, ms = mods_p[layer], mods_s[layer]
        lg0, lb0, lg1, lb1 = ln_g[layer, 0], ln_b[layer, 0], ln_g[layer, 1], ln_b[layer, 1]
        if layer % 2 == 0:
            w1, w2, lam = _s5_tables(s5_a_re[j], s5_a_im[j], s5_log_dt[j], s5_b_re[j], s5_b_im[j],
                                     s5_c_re[j], s5_c_im[j])
            mp_slots = jnp.broadcast_to(mp, (nbp, 6, D_MODEL))
            ms_slots = jnp.repeat(ms, 2, axis=0)
            xp3 = xp.reshape(nbp, seqp, D_MODEL)
            xs3 = xs.reshape(s5_slots_s, s5_seq_s, D_MODEL)
            up, us = _s5_in(xp3, mp_slots, xs3, ms_slots, w_in, j, 128)
            yp, fin = _s5_core(up, w1, w2, lam, s5_d[j], None, True)
            ys, _ = _s5_core(us, w1, w2, lam, s5_d[j], _latent_h0(state_s5[:, j]), False)
            fin = fin.reshape(N_GROUPS, 2, nbp, 2, STATE_DIM)
            new_s.append(fin.transpose(2, 3, 1, 0, 4))
            xp, xs = _s5_out(yp, xp3, mp_slots, ys, xs3, ms_slots, w_glu, w_out, j, lg0, lb0, 64)
            xp = xp.reshape(nbp * seqp, D_MODEL)
            xs = xs.reshape(nbs * seqs, D_MODEL)
        else:
            qp, kp, vp, qs, ks, vs = _qkv(xp, mp, seqp, xs, ms, seqs, w_qkv, j,
                                          q_norm_g[j], k_norm_g[j], new_kv)
            new_kv = (kp, vp)
            xp = _attention(qp, kp, vp, j, None, None, j, xp, mp, wo, lg0, lb0, nbp, seqp, seqp, nbat=4)
            xs = _attention(qs, ks, vs, 0, cache_k, cache_v, j, xs, ms, wo, lg0, lb0, nbs, seqs, 512)
        xp, xs = _ffn(xp, xs, mods_all[layer], seqs, w_ffn_in, w_ffn_out, layer, lg1, lb1, 512)

    y_prompt = xp.reshape(nbp, seqp, D_MODEL)
    y_sample = xs.reshape(nbs, seqs, D_MODEL)
    kv_out = (nbp, DEPTH // 2, seqp, N_KV_HEADS, HEAD_DIM)
    return (y_prompt, y_sample, new_kv[0].reshape(kv_out), new_kv[1].reshape(kv_out),
            jnp.stack(new_s, axis=1))
```
